```python
import math
import jax, jax.numpy as jnp
from jax import lax
import numpy as np

D_MODEL = 2048
BATCH = 4
SEQ = 2048
DEPTH = 2

D_MIX = D_MODEL
D_GROUP = D_MIX // 4
D_FF = 4 * D_MODEL
ROPE_THETA = 10000.0
EPS = 1e-6
Q_BLOCK = 128
NEG_INF = -1e30

S5_CH = D_GROUP
S5_GROUP = 16
S5_G = S5_CH // S5_GROUP
S5_P = 64
S5_DT_MIN = 0.001
S5_DT_MAX = 0.1

RET_HEADS = 4
RET_DK = D_GROUP // RET_HEADS
RET_CHUNK = 128

NSA_HEADS = 4
NSA_DH = D_GROUP // NSA_HEADS
NSA_CMP_LEN = 32
NSA_CMP_STRIDE = 16
NSA_SEL_LEN = 64
NSA_TOPK = 8
NSA_WINDOW = 256
NSA_FORCE_BONUS = 1e4

DIFF_HEADS = 4
DIFF_DV = D_GROUP // DIFF_HEADS
DIFF_DH = DIFF_DV // 2

SPLIT_WIDTHS = (
    S5_CH,
    D_GROUP, D_GROUP, D_GROUP, D_GROUP,
    D_GROUP,
    NSA_DH, NSA_DH, NSA_DH, NSA_DH, NSA_DH, NSA_DH,
    NSA_HEADS * 3,
    D_GROUP, D_GROUP, D_GROUP,
)
N_IN = sum(SPLIT_WIDTHS)

kernel_name = "hybrid_s5_retnet_nsa_diffattn_block"


def _rms(x, gain=None):
    xf = x.astype(jnp.float32)
    y = xf * lax.rsqrt(jnp.mean(xf * xf, axis=-1, keepdims=True) + EPS)
    if gain is not None:
        y = y * gain.astype(jnp.float32)
    return y.astype(x.dtype)


def _rope(x, pos):
    d = x.shape[-1]
    inv = ROPE_THETA ** (-jnp.arange(0, d, 2, dtype=jnp.float32) / d)
    ang = pos.astype(jnp.float32)[..., None] * inv
    cos = jnp.cos(ang)[:, :, None, :]
    sin = jnp.sin(ang)[:, :, None, :]
    xf = x.astype(jnp.float32)
    x1, x2 = xf[..., : d // 2], xf[..., d // 2:]
    return jnp.concatenate([x1 * cos - x2 * sin, x2 * cos + x1 * sin], axis=-1).astype(x.dtype)


def _masked_softmax(s, mask):
    p = jax.nn.softmax(jnp.where(mask, s, NEG_INF), axis=-1)
    return jnp.where(mask, p, 0.0)


def _s5_combine(e1, e2):
    a1r, a1i, b1r, b1i = e1
    a2r, a2i, b2r, b2i = e2
    return (a2r * a1r - a2i * a1i,
            a2r * a1i + a2i * a1r,
            a2r * b1r - a2i * b1i + b2r,
            a2r * b1i + a2i * b1r + b2i)


def _s5_mixer(u, lam_re, lam_im, log_dt, b_re, b_im, c_re, c_im, d_skip, w_glu):
    bsz, seq, _ = u.shape
    f32 = jnp.float32
    ug = u.astype(f32).reshape(bsz, seq, S5_G, S5_GROUP)
    lam_re = lam_re.astype(f32)
    lam_im = lam_im.astype(f32)
    dt = jnp.exp(log_dt.astype(f32))[:, None]
    mag = jnp.exp(lam_re * dt)
    lb_re = mag * jnp.cos(lam_im * dt)
    lb_im = mag * jnp.sin(lam_im * dt)
    den = lam_re * lam_re + lam_im * lam_im
    f_re = ((lb_re - 1.0) * lam_re + lb_im * lam_im) / den
    f_im = (lb_im * lam_re - (lb_re - 1.0) * lam_im) / den
    b_re = b_re.astype(f32)
    b_im = b_im.astype(f32)
    bb_re = f_re[..., None] * b_re - f_im[..., None] * b_im
    bb_im = f_re[..., None] * b_im + f_im[..., None] * b_re
    bu_re = jnp.einsum('blgh,gph->blgp', ug, bb_re)
    bu_im = jnp.einsum('blgh,gph->blgp', ug, bb_im)
    a_re = jnp.broadcast_to(lb_re, bu_re.shape)
    a_im = jnp.broadcast_to(lb_im, bu_im.shape)
    _, _, s_re, s_im = lax.associative_scan(_s5_combine, (a_re, a_im, bu_re, bu_im), axis=1)
    y = (jnp.einsum('gnp,blgp->blgn', c_re.astype(f32), s_re)
         - jnp.einsum('gnp,blgp->blgn', c_im.astype(f32), s_im)
         + d_skip.astype(f32) * ug)
    y = jax.nn.gelu(y.reshape(bsz, seq, S5_CH)).astype(u.dtype)
    return y * jax.nn.sigmoid(y @ w_glu)


def _retention(q, k, v, g, pos):
    bsz, seq, _ = q.shape
    H, dk, C = RET_HEADS, RET_DK, RET_CHUNK
    n_chunk = seq // C
    q = _rope(q.reshape(bsz, seq, H, dk), pos)
    k = _rope(k.reshape(bsz, seq, H, dk), pos) * (dk ** -0.5)
    v = v.reshape(bsz, seq, H, dk)
    log_g = jnp.log(1.0 - 2.0 ** (-5.0 - jnp.arange(H, dtype=jnp.float32)))
    qc = q.reshape(bsz, n_chunk, C, H, dk)
    kc = k.reshape(bsz, n_chunk, C, H, dk)
    vc = v.reshape(bsz, n_chunk, C, H, dk)
    idx = jnp.arange(C, dtype=jnp.float32)
    diff = idx[:, None] - idx[None, :]
    dmat = jnp.where(diff >= 0, jnp.exp(jnp.maximum(diff, 0.0)[None] * log_g[:, None, None]), 0.0)
    inner = jnp.einsum('bnqhd,bnkhd->bnhqk', qc, kc) * dmat[None, None]
    o_inner = jnp.einsum('bnhqk,bnkhe->bnqhe', inner, vc)
    zeta = jnp.exp((C - 1.0 - idx)[None, :] * log_g[:, None])
    kv = jnp.einsum('bnkhd,hk,bnkhe->nbhde', kc, zeta, vc)
    g_chunk = jnp.exp(C * log_g)[None, :, None, None]

    def step(r, kv_i):
        return r * g_chunk + kv_i, r

    _, r_prev = lax.scan(step, jnp.zeros((bsz, H, dk, dk), kv.dtype), kv)
    xi = jnp.exp((idx + 1.0)[None, :] * log_g[:, None])
    o_cross = jnp.einsum('bnqhd,nbhde,hq->bnqhe', qc, r_prev, xi)
    o = (o_inner + o_cross).reshape(bsz, seq, H, dk).astype(jnp.float32)
    mu = jnp.mean(o, axis=-1, keepdims=True)
    var = jnp.mean(jnp.square(o - mu), axis=-1, keepdims=True)
    o = ((o - mu) * lax.rsqrt(var + EPS)).reshape(bsz, seq, H * dk).astype(g.dtype)
    return jax.nn.silu(g) * o


def _nsa(q, kc, vc, ks, vs, kw, vw, gate, pos, pe_k, pe_v, w_cmp_k, w_cmp_v):
    bsz, seq, _ = q.shape
    H, dh, QB = NSA_HEADS, NSA_DH, Q_BLOCK
    scale = dh ** -0.5
    nb = seq // QB
    t = jnp.arange(seq)
    q = _rope(_rms(q.reshape(bsz, seq, H, dh)), pos)

    n_cmp = (seq - NSA_CMP_LEN) // NSA_CMP_STRIDE + 1
    blk_idx = np.arange(n_cmp)[:, None] * NSA_CMP_STRIDE + np.arange(NSA_CMP_LEN)[None, :]
    kc_r = _rope(kc[:, :, None, :], pos)[:, :, 0]
    k_cmp = _rms((kc_r[:, blk_idx] + pe_k).reshape(bsz, n_cmp, NSA_CMP_LEN * dh) @ w_cmp_k)
    v_cmp = (vc[:, blk_idx] + pe_v).reshape(bsz, n_cmp, NSA_CMP_LEN * dh) @ w_cmp_v
    mask_c = jnp.asarray(blk_idx[:, -1])[None, :] <= t[:, None]
    s_c = jnp.einsum('bqhd,bjd->bhqj', q, k_cmp).astype(jnp.float32) * scale
    p_cmp = _masked_softmax(s_c, mask_c[None, None])
    o_cmp = jnp.einsum('bhqj,bjd->bqhd', p_cmp.astype(v_cmp.dtype), v_cmp)

    n_sel = seq // NSA_SEL_LEN
    topk = min(NSA_TOPK, n_sel)
    cmp_start = np.arange(n_cmp) * NSA_CMP_STRIDE
    sel_start = np.arange(n_sel) * NSA_SEL_LEN
    overlap = ((cmp_start[None, :] < sel_start[:, None] + NSA_SEL_LEN)
               & (cmp_start[None, :] + NSA_CMP_LEN > sel_start[:, None])).astype(np.float32)
    imp = jnp.einsum('bhqi,ji->bqj', p_cmp, jnp.asarray(overlap))
    cur = t // NSA_SEL_LEN
    jj = jnp.arange(n_sel)
    valid = jj[None, :] <= cur[:, None]
    forced = (jj[None, :] == 0) | (jj[None, :] == cur[:, None]) | (jj[None, :] == cur[:, None] - 1)
    score = jnp.where(valid, imp + jnp.where(forced, NSA_FORCE_BONUS, 0.0), NEG_INF)
    _, sel = lax.top_k(score, topk)
    ks_blk = _rope(_rms(ks)[:, :, None, :], pos)[:, :, 0].reshape(bsz, n_sel, NSA_SEL_LEN, dh)
    vs_blk = vs.reshape(bsz, n_sel, NSA_SEL_LEN, dh)
    b_idx = jnp.arange(bsz)[:, None, None]
    l_idx = jnp.arange(NSA_SEL_LEN)

    def sel_block(args):
        q_b, sel_b, t_b = args
        kg = ks_blk[b_idx, sel_b]
        vg = vs_blk[b_idx, sel_b]
        s = jnp.einsum('bqhd,bqkld->bhqkl', q_b, kg).astype(jnp.float32) * scale
        kpos = sel_b[..., None] * NSA_SEL_LEN + l_idx
        m = (kpos <= t_b[None, :, None, None]).reshape(bsz, 1, QB, topk * NSA_SEL_LEN)
        p = _masked_softmax(s.reshape(bsz, H, QB, topk * NSA_SEL_LEN), m).reshape(s.shape)
        return jnp.einsum('bhqkl,bqkld->bqhd', p.astype(vg.dtype), vg)

    q_blocks = q.reshape(bsz, nb, QB, H, dh).transpose(1, 0, 2, 3, 4)
    sel_blocks = sel.reshape(bsz, nb, QB, topk).transpose(1, 0, 2, 3)
    o_sel = lax.map(sel_block, (q_blocks, sel_blocks, t.reshape(nb, QB)))
    o_sel = o_sel.transpose(1, 0, 2, 3, 4).reshape(bsz, seq, H, dh)

    n_prev = NSA_WINDOW // QB
    kw_r = _rope(_rms(kw)[:, :, None, :], pos)[:, :, 0]
    kw_pad = jnp.pad(kw_r, ((0, 0), (n_prev * QB, 0), (0, 0))).reshape(bsz, nb + n_prev, QB, dh)
    vw_pad = jnp.pad(vw, ((0, 0), (n_prev * QB, 0), (0, 0))).reshape(bsz, nb + n_prev, QB, dh)
    k_band = jnp.concatenate([kw_pad[:, i:i + nb] for i in range(n_prev + 1)], axis=2)
    v_band = jnp.concatenate([vw_pad[:, i:i + nb] for i in range(n_prev + 1)], axis=2)
    qb = q.reshape(bsz, nb, QB, H, dh)
    s_w = jnp.einsum('bnqhd,bnkd->bnhqk', qb, k_band).astype(jnp.float32) * scale
    qpos = t.reshape(nb, QB)
    kpos = (jnp.arange(nb)[:, None] - n_prev) * QB + jnp.arange((n_prev + 1) * QB)[None, :]
    dist = qpos[:, :, None] - kpos[:, None, :]
    m_w = (dist >= 0) & (dist < NSA_WINDOW) & (kpos[:, None, :] >= 0)
    p_w = _masked_softmax(s_w, m_w[None, :, None])
    o_win = jnp.einsum('bnhqk,bnkd->bnqhd', p_w.astype(v_band.dtype), v_band).reshape(bsz, seq, H, dh)

    gts = jax.nn.sigmoid(gate.reshape(bsz, seq, H, 3))
    o = gts[..., 0:1] * o_cmp + gts[..., 1:2] * o_sel + gts[..., 2:3] * o_win
    return o.reshape(bsz, seq, H * dh)


def _diff_attn(q, k, v, pos, lq1, lk1, lq2, lk2, lam_init):
    bsz, seq, _ = q.shape
    H, dh, dv, QB = DIFF_HEADS, DIFF_DH, DIFF_DV, Q_BLOCK
    nb = seq // QB
    scale = dh ** -0.5
    q = _rope(_rms(q.reshape(bsz, seq, 2 * H, dh)), pos).reshape(bsz, seq, H, 2, dh)
    k = _rope(_rms(k.reshape(bsz, seq, 2 * H, dh)), pos).reshape(bsz, seq, H, 2, dh)
    v = v.reshape(bsz, seq, H, dv)
    lam = (jnp.exp(jnp.sum(lq1.astype(jnp.float32) * lk1.astype(jnp.float32)))
           - jnp.exp(jnp.sum(lq2.astype(jnp.float32) * lk2.astype(jnp.float32))) + lam_init)
    kpos = jnp.arange(seq)

    def block(args):
        q_b, t_b = args
        s = jnp.einsum('bqhcd,bkhcd->bhcqk', q_b, k).astype(jnp.float32) * scale
        p = _masked_softmax(s, kpos[None, :] <= t_b[:, None])
        a = p[:, :, 0] - lam * p[:, :, 1]
        return jnp.einsum('bhqk,bkhe->bqhe', a.astype(v.dtype), v)

    q_blocks = q.reshape(bsz, nb, QB, H, 2, dh).transpose(1, 0, 2, 3, 4, 5)
    o = lax.map(block, (q_blocks, kpos.reshape(nb, QB)))
    o = o.transpose(1, 0, 2, 3, 4).reshape(bsz, seq, H, dv)
    o = _rms(o) * (1.0 - lam_init)
    return o.reshape(bsz, seq, H * dv)


def setup_inputs(seed: int = 0) -> dict:
    key = jax.random.key(seed)
    ks = jax.random.split(key, 32)
    f32 = jnp.float32
    nrm = lambda k, shape, s: jax.random.normal(k, shape, f32) * s
    x = jax.random.normal(ks[0], (BATCH, SEQ, D_MODEL), f32)
    positions = (jnp.arange(SEQ, dtype=jnp.int32)[None, :]
                 + jax.random.randint(ks[1], (BATCH, 1), 0, 1024, dtype=jnp.int32))
    norm1_g = 1.0 + nrm(ks[2], (DEPTH, D_MODEL), 0.02)
    w_in = nrm(ks[3], (DEPTH, D_MODEL, N_IN), D_MODEL ** -0.5)
    s5_lambda_re = -0.5 + nrm(ks[4], (DEPTH, S5_G, S5_P), 0.01)
    s5_lambda_im = math.pi * jnp.arange(S5_P, dtype=f32)[None, None, :] + nrm(ks[5], (DEPTH, S5_G, S5_P), 0.01)
    s5_log_dt = jax.random.uniform(ks[6], (DEPTH, S5_G), f32, math.log(S5_DT_MIN), math.log(S5_DT_MAX))
    s5_b_re = nrm(ks[7], (DEPTH, S5_G, S5_P, S5_GROUP), (2 * S5_GROUP) ** -0.5)
    s5_b_im = nrm(ks[8], (DEPTH, S5_G, S5_P, S5_GROUP), (2 * S5_GROUP) ** -0.5)
    s5_c_re = nrm(ks[9], (DEPTH, S5_G, S5_GROUP, S5_P), (2 * S5_P) ** -0.5)
    s5_c_im = nrm(ks[10], (DEPTH, S5_G, S5_GROUP, S5_P), (2 * S5_P) ** -0.5)
    s5_d = nrm(ks[11], (DEPTH, S5_G, S5_GROUP), 1.0)
    s5_w_glu = nrm(ks[12], (DEPTH, S5_CH, S5_CH), S5_CH ** -0.5)
    nsa_pe_k = nrm(ks[13], (DEPTH, NSA_CMP_LEN, NSA_DH), 0.1)
    nsa_pe_v = nrm(ks[14], (DEPTH, NSA_CMP_LEN, NSA_DH), 0.1)
    nsa_w_cmp_k = nrm(ks[15], (DEPTH, NSA_CMP_LEN * NSA_DH, NSA_DH), (NSA_CMP_LEN * NSA_DH) ** -0.5)
    nsa_w_cmp_v = nrm(ks[16], (DEPTH, NSA_CMP_LEN * NSA_DH, NSA_DH), (NSA_CMP_LEN * NSA_DH) ** -0.5)
    diff_lq1 = nrm(ks[17], (DEPTH, DIFF_DH), 0.1)
    diff_lk1 = nrm(ks[18], (DEPTH, DIFF_DH), 0.1)
    diff_lq2 = nrm(ks[19], (DEPTH, DIFF_DH), 0.1)
    diff_lk2 = nrm(ks[20], (DEPTH, DIFF_DH), 0.1)
    w_out = nrm(ks[21], (DEPTH, D_MIX, D_MODEL), D_MIX ** -0.5)
    norm2_g = 1.0 + nrm(ks[22], (DEPTH, D_MODEL), 0.02)
    mlp_w1 = nrm(ks[23], (DEPTH, D_MODEL, D_FF), D_MODEL ** -0.5)
    mlp_w2 = nrm(ks[24], (DEPTH, D_FF, D_MODEL), D_FF ** -0.5)
    return {"x": x, "positions": positions, "norm1_g": norm1_g, "w_in": w_in,
            "s5_lambda_re": s5_lambda_re, "s5_lambda_im": s5_lambda_im, "s5_log_dt": s5_log_dt,
            "s5_b_re": s5_b_re, "s5_b_im": s5_b_im, "s5_c_re": s5_c_re, "s5_c_im": s5_c_im,
            "s5_d": s5_d, "s5_w_glu": s5_w_glu, "nsa_pe_k": nsa_pe_k, "nsa_pe_v": nsa_pe_v,
            "nsa_w_cmp_k": nsa_w_cmp_k, "nsa_w_cmp_v": nsa_w_cmp_v,
            "diff_lq1": diff_lq1, "diff_lk1": diff_lk1, "diff_lq2": diff_lq2, "diff_lk2": diff_lk2,
            "w_out": w_out, "norm2_g": norm2_g, "mlp_w1": mlp_w1, "mlp_w2": mlp_w2}


def reference(x, positions, norm1_g, w_in, s5_lambda_re, s5_lambda_im, s5_log_dt,
              s5_b_re, s5_b_im, s5_c_re, s5_c_im, s5_d, s5_w_glu, nsa_pe_k, nsa_pe_v,
              nsa_w_cmp_k, nsa_w_cmp_v, diff_lq1, diff_lk1, diff_lq2, diff_lk2,
              w_out, norm2_g, mlp_w1, mlp_w2):
    split_points = []
    acc = 0
    for w in SPLIT_WIDTHS[:-1]:
        acc += w
        split_points.append(acc)
    for layer in range(DEPTH):
        h = _rms(x, norm1_g[layer])
        (u, rq, rk, rv, rg, nq, nkc, nvc, nks, nvs, nkw, nvw, ngate,
         dq, dk, dv) = jnp.split(h @ w_in[layer], split_points, axis=-1)
        y_a = _s5_mixer(u, s5_lambda_re[layer], s5_lambda_im[layer], s5_log_dt[layer],
                        s5_b_re[layer], s5_b_im[layer], s5_c_re[layer], s5_c_im[layer],
                        s5_d[layer], s5_w_glu[layer])
        y_b = _retention(rq, rk, rv, rg, positions)
        y_c = _nsa(nq, nkc, nvc, nks, nvs, nkw, nvw, ngate, positions,
                   nsa_pe_k[layer], nsa_pe_v[layer], nsa_w_cmp_k[layer], nsa_w_cmp_v[layer])
        lam_init = 0.8 - 0.6 * math.exp(-0.3 * layer)
        y_d = _diff_attn(dq, dk, dv, positions, diff_lq1[layer], diff_lk1[layer],
                         diff_lq2[layer], diff_lk2[layer], lam_init)
        mixed = jnp.concatenate([y_a.astype(x.dtype), y_b.astype(x.dtype),
                                 y_c.astype(x.dtype), y_d.astype(x.dtype)], axis=-1)
        x = x + mixed @ w_out[layer]
        h2 = _rms(x, norm2_g[layer])
        x = x + jnp.square(jax.nn.relu(h2 @ mlp_w1[layer])) @ mlp_w2[layer]
    return x
```

```python
import functools
import math

import jax
import jax.numpy as jnp
import numpy as np
from jax import lax
from jax.experimental import pallas as pl
from jax.experimental.pallas import tpu as pltpu

F32 = jnp.float32
BF16 = jnp.bfloat16
HIGHEST = lax.Precision.HIGHEST

D_MODEL = 2048
D_GROUP = 512
D_FF = 4 * D_MODEL
ROPE_THETA = 10000.0
EPS = 1e-6
Q_BLOCK = 128
NEG_INF = -1e30

S5_GROUP = 16
S5_G = 32
S5_P = 64
S5_CHUNK = 16

RET_HEADS = 4
RET_DK = 128
RET_CHUNK = 128

NSA_HEADS = 4
NSA_DH = 128
NSA_CMP_LEN = 32
NSA_CMP_STRIDE = 16
NSA_SEL_LEN = 64
NSA_TOPK = 8
NSA_WINDOW = 256
NSA_FORCE_BONUS = 1e4
NSA_SEL_SHIFT = 6

DIFF_HEADS = 4
DIFF_DH = 64

LANE = 128
N_IN_RAW = 5388
GATE_RAW_OFF = 3840
GATE_RAW_END = 3852
N_IN_PAD = 44 * LANE
CB_U, CB_RQ, CB_RK, CB_RV, CB_RG, CB_NQ = 0, 4, 8, 12, 16, 20
CB_NKC, CB_NVC, CB_NKS, CB_NVS, CB_NKW, CB_NVW, CB_GATE = 24, 25, 26, 27, 28, 29, 30
CB_DQ, CB_DK, CB_DV = 32, 36, 40

VMEM_LIMIT = 56 * 1024 * 1024


def _cparams(sem):
    return pltpu.CompilerParams(dimension_semantics=sem, vmem_limit_bytes=VMEM_LIMIT)


def _dot(a, b, **kw):
    return jnp.dot(a, b, preferred_element_type=F32, **kw)


def _dot_nt(a, b):
    return lax.dot_general(a, b, (((1,), (1,)), ((), ())), preferred_element_type=F32)


def _dot_tn(a, b):
    return lax.dot_general(a, b, (((0,), (0,)), ((), ())), preferred_element_type=F32)


def _rms_rows(x):
    return x * lax.rsqrt(jnp.mean(x * x, axis=-1, keepdims=True) + EPS)


def _rope128(x, cos, sin_signed):
    return x * cos + pltpu.roll(x, 64, axis=1) * sin_signed


def _inproj_kernel(x_ref, g_ref, w_ref, o_ref, xn_ref):
    @pl.when(pl.program_id(1) == 0)
    def _():
        x = x_ref[...]
        xn_ref[...] = (_rms_rows(x) * g_ref[...]).astype(BF16)

    o_ref[...] = _dot(xn_ref[...], w_ref[...])


def _inproj(x2, g, w_pad, tm, tn):
    t, d = x2.shape
    n = w_pad.shape[1]
    return pl.pallas_call(
        _inproj_kernel,
        grid=(t // tm, n // tn),
        in_specs=[pl.BlockSpec((tm, d), lambda i, j: (i, 0)),
                  pl.BlockSpec((1, d), lambda i, j: (0, 0)),
                  pl.BlockSpec((d, tn), lambda i, j: (0, j))],
        out_specs=pl.BlockSpec((tm, tn), lambda i, j: (i, j)),
        out_shape=jax.ShapeDtypeStruct((t, n), F32),
        scratch_shapes=[pltpu.VMEM((tm, d), BF16)],
        compiler_params=_cparams(("arbitrary", "arbitrary")),
        name="inproj",
    )(x2, g, w_pad)


def _s5_tables(lam_re, lam_im, log_dt, b_re, b_im, c_re, c_im, d_skip, n_chunk):
    tc = S5_CHUNK
    dt = jnp.exp(log_dt.astype(F32))[:, None]
    lam_re = lam_re.astype(F32)
    lam_im = lam_im.astype(F32)
    lre = lam_re * dt
    lim = lam_im * dt

    def lbpow(k):
        k = jnp.asarray(k, F32)[..., None, None]
        mag = jnp.exp(lre * k)
        return mag * jnp.cos(lim * k), mag * jnp.sin(lim * k)

    lb_re, lb_im = lbpow(1.0)
    den = lam_re * lam_re + lam_im * lam_im
    f_re = ((lb_re - 1.0) * lam_re + lb_im * lam_im) / den
    f_im = (lb_im * lam_re - (lb_re - 1.0) * lam_im) / den
    b_re = b_re.astype(F32)
    b_im = b_im.astype(F32)
    bb_re = f_re[..., None] * b_re - f_im[..., None] * b_im
    bb_im = f_re[..., None] * b_im + f_im[..., None] * b_re
    c_re = c_re.astype(F32)
    c_im = c_im.astype(F32)

    pr, pi = lbpow(jnp.arange(tc + 1))
    lb_b_re = pr[..., None] * bb_re[None] - pi[..., None] * bb_im[None]
    lb_b_im = pr[..., None] * bb_im[None] + pi[..., None] * bb_re[None]
    kk = (jnp.einsum('gnp,kgph->gknh', c_re, lb_b_re, precision=HIGHEST)
          - jnp.einsum('gnp,kgph->gknh', c_im, lb_b_im, precision=HIGHEST))
    ss = np.arange(tc)[:, None]
    tt = np.arange(tc)[None, :]
    lag = np.clip(tt - ss, 0, tc)
    toe = kk[:, lag]
    toe = jnp.where(jnp.asarray(tt >= ss)[None, :, :, None, None], toe, 0.0)
    g = lam_re.shape[0]
    w_intra = toe.transpose(0, 1, 4, 2, 3).reshape(g, tc * S5_GROUP, tc * S5_GROUP)

    rev = np.arange(tc)[::-1].copy()
    st_re = lb_b_re[rev]
    st_im = lb_b_im[rev]
    w_state = jnp.stack([st_re, st_im], axis=0).transpose(2, 1, 4, 0, 3)
    w_state = w_state.reshape(g, tc * S5_GROUP, 2 * S5_P)

    ar = pr[1:]
    ai = pi[1:]
    o_re = c_re[None] * ar[:, :, None, :] - c_im[None] * ai[:, :, None, :]
    o_im = -c_re[None] * ai[:, :, None, :] - c_im[None] * ar[:, :, None, :]
    w_out = jnp.stack([o_re, o_im], axis=0).transpose(2, 0, 4, 1, 3)
    w_out = w_out.reshape(g, 2 * S5_P, tc * S5_GROUP)

    n_step = int(math.log2(n_chunk))
    sr, si = lbpow(tc * (2.0 ** jnp.arange(n_step)))
    row_a = jnp.concatenate([sr, sr], axis=-1)
    row_b = jnp.concatenate([-si, si], axis=-1)
    apow = jnp.stack([row_a, row_b], axis=1).reshape(2 * n_step, g, 2 * S5_P).transpose(1, 0, 2)
    apow = jnp.pad(apow, ((0, 0), (0, 16 - 2 * n_step), (0, 0)))
    d_t = jnp.tile(d_skip.astype(F32), (1, tc))[:, None, :]
    return w_intra, w_state, w_out, apow, d_t


def _s5_kernel(u_ref, wi_ref, ws_ref, wo_ref, ap_ref, d_ref, y_ref, *, n_chunk):
    u = u_ref[0]
    s = _dot(u, ws_ref[0], precision=HIGHEST)
    row = lax.broadcasted_iota(jnp.int32, s.shape, 0) & (n_chunk - 1)
    x = s
    n_step = int(math.log2(n_chunk))
    for k in range(n_step):
        sh = 1 << k
        xs = jnp.where(row >= sh, pltpu.roll(x, sh, axis=0), 0.0)
        a1 = ap_ref[0, 2 * k:2 * k + 1, :]
        a2 = ap_ref[0, 2 * k + 1:2 * k + 2, :]
        x = x + xs * a1 + pltpu.roll(xs, S5_P, axis=1) * a2
    xp = jnp.where(row >= 1, pltpu.roll(x, 1, axis=0), 0.0)
    y = (_dot(u, wi_ref[0], precision=HIGHEST) + _dot(xp, wo_ref[0], precision=HIGHEST)
         + d_ref[0] * u)
    y_ref[0] = jax.nn.gelu(y)


def _s5_core(u_g, tabs, n_chunk):
    g, rows, w = u_g.shape
    w_intra, w_state, w_out, apow, d_t = tabs
    return pl.pallas_call(
        functools.partial(_s5_kernel, n_chunk=n_chunk),
        grid=(g,),
        in_specs=[pl.BlockSpec((1, rows, w), lambda i: (i, 0, 0)),
                  pl.BlockSpec((1, w, w), lambda i: (i, 0, 0)),
                  pl.BlockSpec((1, w, 2 * S5_P), lambda i: (i, 0, 0)),
                  pl.BlockSpec((1, 2 * S5_P, w), lambda i: (i, 0, 0)),
                  pl.BlockSpec((1, 16, 2 * S5_P), lambda i: (i, 0, 0)),
                  pl.BlockSpec((1, 1, w), lambda i: (i, 0, 0))],
        out_specs=pl.BlockSpec((1, rows, w), lambda i: (i, 0, 0)),
        out_shape=jax.ShapeDtypeStruct((g, rows, w), F32),
        compiler_params=_cparams(("arbitrary",)),
        name="s5_core",
    )(u_g, w_intra, w_state, w_out, apow, d_t)


def _glu_kernel(y_ref, w_ref, o_ref):
    y = y_ref[...]
    z = _dot(y.astype(BF16), w_ref[...])
    o_ref[...] = (y * jax.nn.sigmoid(z)).astype(BF16)


def _glu(y2, w_glu, tm):
    t, c = y2.shape
    return pl.pallas_call(
        _glu_kernel,
        grid=(t // tm,),
        in_specs=[pl.BlockSpec((tm, c), lambda i: (i, 0)),
                  pl.BlockSpec((c, c), lambda i: (0, 0))],
        out_specs=pl.BlockSpec((tm, c), lambda i: (i, 0)),
        out_shape=jax.ShapeDtypeStruct((t, c), BF16),
        compiler_params=_cparams(("arbitrary",)),
        name="s5_glu",
    )(y2, w_glu)


def _s5_mixer(proj, bsz, seq, tabs, w_glu_bf):
    tc = S5_CHUNK
    nc = seq // tc
    u = proj[:, :D_GROUP].reshape(bsz, nc, tc, S5_G, S5_GROUP)
    u_g = u.transpose(3, 0, 1, 2, 4).reshape(S5_G, bsz * nc, tc * S5_GROUP)
    y_g = _s5_core(u_g, tabs, nc)
    y = y_g.reshape(S5_G, bsz, nc, tc, S5_GROUP).transpose(1, 2, 3, 0, 4).reshape(bsz * seq, D_GROUP)
    return _glu(y, w_glu_bf, min(1024, bsz * seq))


def _ret_tables():
    h, c = RET_HEADS, RET_CHUNK
    log_g = jnp.log(1.0 - 2.0 ** (-5.0 - jnp.arange(h, dtype=F32)))
    idx = jnp.arange(c, dtype=F32)
    diff = idx[:, None] - idx[None, :]
    dmat = jnp.where(diff >= 0, jnp.exp(jnp.maximum(diff, 0.0)[None] * log_g[:, None, None]), 0.0)
    zeta = jnp.exp((c - 1.0 - idx)[None, :] * log_g[:, None])
    xi = jnp.exp((idx + 1.0)[None, :] * log_g[:, None])
    g_chunk = jnp.exp(c * log_g)
    zeta_b = jnp.broadcast_to(zeta[:, :, None], (h, c, RET_DK))
    xi_b = jnp.broadcast_to(xi[:, :, None], (h, c, RET_DK))
    gch_b = jnp.broadcast_to(g_chunk[:, None, None], (h, 8, RET_DK))
    return dmat, zeta_b, xi_b, gch_b


def _ret_kernel(q_ref, k_ref, v_ref, g_ref, cos_ref, sin_ref, dm_ref, ze_ref, xi_ref, gc_ref,
                o_ref, *, n_chunk):
    c = RET_CHUNK
    dm = dm_ref[0]
    ze = ze_ref[0]
    xi = xi_ref[0]
    gch = gc_ref[0, 0:1, :]

    def body(n, r):
        sl = pl.ds(pl.multiple_of(n * c, c), c)
        cos = cos_ref[sl, :]
        sin = sin_ref[sl, :]
        qc = _rope128(q_ref[sl, :], cos, sin)
        kc = _rope128(k_ref[sl, :], cos, sin) * (RET_DK ** -0.5)
        vc = v_ref[sl, :].astype(BF16)
        inner = _dot_nt(qc.astype(BF16), kc.astype(BF16)) * dm
        o = _dot(inner.astype(BF16), vc) + _dot((qc * xi).astype(BF16), r.astype(BF16))
        kv = _dot_tn((kc * ze).astype(BF16), vc)
        r = r * gch + kv
        mu = jnp.mean(o, axis=-1, keepdims=True)
        oc = o - mu
        var = jnp.mean(oc * oc, axis=-1, keepdims=True)
        on = oc * lax.rsqrt(var + EPS)
        gg = g_ref[sl, :]
        o_ref[sl, :] = (gg * jax.nn.sigmoid(gg) * on).astype(BF16)
        return r

    lax.fori_loop(0, n_chunk, body, jnp.zeros((RET_DK, RET_DK), F32))


def _retention(proj, cos128, sin128, tabs, bsz, seq):
    dmat, zeta_b, xi_b, gch_b = tabs
    h = RET_HEADS
    col = lambda cb: pl.BlockSpec((seq, LANE), lambda b, hh, cb=cb: (b, cb + hh))
    tab = pl.BlockSpec((seq, LANE), lambda b, hh: (b, 0))
    head = lambda r: pl.BlockSpec((1, r, LANE), lambda b, hh: (hh, 0, 0))
    return pl.pallas_call(
        functools.partial(_ret_kernel, n_chunk=seq // RET_CHUNK),
        grid=(bsz, h),
        in_specs=[col(CB_RQ), col(CB_RK), col(CB_RV), col(CB_RG), tab, tab,
                  head(RET_CHUNK), head(RET_CHUNK), head(RET_CHUNK), head(8)],
        out_specs=pl.BlockSpec((seq, LANE), lambda b, hh: (b, hh)),
        out_shape=jax.ShapeDtypeStruct((bsz * seq, D_GROUP), BF16),
        compiler_params=_cparams(("arbitrary", "arbitrary")),
        name="retention",
    )(proj, proj, proj, proj, cos128, sin128, dmat, zeta_b, xi_b, gch_b)


def _nsa_prep_kernel(kc_ref, vc_ref, cos16_ref, sin16_ref, pek_ref, pev_ref, wk_ref, wv_ref,
                     ks_ref, kw_ref, cos_ref, sin_ref,
                     kcmp_ref, vcmp_ref, ksr_ref, kwr_ref):
    half = NSA_CMP_STRIDE * NSA_DH
    x = kc_ref[0]
    parts = []
    for i in range(NSA_CMP_STRIDE):
        sl = slice(i * LANE, (i + 1) * LANE)
        parts.append(_rope128(x[:, sl], cos16_ref[0, :, sl], sin16_ref[0, :, sl]))
    xr = jnp.concatenate(parts, axis=1)
    n16 = xr.shape[0]
    a = _dot((xr + pek_ref[0:1, :]).astype(BF16), wk_ref[0:half, :])
    b = _dot((xr + pek_ref[1:2, :]).astype(BF16), wk_ref[half:2 * half, :])
    kcm = _rms_rows(a + pltpu.roll(b, n16 - 1, axis=0))
    rowi = lax.broadcasted_iota(jnp.int32, kcm.shape, 0)
    kcmp_ref[0] = jnp.where(rowi < n16 - 1, kcm, 0.0).astype(BF16)
    xv = vc_ref[0]
    av = _dot((xv + pev_ref[0:1, :]).astype(BF16), wv_ref[0:half, :])
    bv = _dot((xv + pev_ref[1:2, :]).astype(BF16), wv_ref[half:2 * half, :])
    vcm = av + pltpu.roll(bv, n16 - 1, axis=0)
    vcmp_ref[0] = jnp.where(rowi < n16 - 1, vcm, 0.0).astype(BF16)
    cos = cos_ref[...]
    sin = sin_ref[...]
    ksr_ref[...] = _rope128(_rms_rows(ks_ref[...]), cos, sin).astype(BF16)
    kwr_ref[...] = _rope128(_rms_rows(kw_ref[...]), cos, sin).astype(BF16)


def _nsa_prep(proj, cos128, sin128, pe_k, pe_v, wk_bf, wv_bf, bsz, seq):
    n16 = seq // NSA_CMP_STRIDE
    half = NSA_CMP_STRIDE * NSA_DH
    kc16 = proj[:, CB_NKC * LANE:(CB_NKC + 1) * LANE].reshape(bsz, n16, half)
    vc16 = proj[:, CB_NVC * LANE:(CB_NVC + 1) * LANE].reshape(bsz, n16, half)
    cos16 = cos128.reshape(bsz, n16, half)
    sin16 = sin128.reshape(bsz, n16, half)
    pek = pe_k.astype(F32).reshape(2, half)
    pev = pe_v.astype(F32).reshape(2, half)
    b3 = pl.BlockSpec((1, n16, half), lambda b: (b, 0, 0))
    full = lambda a: pl.BlockSpec(a.shape, lambda b: (0,) * a.ndim)
    col = lambda cb: pl.BlockSpec((seq, LANE), lambda b, cb=cb: (b, cb))
    tab = pl.BlockSpec((seq, LANE), lambda b: (b, 0))
    cmp_spec = pl.BlockSpec((1, n16, NSA_DH), lambda b: (b, 0, 0))
    seq_spec = pl.BlockSpec((seq, LANE), lambda b: (b, 0))
    return pl.pallas_call(
        _nsa_prep_kernel,
        grid=(bsz,),
        in_specs=[b3, b3, b3, b3, full(pek), full(pev), full(wk_bf), full(wv_bf),
                  col(CB_NKS), col(CB_NKW), tab, tab],
        out_specs=[cmp_spec, cmp_spec, seq_spec, seq_spec],
        out_shape=[jax.ShapeDtypeStruct((bsz, n16, NSA_DH), BF16),
                   jax.ShapeDtypeStruct((bsz, n16, NSA_DH), BF16),
                   jax.ShapeDtypeStruct((bsz * seq, NSA_DH), BF16),
                   jax.ShapeDtypeStruct((bsz * seq, NSA_DH), BF16)],
        compiler_params=_cparams(("arbitrary",)),
        name="nsa_prep",
    )(kc16, vc16, cos16, sin16, pek, pev, wk_bf, wv_bf, proj, proj, cos128, sin128)


def _softmax_step(s3, mask, vv, carry):
    m, l, acc = carry
    hh, qb, kk = s3.shape
    sm = jnp.where(mask[None], s3, NEG_INF)
    m_new = jnp.maximum(m, jnp.max(sm, axis=-1, keepdims=True))
    alpha = jnp.exp(m - m_new)
    e = jnp.where(mask[None], jnp.exp(sm - m_new), 0.0)
    l = alpha * l + jnp.sum(e, axis=-1, keepdims=True)
    pv = _dot(e.astype(BF16).reshape(hh * qb, kk), vv).reshape(hh, qb, vv.shape[-1])
    return m_new, l, alpha * acc + pv


def _softmax_init(hh, qb, dh):
    return (jnp.full((hh, qb, 1), NEG_INF, F32), jnp.zeros((hh, qb, 1), F32),
            jnp.zeros((hh, qb, dh), F32))


def _softmax_finish(carry):
    _, l, acc = carry
    return jnp.where(l > 0.0, acc / jnp.where(l > 0.0, l, 1.0), 0.0)


def _nsa_kernel(q_ref, gate_ref, cos_ref, sin_ref, kc_ref, vc_ref, ks_ref, vs_ref, kw_ref, vw_ref,
                ov_ref, o_ref):
    hh, qb, dh = NSA_HEADS, Q_BLOCK, NSA_DH
    scale = dh ** -0.5
    n = pl.program_id(1)
    cos = cos_ref[...]
    sin = sin_ref[...]
    qs = [_rope128(_rms_rows(q_ref[:, h * dh:(h + 1) * dh]), cos, sin) for h in range(hh)]
    qq = jnp.concatenate(qs, axis=0).astype(BF16)
    col = lax.broadcasted_iota(jnp.int32, (qb, LANE), 1)
    t = n * qb + lax.broadcasted_iota(jnp.int32, (qb, LANE), 0)

    ncp = kc_ref.shape[1]
    colc = lax.broadcasted_iota(jnp.int32, (qb, ncp), 1)
    tc = n * qb + lax.broadcasted_iota(jnp.int32, (qb, ncp), 0)
    s3 = (_dot_nt(qq, kc_ref[0]) * scale).reshape(hh, qb, ncp)
    mask_c = (colc * NSA_CMP_STRIDE + (NSA_CMP_LEN - 1)) <= tc
    sm = jnp.where(mask_c[None], s3, NEG_INF)
    e = jnp.exp(sm - jnp.max(sm, axis=-1, keepdims=True))
    p = jnp.where(mask_c[None], e / jnp.sum(e, axis=-1, keepdims=True), 0.0)
    o_cmp = _dot(p.astype(BF16).reshape(hh * qb, ncp), vc_ref[0]).reshape(hh, qb, dh)

    imp = _dot(jnp.sum(p, axis=0), ov_ref[...], precision=HIGHEST)
    cur = t >> NSA_SEL_SHIFT
    forced = (col == 0) | (col == cur) | (col == cur - 1)
    score = jnp.where(col <= cur, imp + jnp.where(forced, NSA_FORCE_BONUS, 0.0), NEG_INF)
    n_sel = ks_ref.shape[0] // NSA_SEL_LEN
    rank = jnp.zeros((qb, LANE), F32)
    for jp in range(n_sel):
        cj = score[:, jp:jp + 1]
        tie = jnp.where(col > jp, 1.0, 0.0)
        rank = rank + jnp.where(cj > score, 1.0, jnp.where(cj == score, tie, 0.0))
    selm = jnp.where(rank < float(min(NSA_TOPK, n_sel)), 1.0, 0.0).astype(BF16)

    rowi = lax.broadcasted_iota(jnp.int32, (LANE, LANE), 0)
    coli = lax.broadcasted_iota(jnp.int32, (LANE, LANE), 1)

    def sel_body(kt, carry):
        sl = pl.ds(pl.multiple_of(kt * qb, qb), qb)
        kk = ks_ref[sl, :]
        vv = vs_ref[sl, :].astype(BF16)
        s = (_dot_nt(qq, kk) * scale).reshape(hh, qb, qb)
        expand = jnp.where(rowi == kt * (qb // NSA_SEL_LEN) + (coli >> NSA_SEL_SHIFT), 1.0, 0.0)
        picked = _dot(selm, expand.astype(BF16))
        kpos = kt * qb + col
        mask = jnp.where(kpos <= t, picked, 0.0) > 0.5
        return _softmax_step(s, mask, vv, carry)

    o_sel = _softmax_finish(lax.fori_loop(0, n + 1, sel_body, _softmax_init(hh, qb, dh)))

    carry = _softmax_init(hh, qb, dh)
    for i in range(NSA_WINDOW // qb + 1):
        tile = n - NSA_WINDOW // qb + i
        tidx = jnp.maximum(tile, 0)
        sl = pl.ds(pl.multiple_of(tidx * qb, qb), qb)
        kk = kw_ref[sl, :]
        vv = vw_ref[sl, :].astype(BF16)
        s = (_dot_nt(qq, kk) * scale).reshape(hh, qb, qb)
        dist = t - (tile * qb + col)
        inwin = jnp.where(dist >= 0, jnp.where(dist < NSA_WINDOW, 1.0, 0.0), 0.0)
        mask = jnp.where(tile >= 0, inwin, 0.0) > 0.5
        carry = _softmax_step(s, mask, vv, carry)
    o_win = _softmax_finish(carry)

    gts = jax.nn.sigmoid(gate_ref[...])
    for h in range(hh):
        o = (gts[:, 3 * h:3 * h + 1] * o_cmp[h] + gts[:, 3 * h + 1:3 * h + 2] * o_sel[h]
             + gts[:, 3 * h + 2:3 * h + 3] * o_win[h])
        o_ref[:, h * dh:(h + 1) * dh] = o.astype(BF16)


def _nsa_overlap_table(seq):
    n_cmp = (seq - NSA_CMP_LEN) // NSA_CMP_STRIDE + 1
    n_sel = seq // NSA_SEL_LEN
    cmp_start = np.arange(n_cmp) * NSA_CMP_STRIDE
    sel_start = np.arange(n_sel) * NSA_SEL_LEN
    overlap = ((cmp_start[None, :] < sel_start[:, None] + NSA_SEL_LEN)
               & (cmp_start[None, :] + NSA_CMP_LEN > sel_start[:, None])).astype(np.float32)
    tab = np.zeros((n_cmp + 1, LANE), np.float32)
    tab[:n_cmp, :n_sel] = overlap.T
    return jnp.asarray(tab)


def _nsa_attn(proj, cos128, sin128, kcmp, vcmp, ksr, kwr, bsz, seq):
    nb = seq // Q_BLOCK
    qspec = pl.BlockSpec((Q_BLOCK, D_GROUP), lambda b, n: (b * nb + n, CB_NQ // 4))
    qcol = lambda cb: pl.BlockSpec((Q_BLOCK, LANE), lambda b, n, cb=cb: (b * nb + n, cb))
    qtab = pl.BlockSpec((Q_BLOCK, LANE), lambda b, n: (b * nb + n, 0))
    cmp_spec = pl.BlockSpec((1, kcmp.shape[1], NSA_DH), lambda b, n: (b, 0, 0))
    kseq = pl.BlockSpec((seq, LANE), lambda b, n: (b, 0))
    vcol = lambda cb: pl.BlockSpec((seq, LANE), lambda b, n, cb=cb: (b, cb))
    ov = _nsa_overlap_table(seq)
    return pl.pallas_call(
        _nsa_kernel,
        grid=(bsz, nb),
        in_specs=[qspec, qcol(CB_GATE), qtab, qtab, cmp_spec, cmp_spec,
                  kseq, vcol(CB_NVS), kseq, vcol(CB_NVW),
                  pl.BlockSpec(ov.shape, lambda b, n: (0, 0))],
        out_specs=pl.BlockSpec((Q_BLOCK, D_GROUP), lambda b, n: (b * nb + n, 0)),
        out_shape=jax.ShapeDtypeStruct((bsz * seq, D_GROUP), BF16),
        compiler_params=_cparams(("arbitrary", "arbitrary")),
        name="nsa_attn",
    )(proj, proj, cos128, sin128, kcmp, vcmp, ksr, proj, kwr, proj, ov)


def _diff_prep_kernel(q_ref, k_ref, cos_ref, sin_ref, qo_ref, ko_ref):
    cos = cos_ref[...]
    sin = sin_ref[...]
    lane = lax.broadcasted_iota(jnp.int32, cos.shape, 1)
    lo = lane < DIFF_DH
    first = (lane & (DIFF_DH - 1)) < DIFF_DH // 2

    def prep(x):
        x2 = x * x
        ss_lo = jnp.sum(jnp.where(lo, x2, 0.0), axis=-1, keepdims=True)
        ss_hi = jnp.sum(jnp.where(lo, 0.0, x2), axis=-1, keepdims=True)
        ms = jnp.where(lo, ss_lo, ss_hi) * (1.0 / DIFF_DH)
        xn = x * lax.rsqrt(ms + EPS)
        partner = jnp.where(first, pltpu.roll(xn, LANE - DIFF_DH // 2, axis=1),
                            pltpu.roll(xn, DIFF_DH // 2, axis=1))
        return xn * cos + partner * sin

    for h in range(DIFF_HEADS):
        sl = slice(h * LANE, (h + 1) * LANE)
        qo_ref[:, sl] = (prep(q_ref[:, sl]) * (DIFF_DH ** -0.5)).astype(BF16)
        ko_ref[:, sl] = prep(k_ref[:, sl]).astype(BF16)


def _diff_prep(proj, cos64, sin64, tm):
    t = proj.shape[0]
    blk = lambda cb: pl.BlockSpec((tm, D_GROUP), lambda i, cb=cb: (i, cb // 4))
    tab = pl.BlockSpec((tm, LANE), lambda i: (i, 0))
    out = pl.BlockSpec((tm, D_GROUP), lambda i: (i, 0))
    return pl.pallas_call(
        _diff_prep_kernel,
        grid=(t // tm,),
        in_specs=[blk(CB_DQ), blk(CB_DK), tab, tab],
        out_specs=[out, out],
        out_shape=[jax.ShapeDtypeStruct((t, D_GROUP), BF16)] * 2,
        compiler_params=_cparams(("arbitrary",)),
        name="diff_prep",
    )(proj, proj, cos64, sin64)


def _diff_kernel(lam_ref, q_ref, k_ref, v_ref, o_ref, *, out_scale):
    qb = Q_BLOCK
    n = pl.program_id(2)
    q = q_ref[...]
    lane = lax.broadcasted_iota(jnp.int32, q.shape, 1)
    zero = jnp.zeros_like(q)
    qq = jnp.concatenate([jnp.where(lane < DIFF_DH, q, zero), jnp.where(lane < DIFF_DH, zero, q)],
                         axis=0)
    col = lax.broadcasted_iota(jnp.int32, (qb, qb), 1)
    t = n * qb + lax.broadcasted_iota(jnp.int32, (qb, qb), 0)

    def body(kt, carry):
        sl = pl.ds(pl.multiple_of(kt * qb, qb), qb)
        s = _dot_nt(qq, k_ref[sl, :]).reshape(2, qb, qb)
        mask = (kt * qb + col) <= t
        return _softmax_step(s, mask, v_ref[sl, :].astype(BF16), carry)

    o2 = _softmax_finish(lax.fori_loop(0, n + 1, body, _softmax_init(2, qb, LANE)))
    o = o2[0] - lam_ref[0, 0] * o2[1]
    o_ref[...] = (_rms_rows(o) * out_scale).astype(BF16)


def _diff_attn(lam, qd, kd, proj, bsz, seq, lam_init):
    nb = seq // Q_BLOCK
    h = DIFF_HEADS
    return pl.pallas_call(
        functools.partial(_diff_kernel, out_scale=1.0 - lam_init),
        grid=(bsz, h, nb),
        in_specs=[pl.BlockSpec(memory_space=pltpu.SMEM),
                  pl.BlockSpec((Q_BLOCK, LANE), lambda b, hh, n: (b * nb + n, hh)),
                  pl.BlockSpec((seq, LANE), lambda b, hh, n: (b, hh)),
                  pl.BlockSpec((seq, LANE), lambda b, hh, n: (b, CB_DV + hh))],
        out_specs=pl.BlockSpec((Q_BLOCK, LANE), lambda b, hh, n: (b * nb + n, hh)),
        out_shape=jax.ShapeDtypeStruct((bsz * seq, D_GROUP), BF16),
        compiler_params=_cparams(("arbitrary", "arbitrary", "arbitrary")),
        name="diff_attn",
    )(lam, qd, kd, proj)


def _outproj_kernel(ya_ref, yb_ref, yc_ref, yd_ref, w_ref, x_ref, g_ref, xo_ref, h_ref):
    mixed = jnp.concatenate([ya_ref[...], yb_ref[...], yc_ref[...], yd_ref[...]], axis=1)
    x = x_ref[...] + _dot(mixed, w_ref[...])
    xo_ref[...] = x
    h_ref[...] = (_rms_rows(x) * g_ref[...]).astype(BF16)


def _outproj(ys, w_bf, x2, g, tm):
    t, d = x2.shape
    yspec = pl.BlockSpec((tm, D_GROUP), lambda i: (i, 0))
    row = pl.BlockSpec((tm, d), lambda i: (i, 0))
    return pl.pallas_call(
        _outproj_kernel,
        grid=(t // tm,),
        in_specs=[yspec, yspec, yspec, yspec,
                  pl.BlockSpec((d, d), lambda i: (0, 0)), row,
                  pl.BlockSpec((1, d), lambda i: (0, 0))],
        out_specs=[row, row],
        out_shape=[jax.ShapeDtypeStruct((t, d), F32), jax.ShapeDtypeStruct((t, d), BF16)],
        compiler_params=_cparams(("arbitrary",)),
        name="outproj",
    )(*ys, w_bf, x2, g)


def _mlp_kernel(h_ref, x_ref, w1_ref, w2_ref, o_ref):
    @pl.when(pl.program_id(1) == 0)
    def _():
        o_ref[...] = x_ref[...]

    a = _dot(h_ref[...], w1_ref[...])
    a = jnp.square(jnp.maximum(a, 0.0)).astype(BF16)
    o_ref[...] += _dot(a, w2_ref[...])


def _mlp(h2, x2, w1_bf, w2_bf, tm, tf):
    t, d = x2.shape
    f = w1_bf.shape[1]
    return pl.pallas_call(
        _mlp_kernel,
        grid=(t // tm, f // tf),
        in_specs=[pl.BlockSpec((tm, d), lambda i, j: (i, 0)),
                  pl.BlockSpec((tm, d), lambda i, j: (i, 0)),
                  pl.BlockSpec((d, tf), lambda i, j: (0, j)),
                  pl.BlockSpec((tf, d), lambda i, j: (j, 0))],
        out_specs=pl.BlockSpec((tm, d), lambda i, j: (i, 0)),
        out_shape=jax.ShapeDtypeStruct((t, d), F32),
        compiler_params=_cparams(("arbitrary", "arbitrary")),
        name="mlp",
    )(h2, x2, w1_bf, w2_bf)


def _rope_tables(positions):
    bsz, seq = positions.shape
    pos = positions.astype(F32)[..., None]

    def tab(d):
        inv = ROPE_THETA ** (-jnp.arange(0, d, 2, dtype=F32) / d)
        ang = pos * inv
        return jnp.cos(ang), jnp.sin(ang)

    c, s = tab(NSA_DH)
    cos128 = jnp.concatenate([c, c], axis=-1).reshape(bsz * seq, LANE)
    sin128 = jnp.concatenate([-s, s], axis=-1).reshape(bsz * seq, LANE)
    c, s = tab(DIFF_DH)
    cos64 = jnp.concatenate([c, c, c, c], axis=-1).reshape(bsz * seq, LANE)
    sin64 = jnp.concatenate([-s, s, -s, s], axis=-1).reshape(bsz * seq, LANE)
    return cos128, sin128, cos64, sin64


def _pad_w_in(w):
    d = w.shape[0]
    return jnp.concatenate(
        [w[:, :GATE_RAW_END], jnp.zeros((d, CB_DQ * LANE - GATE_RAW_END), w.dtype), w[:, GATE_RAW_END:]],
        axis=1).astype(BF16)


def kernel(x, positions, norm1_g, w_in, s5_lambda_re, s5_lambda_im, s5_log_dt, s5_b_re, s5_b_im, s5_c_re, s5_c_im, s5_d, s5_w_glu, nsa_pe_k, nsa_pe_v, nsa_w_cmp_k, nsa_w_cmp_v, diff_lq1, diff_lk1, diff_lq2, diff_lk2, w_out, norm2_g, mlp_w1, mlp_w2):
    bsz, seq, d = x.shape
    t = bsz * seq
    depth = w_in.shape[0]
    tm = min(1024, t)
    cos128, sin128, cos64, sin64 = _rope_tables(positions)
    ret_tabs = _ret_tables()
    x2 = x.reshape(t, d).astype(F32)
    for layer in range(depth):
        proj = _inproj(x2, norm1_g[layer].astype(F32)[None, :], _pad_w_in(w_in[layer]), tm, 512)

        s5_tabs = _s5_tables(s5_lambda_re[layer], s5_lambda_im[layer], s5_log_dt[layer],
                             s5_b_re[layer], s5_b_im[layer], s5_c_re[layer], s5_c_im[layer],
                             s5_d[layer], seq // S5_CHUNK)
        y_a = _s5_mixer(proj, bsz, seq, s5_tabs, s5_w_glu[layer].astype(BF16))

        y_b = _retention(proj, cos128, sin128, ret_tabs, bsz, seq)

        kcmp, vcmp, ksr, kwr = _nsa_prep(proj, cos128, sin128, nsa_pe_k[layer], nsa_pe_v[layer],
                                         nsa_w_cmp_k[layer].astype(BF16), nsa_w_cmp_v[layer].astype(BF16),
                                         bsz, seq)
        y_c = _nsa_attn(proj, cos128, sin128, kcmp, vcmp, ksr, kwr, bsz, seq)

        lam_init = 0.8 - 0.6 * math.exp(-0.3 * layer)
        lam = (jnp.exp(jnp.sum(diff_lq1[layer].astype(F32) * diff_lk1[layer].astype(F32)))
               - jnp.exp(jnp.sum(diff_lq2[layer].astype(F32) * diff_lk2[layer].astype(F32))) + lam_init)
        qd, kd = _diff_prep(proj, cos64, sin64, tm)
        y_d = _diff_attn(lam.reshape(1, 1).astype(F32), qd, kd, proj, bsz, seq, lam_init)

        x2, h2 = _outproj((y_a, y_b, y_c, y_d), w_out[layer].astype(BF16), x2,
                          norm2_g[layer].astype(F32)[None, :], min(512, t))
        x2 = _mlp(h2, x2, mlp_w1[layer].astype(BF16), mlp_w2[layer].astype(BF16), tm, 512)
    return x2.reshape(bsz, seq, d).astype(x.dtype)
```

```python
import functools
import math

import jax
import jax.numpy as jnp
import numpy as np
from jax import lax
from jax.experimental import pallas as pl
from jax.experimental.pallas import tpu as pltpu

F32 = jnp.float32
BF16 = jnp.bfloat16
HIGHEST = lax.Precision.HIGHEST

D_MODEL = 2048
D_GROUP = 512
D_FF = 4 * D_MODEL
ROPE_THETA = 10000.0
EPS = 1e-6
Q_BLOCK = 128
NEG_INF = -1e30

S5_GROUP = 16
S5_G = 32
S5_P = 64
S5_CHUNK = 16

RET_HEADS = 4
RET_DK = 128
RET_CHUNK = 128

NSA_HEADS = 4
NSA_DH = 128
NSA_CMP_LEN = 32
NSA_CMP_STRIDE = 16
NSA_SEL_LEN = 64
NSA_TOPK = 8
NSA_WINDOW = 256
NSA_FORCE_BONUS = 1e4
NSA_SEL_SHIFT = 6

DIFF_HEADS = 4
DIFF_DH = 64
DIFF_TQ = 256

LANE = 128
GATE_RAW_END = 3852
IN_TN = 512
MAIN_TILES = 8
CB_U, CB_RQ, CB_RK, CB_RV, CB_RG, CB_NQ = 0, 4, 8, 12, 16, 20
CB_NKC, CB_NVC, CB_NKS, CB_NVS, CB_NKW, CB_NVW, CB_GATE = 24, 25, 26, 27, 28, 29, 30
CB_DQ, CB_DK, CB_DV = 0, 4, 8

VMEM_LIMIT = 56 * 1024 * 1024


def _cparams(sem):
    return pltpu.CompilerParams(dimension_semantics=sem, vmem_limit_bytes=VMEM_LIMIT)


def _dot(a, b, **kw):
    return jnp.dot(a, b, preferred_element_type=F32, **kw)


def _dot_nt(a, b):
    return lax.dot_general(a, b, (((1,), (1,)), ((), ())), preferred_element_type=F32)


def _dot_tn(a, b):
    return lax.dot_general(a, b, (((0,), (0,)), ((), ())), preferred_element_type=F32)


def _rms_rows(x):
    return x * lax.rsqrt(jnp.mean(x * x, axis=-1, keepdims=True) + EPS)


def _rope128(x, cos, sin_signed):
    return x * cos + pltpu.roll(x, 64, axis=1) * sin_signed


def _inproj_kernel(x_ref, g_ref, w_ref, o_ref, xn_ref):
    @pl.when(pl.program_id(1) == 0)
    def _():
        x = x_ref[...]
        xn_ref[...] = (_rms_rows(x) * g_ref[...]).astype(BF16)

    o_ref[...] = _dot(xn_ref[...], w_ref[0].astype(BF16))


def _inproj(x2, g, w3, layer, n_tiles, tm, tn):
    t, d = x2.shape
    return pl.pallas_call(
        _inproj_kernel,
        grid=(t // tm, n_tiles),
        in_specs=[pl.BlockSpec((tm, d), lambda i, j: (i, 0)),
                  pl.BlockSpec((1, d), lambda i, j: (0, 0)),
                  pl.BlockSpec((1, d, tn), lambda i, j: (layer, 0, j))],
        out_specs=pl.BlockSpec((tm, tn), lambda i, j: (i, j)),
        out_shape=jax.ShapeDtypeStruct((t, n_tiles * tn), F32),
        scratch_shapes=[pltpu.VMEM((tm, d), BF16)],
        compiler_params=_cparams(("arbitrary", "arbitrary")),
        name="inproj",
    )(x2, g, w3)


def _s5_tables(lam_re, lam_im, log_dt, b_re, b_im, c_re, c_im, d_skip, n_chunk):
    tc = S5_CHUNK
    dt = jnp.exp(log_dt.astype(F32))[:, None]
    lam_re = lam_re.astype(F32)
    lam_im = lam_im.astype(F32)
    lre = lam_re * dt
    lim = lam_im * dt

    def lbpow(k):
        k = jnp.asarray(k, F32)[..., None, None]
        mag = jnp.exp(lre * k)
        return mag * jnp.cos(lim * k), mag * jnp.sin(lim * k)

    lb_re, lb_im = lbpow(1.0)
    den = lam_re * lam_re + lam_im * lam_im
    f_re = ((lb_re - 1.0) * lam_re + lb_im * lam_im) / den
    f_im = (lb_im * lam_re - (lb_re - 1.0) * lam_im) / den
    b_re = b_re.astype(F32)
    b_im = b_im.astype(F32)
    bb_re = f_re[..., None] * b_re - f_im[..., None] * b_im
    bb_im = f_re[..., None] * b_im + f_im[..., None] * b_re
    c_re = c_re.astype(F32)
    c_im = c_im.astype(F32)

    pr, pi = lbpow(jnp.arange(tc + 1))
    lb_b_re = pr[..., None] * bb_re[None] - pi[..., None] * bb_im[None]
    lb_b_im = pr[..., None] * bb_im[None] + pi[..., None] * bb_re[None]
    kk = (jnp.einsum('gnp,kgph->gknh', c_re, lb_b_re, precision=HIGHEST)
          - jnp.einsum('gnp,kgph->gknh', c_im, lb_b_im, precision=HIGHEST))
    ss = np.arange(tc)[:, None]
    tt = np.arange(tc)[None, :]
    lag = np.clip(tt - ss, 0, tc)
    toe = kk[:, lag]
    toe = jnp.where(jnp.asarray(tt >= ss)[None, :, :, None, None], toe, 0.0)
    g = lam_re.shape[0]
    w_intra = toe.transpose(0, 1, 4, 2, 3).reshape(g, tc * S5_GROUP, tc * S5_GROUP)

    rev = np.arange(tc)[::-1].copy()
    st_re = lb_b_re[rev]
    st_im = lb_b_im[rev]
    w_state = jnp.stack([st_re, st_im], axis=0).transpose(2, 1, 4, 0, 3)
    w_state = w_state.reshape(g, tc * S5_GROUP, 2 * S5_P)

    ar = pr[1:]
    ai = pi[1:]
    o_re = c_re[None] * ar[:, :, None, :] - c_im[None] * ai[:, :, None, :]
    o_im = -c_re[None] * ai[:, :, None, :] - c_im[None] * ar[:, :, None, :]
    w_out = jnp.stack([o_re, o_im], axis=0).transpose(2, 0, 4, 1, 3)
    w_out = w_out.reshape(g, 2 * S5_P, tc * S5_GROUP)

    n_step = int(math.log2(n_chunk))
    sr, si = lbpow(tc * (2.0 ** jnp.arange(n_step)))
    row_a = jnp.concatenate([sr, sr], axis=-1)
    row_b = jnp.concatenate([-si, si], axis=-1)
    apow = jnp.stack([row_a, row_b], axis=1).reshape(2 * n_step, g, 2 * S5_P).transpose(1, 0, 2)
    apow = jnp.pad(apow, ((0, 0), (0, 16 - 2 * n_step), (0, 0)))
    d_row = d_skip.astype(F32).reshape(1, g * S5_GROUP)
    return (w_intra.transpose(0, 2, 1).astype(BF16), w_state.transpose(0, 2, 1).astype(BF16),
            w_out.transpose(0, 2, 1).astype(BF16), apow, d_row)


def _s5_pack_kernel(*refs):
    at_ref = refs[-1]
    nc = at_ref.shape[2]
    for t in range(S5_CHUNK):
        for j, u_ref in enumerate(refs[:-1]):
            at_ref[t, j * LANE:(j + 1) * LANE, :] = u_ref[pl.ds(t, nc, stride=S5_CHUNK), :].T.astype(BF16)


def _u_specs(seq):
    return [pl.BlockSpec((seq, LANE), lambda b, j=j: (b, CB_U + j)) for j in range(D_GROUP // LANE)]


def _s5_pack(proj, bsz, seq):
    nc = seq // S5_CHUNK
    return pl.pallas_call(
        _s5_pack_kernel,
        grid=(bsz,),
        in_specs=_u_specs(seq),
        out_specs=pl.BlockSpec((S5_CHUNK, D_GROUP, nc), lambda b: (0, 0, b)),
        out_shape=jax.ShapeDtypeStruct((S5_CHUNK, D_GROUP, bsz * nc), BF16),
        compiler_params=_cparams(("arbitrary",)),
        name="s5_pack",
    )(*([proj] * (D_GROUP // LANE)))


def _s5_kernel(ut_ref, wi_ref, ws_ref, wo_ref, ap_ref, yt_ref, *, n_chunk):
    tc, gs, r = ut_ref.shape
    ut = ut_ref[...].reshape(tc * gs, r)
    s = _dot(ws_ref[0], ut)
    lane = lax.broadcasted_iota(jnp.int32, s.shape, 1) & (n_chunk - 1)
    apt = ap_ref[0].T
    x = s
    n_step = int(math.log2(n_chunk))
    for k in range(n_step):
        sh = 1 << k
        xs = jnp.where(lane >= sh, pltpu.roll(x, sh, axis=1), 0.0)
        xsw = jnp.concatenate([xs[S5_P:], xs[:S5_P]], axis=0)
        x = x + xs * apt[:, 2 * k:2 * k + 1] + xsw * apt[:, 2 * k + 1:2 * k + 2]
    xp = jnp.where(lane >= 1, pltpu.roll(x, 1, axis=1), 0.0)
    y = _dot(wi_ref[0], ut) + _dot(wo_ref[0], xp.astype(BF16))
    yt_ref[...] = y.reshape(tc, gs, r)


def _s5_core(at, tabs, n_chunk):
    tc, ch, r = at.shape
    w_intra_t, w_state_t, w_out_t, apow, _ = tabs
    g = w_intra_t.shape[0]
    w = tc * S5_GROUP
    return pl.pallas_call(
        functools.partial(_s5_kernel, n_chunk=n_chunk),
        grid=(g,),
        in_specs=[pl.BlockSpec((tc, S5_GROUP, r), lambda i: (0, i, 0)),
                  pl.BlockSpec((1, w, w), lambda i: (i, 0, 0)),
                  pl.BlockSpec((1, 2 * S5_P, w), lambda i: (i, 0, 0)),
                  pl.BlockSpec((1, w, 2 * S5_P), lambda i: (i, 0, 0)),
                  pl.BlockSpec((1, 16, 2 * S5_P), lambda i: (i, 0, 0))],
        out_specs=pl.BlockSpec((tc, S5_GROUP, r), lambda i: (0, i, 0)),
        out_shape=jax.ShapeDtypeStruct((tc, ch, r), F32),
        compiler_params=_cparams(("arbitrary",)),
        name="s5_core",
    )(at, w_intra_t, w_state_t, w_out_t, apow)


def _s5_out_kernel(yt_ref, u0_ref, u1_ref, u2_ref, u3_ref, d_ref, wg_ref, o_ref, y_scr):
    nc = yt_ref.shape[2]
    for t in range(S5_CHUNK):
        rows = pl.ds(t, nc, stride=S5_CHUNK)
        for j, u_ref in enumerate((u0_ref, u1_ref, u2_ref, u3_ref)):
            sl = slice(j * LANE, (j + 1) * LANE)
            y_scr[j, rows, :] = jax.nn.gelu(yt_ref[t, sl, :].T + d_ref[:, sl] * u_ref[rows, :])
    y = jnp.concatenate([y_scr[j] for j in range(D_GROUP // LANE)], axis=1)
    o_ref[...] = (y * jax.nn.sigmoid(_dot(y.astype(BF16), wg_ref[...]))).astype(BF16)


def _s5_out(yt, proj, d_row, w_glu_bf, bsz, seq):
    nc = seq // S5_CHUNK
    return pl.pallas_call(
        _s5_out_kernel,
        grid=(bsz,),
        in_specs=[pl.BlockSpec((S5_CHUNK, D_GROUP, nc), lambda b: (0, 0, b))] + _u_specs(seq)
        + [pl.BlockSpec((1, D_GROUP), lambda b: (0, 0)),
           pl.BlockSpec((D_GROUP, D_GROUP), lambda b: (0, 0))],
        out_specs=pl.BlockSpec((seq, D_GROUP), lambda b: (b, 0)),
        out_shape=jax.ShapeDtypeStruct((bsz * seq, D_GROUP), BF16),
        scratch_shapes=[pltpu.VMEM((D_GROUP // LANE, seq, LANE), F32)],
        compiler_params=_cparams(("arbitrary",)),
        name="s5_out",
    )(yt, proj, proj, proj, proj, d_row, w_glu_bf)


def _s5_mixer(proj, bsz, seq, tabs, w_glu_bf):
    yt = _s5_core(_s5_pack(proj, bsz, seq), tabs, seq // S5_CHUNK)
    return _s5_out(yt, proj, tabs[4], w_glu_bf, bsz, seq)


def _ret_tables():
    h, c = RET_HEADS, RET_CHUNK
    log_g = jnp.log(1.0 - 2.0 ** (-5.0 - jnp.arange(h, dtype=F32)))
    idx = jnp.arange(c, dtype=F32)
    diff = idx[:, None] - idx[None, :]
    dmat = jnp.where(diff >= 0, jnp.exp(jnp.maximum(diff, 0.0)[None] * log_g[:, None, None]), 0.0)
    zeta = jnp.exp((c - 1.0 - idx)[None, :] * log_g[:, None])
    xi = jnp.exp((idx + 1.0)[None, :] * log_g[:, None])
    g_chunk = jnp.exp(c * log_g)
    zeta_b = jnp.broadcast_to(zeta[:, :, None], (h, c, RET_DK))
    xi_b = jnp.broadcast_to(xi[:, :, None], (h, c, RET_DK))
    gch_b = jnp.broadcast_to(g_chunk[:, None, None], (h, 8, RET_DK))
    return dmat, zeta_b, xi_b, gch_b


def _ret_kernel(q_ref, k_ref, v_ref, g_ref, cos_ref, sin_ref, dm_ref, ze_ref, xi_ref, gc_ref,
                o_ref, *, n_chunk):
    c = RET_CHUNK
    dm = dm_ref[0]
    ze = ze_ref[0]
    xi = xi_ref[0]
    gch = gc_ref[0, 0:1, :]

    def body(n, r):
        sl = pl.ds(pl.multiple_of(n * c, c), c)
        cos = cos_ref[sl, :]
        sin = sin_ref[sl, :]
        qc = _rope128(q_ref[sl, :], cos, sin)
        kc = _rope128(k_ref[sl, :], cos, sin) * (RET_DK ** -0.5)
        vc = v_ref[sl, :].astype(BF16)
        inner = _dot_nt(qc.astype(BF16), kc.astype(BF16)) * dm
        o = _dot(inner.astype(BF16), vc) + _dot((qc * xi).astype(BF16), r.astype(BF16))
        kv = _dot_tn((kc * ze).astype(BF16), vc)
        r = r * gch + kv
        mu = jnp.mean(o, axis=-1, keepdims=True)
        oc = o - mu
        var = jnp.mean(oc * oc, axis=-1, keepdims=True)
        on = oc * lax.rsqrt(var + EPS)
        gg = g_ref[sl, :]
        o_ref[sl, :] = (gg * jax.nn.sigmoid(gg) * on).astype(BF16)
        return r

    lax.fori_loop(0, n_chunk, body, jnp.zeros((RET_DK, RET_DK), F32))


def _retention(proj, cos128, sin128, tabs, bsz, seq):
    dmat, zeta_b, xi_b, gch_b = tabs
    h = RET_HEADS
    col = lambda cb: pl.BlockSpec((seq, LANE), lambda b, hh, cb=cb: (b, cb + hh))
    tab = pl.BlockSpec((seq, LANE), lambda b, hh: (b, 0))
    head = lambda r: pl.BlockSpec((1, r, LANE), lambda b, hh: (hh, 0, 0))
    return pl.pallas_call(
        functools.partial(_ret_kernel, n_chunk=seq // RET_CHUNK),
        grid=(bsz, h),
        in_specs=[col(CB_RQ), col(CB_RK), col(CB_RV), col(CB_RG), tab, tab,
                  head(RET_CHUNK), head(RET_CHUNK), head(RET_CHUNK), head(8)],
        out_specs=pl.BlockSpec((seq, LANE), lambda b, hh: (b, hh)),
        out_shape=jax.ShapeDtypeStruct((bsz * seq, D_GROUP), BF16),
        compiler_params=_cparams(("arbitrary", "arbitrary")),
        name="retention",
    )(proj, proj, proj, proj, cos128, sin128, dmat, zeta_b, xi_b, gch_b)


def _nsa_prep_kernel(kc_ref, vc_ref, pek_ref, pev_ref, wk_ref, wv_ref, ks_ref, kw_ref, cos_ref, sin_ref,
                     kcmp_ref, vcmp_ref, ksr_ref, kwr_ref):
    half = NSA_CMP_STRIDE * NSA_DH
    n16 = kcmp_ref.shape[1]
    kparts, vparts = [], []
    for i in range(NSA_CMP_STRIDE):
        rows = pl.ds(i, n16, stride=NSA_CMP_STRIDE)
        kparts.append(_rope128(kc_ref[rows, :], cos_ref[rows, :], sin_ref[rows, :]))
        vparts.append(vc_ref[rows, :])
    xr = jnp.concatenate(kparts, axis=1)
    xv = jnp.concatenate(vparts, axis=1)
    a = _dot((xr + pek_ref[0:1, :]).astype(BF16), wk_ref[0:half, :])
    b = _dot((xr + pek_ref[1:2, :]).astype(BF16), wk_ref[half:2 * half, :])
    kcm = _rms_rows(a + pltpu.roll(b, n16 - 1, axis=0))
    rowi = lax.broadcasted_iota(jnp.int32, kcm.shape, 0)
    kcmp_ref[0] = jnp.where(rowi < n16 - 1, kcm, 0.0).astype(BF16)
    av = _dot((xv + pev_ref[0:1, :]).astype(BF16), wv_ref[0:half, :])
    bv = _dot((xv + pev_ref[1:2, :]).astype(BF16), wv_ref[half:2 * half, :])
    vcm = av + pltpu.roll(bv, n16 - 1, axis=0)
    vcmp_ref[0] = jnp.where(rowi < n16 - 1, vcm, 0.0).astype(BF16)
    cos = cos_ref[...]
    sin = sin_ref[...]
    ksr_ref[...] = _rope128(_rms_rows(ks_ref[...]), cos, sin).astype(BF16)
    kwr_ref[...] = _rope128(_rms_rows(kw_ref[...]), cos, sin).astype(BF16)


def _nsa_prep(proj, cos128, sin128, pe_k, pe_v, wk_bf, wv_bf, bsz, seq):
    n16 = seq // NSA_CMP_STRIDE
    half = NSA_CMP_STRIDE * NSA_DH
    pek = pe_k.astype(F32).reshape(2, half)
    pev = pe_v.astype(F32).reshape(2, half)
    full = lambda a: pl.BlockSpec(a.shape, lambda b: (0,) * a.ndim)
    col = lambda cb: pl.BlockSpec((seq, LANE), lambda b, cb=cb: (b, cb))
    tab = pl.BlockSpec((seq, LANE), lambda b: (b, 0))
    cmp_spec = pl.BlockSpec((1, n16, NSA_DH), lambda b: (b, 0, 0))
    seq_spec = pl.BlockSpec((seq, LANE), lambda b: (b, 0))
    return pl.pallas_call(
        _nsa_prep_kernel,
        grid=(bsz,),
        in_specs=[col(CB_NKC), col(CB_NVC), full(pek), full(pev), full(wk_bf), full(wv_bf),
                  col(CB_NKS), col(CB_NKW), tab, tab],
        out_specs=[cmp_spec, cmp_spec, seq_spec, seq_spec],
        out_shape=[jax.ShapeDtypeStruct((bsz, n16, NSA_DH), BF16),
                   jax.ShapeDtypeStruct((bsz, n16, NSA_DH), BF16),
                   jax.ShapeDtypeStruct((bsz * seq, NSA_DH), BF16),
                   jax.ShapeDtypeStruct((bsz * seq, NSA_DH), BF16)],
        compiler_params=_cparams(("arbitrary",)),
        name="nsa_prep",
    )(proj, proj, pek, pev, wk_bf, wv_bf, proj, proj, cos128, sin128)


def _softmax_step(s3, mask, vv, carry):
    m, l, acc = carry
    hh, qb, kk = s3.shape
    sm = jnp.where(mask[None], s3, NEG_INF)
    m_new = jnp.maximum(m, jnp.max(sm, axis=-1, keepdims=True))
    alpha = jnp.exp(m - m_new)
    e = jnp.where(mask[None], jnp.exp(sm - m_new), 0.0)
    l = alpha * l + jnp.sum(e, axis=-1, keepdims=True)
    pv = _dot(e.astype(BF16).reshape(hh * qb, kk), vv).reshape(hh, qb, vv.shape[-1])
    return m_new, l, alpha * acc + pv


def _softmax_init(hh, qb, dh):
    return (jnp.full((hh, qb, 1), NEG_INF, F32), jnp.zeros((hh, qb, 1), F32),
            jnp.zeros((hh, qb, dh), F32))


def _softmax_finish(carry):
    _, l, acc = carry
    return jnp.where(l > 0.0, acc / jnp.where(l > 0.0, l, 1.0), 0.0)


def _nsa_kernel(q_ref, gate_ref, cos_ref, sin_ref, kc_ref, vc_ref, ks_ref, vs_ref, kw_ref, vw_ref,
                ov_ref, o_ref):
    hh, qb, dh = NSA_HEADS, Q_BLOCK, NSA_DH
    scale = dh ** -0.5
    n = pl.program_id(1)
    cos = cos_ref[...]
    sin = sin_ref[...]
    qs = [_rope128(_rms_rows(q_ref[:, h * dh:(h + 1) * dh]), cos, sin) for h in range(hh)]
    qq = jnp.concatenate(qs, axis=0).astype(BF16)
    col = lax.broadcasted_iota(jnp.int32, (qb, LANE), 1)
    t = n * qb + lax.broadcasted_iota(jnp.int32, (qb, LANE), 0)

    ncp = kc_ref.shape[1]
    colc = lax.broadcasted_iota(jnp.int32, (qb, ncp), 1)
    tc = n * qb + lax.broadcasted_iota(jnp.int32, (qb, ncp), 0)
    s3 = (_dot_nt(qq, kc_ref[0]) * scale).reshape(hh, qb, ncp)
    mask_c = (colc * NSA_CMP_STRIDE + (NSA_CMP_LEN - 1)) <= tc
    sm = jnp.where(mask_c[None], s3, NEG_INF)
    e = jnp.exp(sm - jnp.max(sm, axis=-1, keepdims=True))
    p = jnp.where(mask_c[None], e / jnp.sum(e, axis=-1, keepdims=True), 0.0)
    o_cmp = _dot(p.astype(BF16).reshape(hh * qb, ncp), vc_ref[0]).reshape(hh, qb, dh)

    imp = _dot(jnp.sum(p, axis=0), ov_ref[...], precision=HIGHEST)
    cur = t >> NSA_SEL_SHIFT
    forced = (col == 0) | (col == cur) | (col == cur - 1)
    score = jnp.where(col <= cur, imp + jnp.where(forced, NSA_FORCE_BONUS, 0.0), NEG_INF)
    n_sel = ks_ref.shape[0] // NSA_SEL_LEN
    rank = jnp.zeros((qb, LANE), F32)
    for jp in range(n_sel):
        cj = score[:, jp:jp + 1]
        tie = jnp.where(col > jp, 1.0, 0.0)
        rank = rank + jnp.where(cj > score, 1.0, jnp.where(cj == score, tie, 0.0))
    selm = jnp.where(rank < float(min(NSA_TOPK, n_sel)), 1.0, 0.0).astype(BF16)

    rowi = lax.broadcasted_iota(jnp.int32, (LANE, LANE), 0)
    coli = lax.broadcasted_iota(jnp.int32, (LANE, LANE), 1)

    def sel_body(kt, carry):
        sl = pl.ds(pl.multiple_of(kt * qb, qb), qb)
        kk = ks_ref[sl, :]
        vv = vs_ref[sl, :].astype(BF16)
        s = (_dot_nt(qq, kk) * scale).reshape(hh, qb, qb)
        expand = jnp.where(rowi == kt * (qb // NSA_SEL_LEN) + (coli >> NSA_SEL_SHIFT), 1.0, 0.0)
        picked = _dot(selm, expand.astype(BF16))
        kpos = kt * qb + col
        mask = jnp.where(kpos <= t, picked, 0.0) > 0.5
        return _softmax_step(s, mask, vv, carry)

    o_sel = _softmax_finish(lax.fori_loop(0, n + 1, sel_body, _softmax_init(hh, qb, dh)))

    carry = _softmax_init(hh, qb, dh)
    for i in range(NSA_WINDOW // qb + 1):
        tile = n - NSA_WINDOW // qb + i
        tidx = jnp.maximum(tile, 0)
        sl = pl.ds(pl.multiple_of(tidx * qb, qb), qb)
        kk = kw_ref[sl, :]
        vv = vw_ref[sl, :].astype(BF16)
        s = (_dot_nt(qq, kk) * scale).reshape(hh, qb, qb)
        dist = t - (tile * qb + col)
        inwin = jnp.where(dist >= 0, jnp.where(dist < NSA_WINDOW, 1.0, 0.0), 0.0)
        mask = jnp.where(tile >= 0, inwin, 0.0) > 0.5
        carry = _softmax_step(s, mask, vv, carry)
    o_win = _softmax_finish(carry)

    gts = jax.nn.sigmoid(gate_ref[...])
    for h in range(hh):
        o = (gts[:, 3 * h:3 * h + 1] * o_cmp[h] + gts[:, 3 * h + 1:3 * h + 2] * o_sel[h]
             + gts[:, 3 * h + 2:3 * h + 3] * o_win[h])
        o_ref[:, h * dh:(h + 1) * dh] = o.astype(BF16)


def _nsa_overlap_table(seq):
    n_cmp = (seq - NSA_CMP_LEN) // NSA_CMP_STRIDE + 1
    n_sel = seq // NSA_SEL_LEN
    cmp_start = np.arange(n_cmp) * NSA_CMP_STRIDE
    sel_start = np.arange(n_sel) * NSA_SEL_LEN
    overlap = ((cmp_start[None, :] < sel_start[:, None] + NSA_SEL_LEN)
               & (cmp_start[None, :] + NSA_CMP_LEN > sel_start[:, None])).astype(np.float32)
    tab = np.zeros((n_cmp + 1, LANE), np.float32)
    tab[:n_cmp, :n_sel] = overlap.T
    return jnp.asarray(tab)


def _nsa_attn(proj, cos128, sin128, kcmp, vcmp, ksr, kwr, bsz, seq):
    nb = seq // Q_BLOCK
    qspec = pl.BlockSpec((Q_BLOCK, D_GROUP), lambda b, n: (b * nb + n, CB_NQ // 4))
    qcol = lambda cb: pl.BlockSpec((Q_BLOCK, LANE), lambda b, n, cb=cb: (b * nb + n, cb))
    qtab = pl.BlockSpec((Q_BLOCK, LANE), lambda b, n: (b * nb + n, 0))
    cmp_spec = pl.BlockSpec((1, kcmp.shape[1], NSA_DH), lambda b, n: (b, 0, 0))
    kseq = pl.BlockSpec((seq, LANE), lambda b, n: (b, 0))
    vcol = lambda cb: pl.BlockSpec((seq, LANE), lambda b, n, cb=cb: (b, cb))
    ov = _nsa_overlap_table(seq)
    return pl.pallas_call(
        _nsa_kernel,
        grid=(bsz, nb),
        in_specs=[qspec, qcol(CB_GATE), qtab, qtab, cmp_spec, cmp_spec,
                  kseq, vcol(CB_NVS), kseq, vcol(CB_NVW),
                  pl.BlockSpec(ov.shape, lambda b, n: (0, 0))],
        out_specs=pl.BlockSpec((Q_BLOCK, D_GROUP), lambda b, n: (b * nb + n, 0)),
        out_shape=jax.ShapeDtypeStruct((bsz * seq, D_GROUP), BF16),
        compiler_params=_cparams(("arbitrary", "arbitrary")),
        name="nsa_attn",
    )(proj, proj, cos128, sin128, kcmp, vcmp, ksr, proj, kwr, proj, ov)


def _diff_prep_kernel(q_ref, k_ref, cos_ref, sin_ref, qt_ref, ko_ref):
    cos = cos_ref[...]
    sin = sin_ref[...]
    lane = lax.broadcasted_iota(jnp.int32, cos.shape, 1)
    lo = lane < DIFF_DH
    first = (lane & (DIFF_DH - 1)) < DIFF_DH // 2

    def prep(x):
        x2 = x * x
        ss_lo = jnp.sum(jnp.where(lo, x2, 0.0), axis=-1, keepdims=True)
        ss_hi = jnp.sum(jnp.where(lo, 0.0, x2), axis=-1, keepdims=True)
        ms = jnp.where(lo, ss_lo, ss_hi) * (1.0 / DIFF_DH)
        xn = x * lax.rsqrt(ms + EPS)
        partner = jnp.where(first, pltpu.roll(xn, LANE - DIFF_DH // 2, axis=1),
                            pltpu.roll(xn, DIFF_DH // 2, axis=1))
        return xn * cos + partner * sin

    for h in range(DIFF_HEADS):
        sl = slice(h * LANE, (h + 1) * LANE)
        qt_ref[sl, :] = (prep(q_ref[:, sl]) * (DIFF_DH ** -0.5)).T.astype(BF16)
        ko_ref[:, sl] = prep(k_ref[:, sl]).astype(BF16)


def _diff_prep(proj, cos64, sin64, tm):
    t = proj.shape[0]
    blk = lambda cb: pl.BlockSpec((tm, D_GROUP), lambda i, cb=cb: (i, cb // 4))
    tab = pl.BlockSpec((tm, LANE), lambda i: (i, 0))
    return pl.pallas_call(
        _diff_prep_kernel,
        grid=(t // tm,),
        in_specs=[blk(CB_DQ), blk(CB_DK), tab, tab],
        out_specs=[pl.BlockSpec((D_GROUP, tm), lambda i: (0, i)),
                   pl.BlockSpec((tm, D_GROUP), lambda i: (i, 0))],
        out_shape=[jax.ShapeDtypeStruct((D_GROUP, t), BF16), jax.ShapeDtypeStruct((t, D_GROUP), BF16)],
        compiler_params=_cparams(("arbitrary",)),
        name="diff_prep",
    )(proj, proj, cos64, sin64)


def _diff_kernel(lam_ref, qt_ref, k_ref, v_ref, o_ref, acc_ref, l_ref, *, out_scale):
    nq = DIFF_TQ
    qi = pl.program_id(2)
    qt = qt_ref[...]
    row = lax.broadcasted_iota(jnp.int32, qt.shape, 0)
    zero = jnp.zeros_like(qt)
    qq = jnp.concatenate([jnp.where(row < DIFF_DH, qt, zero), jnp.where(row < DIFF_DH, zero, qt)], axis=1)
    acc_ref[...] = jnp.zeros_like(acc_ref)
    l_ref[...] = jnp.zeros_like(l_ref)

    def tile(kt, masked):
        sl = pl.ds(pl.multiple_of(kt * nq, nq), nq)
        p = jnp.exp(_dot(k_ref[sl, :], qq))
        if masked:
            kpos = lax.broadcasted_iota(jnp.int32, p.shape, 0)
            qpos = lax.broadcasted_iota(jnp.int32, p.shape, 1) & (nq - 1)
            p = jnp.where(kpos <= qpos, p, 0.0)
        l_ref[...] += jnp.sum(p, axis=0, keepdims=True)
        acc_ref[...] += _dot_tn(v_ref[sl, :].astype(BF16), p.astype(BF16))

    def body(kt, c):
        tile(kt, False)
        return c

    lax.fori_loop(0, qi, body, 0)
    tile(qi, True)
    l = l_ref[...]
    acc = acc_ref[...]
    o = acc[:, :nq] / l[:, :nq] - lam_ref[0, 0] * (acc[:, nq:] / l[:, nq:])
    on = o * lax.rsqrt(jnp.mean(o * o, axis=0, keepdims=True) + EPS) * out_scale
    o_ref[...] = on.T.astype(BF16)


def _diff_attn(lam, qt, kd, proj, bsz, seq, lam_init):
    nq = DIFF_TQ
    nb = seq // nq
    h = DIFF_HEADS
    return pl.pallas_call(
        functools.partial(_diff_kernel, out_scale=1.0 - lam_init),
        grid=(bsz, h, nb),
        in_specs=[pl.BlockSpec(memory_space=pltpu.SMEM),
                  pl.BlockSpec((LANE, nq), lambda b, hh, n: (hh, b * nb + n)),
                  pl.BlockSpec((seq, LANE), lambda b, hh, n: (b, hh)),
                  pl.BlockSpec((seq, LANE), lambda b, hh, n: (b, CB_DV + hh))],
        out_specs=pl.BlockSpec((nq, LANE), lambda b, hh, n: (b * nb + n, hh)),
        out_shape=jax.ShapeDtypeStruct((bsz * seq, D_GROUP), BF16),
        scratch_shapes=[pltpu.VMEM((LANE, 2 * nq), F32), pltpu.VMEM((1, 2 * nq), F32)],
        compiler_params=_cparams(("arbitrary", "arbitrary", "arbitrary")),
        name="diff_attn",
    )(lam, qt, kd, proj)


def _outproj_kernel(ya_ref, yb_ref, yc_ref, yd_ref, w_ref, x_ref, g_ref, xo_ref, h_ref):
    mixed = jnp.concatenate([ya_ref[...], yb_ref[...], yc_ref[...], yd_ref[...]], axis=1)
    x = x_ref[...] + _dot(mixed, w_ref[0])
    xo_ref[...] = x
    h_ref[...] = (_rms_rows(x) * g_ref[...]).astype(BF16)


def _outproj(ys, w3_bf, layer, x2, g, tm):
    t, d = x2.shape
    yspec = pl.BlockSpec((tm, D_GROUP), lambda i: (i, 0))
    row = pl.BlockSpec((tm, d), lambda i: (i, 0))
    return pl.pallas_call(
        _outproj_kernel,
        grid=(t // tm,),
        in_specs=[yspec, yspec, yspec, yspec,
                  pl.BlockSpec((1, d, d), lambda i: (layer, 0, 0)), row,
                  pl.BlockSpec((1, d), lambda i: (0, 0))],
        out_specs=[row, row],
        out_shape=[jax.ShapeDtypeStruct((t, d), F32), jax.ShapeDtypeStruct((t, d), BF16)],
        compiler_params=_cparams(("arbitrary",)),
        name="outproj",
    )(*ys, w3_bf, x2, g)


def _mlp_kernel(h_ref, x_ref, w1_ref, w2_ref, o_ref):
    @pl.when(pl.program_id(1) == 0)
    def _():
        o_ref[...] = x_ref[...]

    a = _dot(h_ref[...], w1_ref[0].astype(BF16))
    a = jnp.square(jnp.maximum(a, 0.0)).astype(BF16)
    o_ref[...] += _dot(a, w2_ref[0].astype(BF16))


def _mlp(h2, x2, w1, w2, layer, tm, tf):
    t, d = x2.shape
    f = w1.shape[2]
    return pl.pallas_call(
        _mlp_kernel,
        grid=(t // tm, f // tf),
        in_specs=[pl.BlockSpec((tm, d), lambda i, j: (i, 0)),
                  pl.BlockSpec((tm, d), lambda i, j: (i, 0), pipeline_mode=pl.Buffered(1)),
                  pl.BlockSpec((1, d, tf), lambda i, j: (layer, 0, j)),
                  pl.BlockSpec((1, tf, d), lambda i, j: (layer, j, 0))],
        out_specs=pl.BlockSpec((tm, d), lambda i, j: (i, 0)),
        out_shape=jax.ShapeDtypeStruct((t, d), F32),
        compiler_params=_cparams(("arbitrary", "arbitrary")),
        name="mlp",
    )(h2, x2, w1, w2)


def _rope_tables(positions):
    bsz, seq = positions.shape
    pos = positions.astype(F32)[..., None]

    def tab(d):
        inv = ROPE_THETA ** (-jnp.arange(0, d, 2, dtype=F32) / d)
        ang = pos * inv
        return jnp.cos(ang), jnp.sin(ang)

    c, s = tab(NSA_DH)
    cos128 = jnp.concatenate([c, c], axis=-1).reshape(bsz * seq, LANE)
    sin128 = jnp.concatenate([-s, s], axis=-1).reshape(bsz * seq, LANE)
    c, s = tab(DIFF_DH)
    cos64 = jnp.concatenate([c, c, c, c], axis=-1).reshape(bsz * seq, LANE)
    sin64 = jnp.concatenate([-s, s, -s, s], axis=-1).reshape(bsz * seq, LANE)
    return cos128, sin128, cos64, sin64


def kernel(x, positions, norm1_g, w_in, s5_lambda_re, s5_lambda_im, s5_log_dt, s5_b_re, s5_b_im, s5_c_re, s5_c_im, s5_d, s5_w_glu, nsa_pe_k, nsa_pe_v, nsa_w_cmp_k, nsa_w_cmp_v, diff_lq1, diff_lk1, diff_lq2, diff_lk2, w_out, norm2_g, mlp_w1, mlp_w2):
    bsz, seq, d = x.shape
    t = bsz * seq
    depth = w_in.shape[0]
    tm = min(1024, t)
    cos128, sin128, cos64, sin64 = _rope_tables(positions)
    ret_tabs = _ret_tables()
    x2 = x.reshape(t, d).astype(F32)
    w_in = w_in.astype(F32)
    w_out_bf = w_out.astype(BF16)
    w_glu_bf = s5_w_glu.astype(BF16)
    wk_bf = nsa_w_cmp_k.astype(BF16)
    wv_bf = nsa_w_cmp_v.astype(BF16)
    for layer in range(depth):
        g1 = norm1_g[layer].astype(F32)[None, :]
        proj = _inproj(x2, g1, w_in, layer, MAIN_TILES, tm, IN_TN)
        proj_d = _inproj(x2, g1, w_in[layer:layer + 1, :, GATE_RAW_END:], 0, 3 * D_GROUP // IN_TN, tm, IN_TN)

        s5_tabs = _s5_tables(s5_lambda_re[layer], s5_lambda_im[layer], s5_log_dt[layer],
                             s5_b_re[layer], s5_b_im[layer], s5_c_re[layer], s5_c_im[layer],
                             s5_d[layer], seq // S5_CHUNK)
        y_a = _s5_mixer(proj, bsz, seq, s5_tabs, w_glu_bf[layer])

        y_b = _retention(proj, cos128, sin128, ret_tabs, bsz, seq)

        kcmp, vcmp, ksr, kwr = _nsa_prep(proj, cos128, sin128, nsa_pe_k[layer], nsa_pe_v[layer],
                                         wk_bf[layer], wv_bf[layer], bsz, seq)
        y_c = _nsa_attn(proj, cos128, sin128, kcmp, vcmp, ksr, kwr, bsz, seq)

        lam_init = 0.8 - 0.6 * math.exp(-0.3 * layer)
        lam = (jnp.exp(jnp.sum(diff_lq1[layer].astype(F32) * diff_lk1[layer].astype(F32)))
               - jnp.exp(jnp.sum(diff_lq2[layer].astype(F32) * diff_lk2[layer].astype(F32))) + lam_init)
        qt, kd = _diff_prep(proj_d, cos64, sin64, tm)
        y_d = _diff_attn(lam.reshape(1, 1).astype(F32), qt, kd, proj_d, bsz, seq, lam_init)

        x2, h2 = _outproj((y_a, y_b, y_c, y_d), w_out_bf, layer, x2,
                          norm2_g[layer].astype(F32)[None, :], min(512, t))
        x2 = _mlp(h2, x2, mlp_w1.astype(F32), mlp_w2.astype(F32), layer, tm, 512)
    return x2.reshape(bsz, seq, d).astype(x.dtype)
```

```python
import functools
import math

import jax
import jax.numpy as jnp
import numpy as np
from jax import lax
from jax.experimental import pallas as pl
from jax.experimental.pallas import tpu as pltpu

F32 = jnp.float32
BF16 = jnp.bfloat16
HIGHEST = lax.Precision.HIGHEST

D_MODEL = 2048
D_GROUP = 512
D_FF = 4 * D_MODEL
ROPE_THETA = 10000.0
EPS = 1e-6
Q_BLOCK = 128
NEG_INF = -1e30

S5_GROUP = 16
S5_G = 32
S5_P = 64
S5_CHUNK = 16

RET_HEADS = 4
RET_DK = 128
RET_CHUNK = 128

NSA_HEADS = 4
NSA_DH = 128
NSA_CMP_LEN = 32
NSA_CMP_STRIDE = 16
NSA_SEL_LEN = 64
NSA_TOPK = 8
NSA_WINDOW = 256
NSA_FORCE_BONUS = 1e4
NSA_SEL_SHIFT = 6

DIFF_HEADS = 4
DIFF_DH = 64
DIFF_TQ = 256

LANE = 128
GATE_RAW_END = 3852
IN_TN = 512
MAIN_TILES = 8
CB_U, CB_RQ, CB_RK, CB_RV, CB_RG, CB_NQ = 0, 4, 8, 12, 16, 20
CB_NKC, CB_NVC, CB_NKS, CB_NVS, CB_NKW, CB_NVW, CB_GATE = 24, 25, 26, 27, 28, 29, 30
CB_DQ, CB_DK, CB_DV = 0, 4, 8

VMEM_LIMIT = 56 * 1024 * 1024


def _cparams(sem):
    return pltpu.CompilerParams(dimension_semantics=sem, vmem_limit_bytes=VMEM_LIMIT)


def _dot(a, b, **kw):
    return jnp.dot(a, b, preferred_element_type=F32, **kw)


def _dot_nt(a, b):
    return lax.dot_general(a, b, (((1,), (1,)), ((), ())), preferred_element_type=F32)


def _dot_tn(a, b):
    return lax.dot_general(a, b, (((0,), (0,)), ((), ())), preferred_element_type=F32)


def _rms_rows(x):
    return x * lax.rsqrt(jnp.mean(x * x, axis=-1, keepdims=True) + EPS)


def _rope128(x, cos, sin_signed):
    return x * cos + pltpu.roll(x, 64, axis=1) * sin_signed


def _inproj_kernel(x_ref, g_ref, w_ref, o_ref, xn_ref):
    @pl.when(pl.program_id(1) == 0)
    def _():
        x = x_ref[...]
        xn_ref[...] = (_rms_rows(x) * g_ref[...]).astype(BF16)

    o_ref[...] = _dot(xn_ref[...], w_ref[0].astype(BF16))


def _inproj(x2, g, w3, layer, n_tiles, tm, tn):
    t, d = x2.shape
    return pl.pallas_call(
        _inproj_kernel,
        grid=(t // tm, n_tiles),
        in_specs=[pl.BlockSpec((tm, d), lambda i, j: (i, 0)),
                  pl.BlockSpec((1, d), lambda i, j: (0, 0)),
                  pl.BlockSpec((1, d, tn), lambda i, j: (layer, 0, j))],
        out_specs=pl.BlockSpec((tm, tn), lambda i, j: (i, j)),
        out_shape=jax.ShapeDtypeStruct((t, n_tiles * tn), F32),
        scratch_shapes=[pltpu.VMEM((tm, d), BF16)],
        compiler_params=_cparams(("arbitrary", "arbitrary")),
        name="inproj",
    )(x2, g, w3)


def _s5_tables(lam_re, lam_im, log_dt, b_re, b_im, c_re, c_im, d_skip, n_chunk):
    tc = S5_CHUNK
    dt = jnp.exp(log_dt.astype(F32))[:, None]
    lam_re = lam_re.astype(F32)
    lam_im = lam_im.astype(F32)
    lre = lam_re * dt
    lim = lam_im * dt

    def lbpow(k):
        k = jnp.asarray(k, F32)[..., None, None]
        mag = jnp.exp(lre * k)
        return mag * jnp.cos(lim * k), mag * jnp.sin(lim * k)

    lb_re, lb_im = lbpow(1.0)
    den = lam_re * lam_re + lam_im * lam_im
    f_re = ((lb_re - 1.0) * lam_re + lb_im * lam_im) / den
    f_im = (lb_im * lam_re - (lb_re - 1.0) * lam_im) / den
    b_re = b_re.astype(F32)
    b_im = b_im.astype(F32)
    bb_re = f_re[..., None] * b_re - f_im[..., None] * b_im
    bb_im = f_re[..., None] * b_im + f_im[..., None] * b_re
    c_re = c_re.astype(F32)
    c_im = c_im.astype(F32)

    pr, pi = lbpow(jnp.arange(tc + 1))
    lb_b_re = pr[..., None] * bb_re[None] - pi[..., None] * bb_im[None]
    lb_b_im = pr[..., None] * bb_im[None] + pi[..., None] * bb_re[None]
    kk = (jnp.einsum('gnp,kgph->gknh', c_re, lb_b_re, precision=HIGHEST)
          - jnp.einsum('gnp,kgph->gknh', c_im, lb_b_im, precision=HIGHEST))
    ss = np.arange(tc)[:, None]
    tt = np.arange(tc)[None, :]
    lag = np.clip(tt - ss, 0, tc)
    toe = kk[:, lag]
    toe = jnp.where(jnp.asarray(tt >= ss)[None, :, :, None, None], toe, 0.0)
    g = lam_re.shape[0]
    w_intra = toe.transpose(0, 1, 4, 2, 3).reshape(g, tc * S5_GROUP, tc * S5_GROUP)

    rev = np.arange(tc)[::-1].copy()
    st_re = lb_b_re[rev]
    st_im = lb_b_im[rev]
    w_state = jnp.stack([st_re, st_im], axis=0).transpose(2, 1, 4, 0, 3)
    w_state = w_state.reshape(g, tc * S5_GROUP, 2 * S5_P)

    ar = pr[1:]
    ai = pi[1:]
    o_re = c_re[None] * ar[:, :, None, :] - c_im[None] * ai[:, :, None, :]
    o_im = -c_re[None] * ai[:, :, None, :] - c_im[None] * ar[:, :, None, :]
    w_out = jnp.stack([o_re, o_im], axis=0).transpose(2, 0, 4, 1, 3)
    w_out = w_out.reshape(g, 2 * S5_P, tc * S5_GROUP)

    n_step = int(math.log2(n_chunk))
    sr, si = lbpow(tc * (2.0 ** jnp.arange(n_step)))
    row_a = jnp.concatenate([sr, sr], axis=-1)
    row_b = jnp.concatenate([-si, si], axis=-1)
    apow = jnp.stack([row_a, row_b], axis=1).reshape(2 * n_step, g, 2 * S5_P).transpose(1, 0, 2)
    apow = jnp.pad(apow, ((0, 0), (0, 16 - 2 * n_step), (0, 0)))
    d_row = d_skip.astype(F32).reshape(1, g * S5_GROUP)
    return (w_intra.transpose(0, 2, 1).astype(BF16), w_state.transpose(0, 2, 1).astype(BF16),
            w_out.transpose(0, 2, 1).astype(BF16), apow, d_row)


def _s5_pack_kernel(*refs):
    at_ref = refs[-1]
    nc = at_ref.shape[2]
    for t in range(S5_CHUNK):
        for j, u_ref in enumerate(refs[:-1]):
            at_ref[t, j * LANE:(j + 1) * LANE, :] = u_ref[pl.ds(t, nc, stride=S5_CHUNK), :].T.astype(BF16)


def _u_specs(seq):
    return [pl.BlockSpec((seq, LANE), lambda b, j=j: (b, CB_U + j)) for j in range(D_GROUP // LANE)]


def _s5_pack(proj, bsz, seq):
    nc = seq // S5_CHUNK
    return pl.pallas_call(
        _s5_pack_kernel,
        grid=(bsz,),
        in_specs=_u_specs(seq),
        out_specs=pl.BlockSpec((S5_CHUNK, D_GROUP, nc), lambda b: (0, 0, b)),
        out_shape=jax.ShapeDtypeStruct((S5_CHUNK, D_GROUP, bsz * nc), BF16),
        compiler_params=_cparams(("arbitrary",)),
        name="s5_pack",
    )(*([proj] * (D_GROUP // LANE)))


def _s5_kernel(ut_ref, wi_ref, ws_ref, wo_ref, ap_ref, yt_ref, *, n_chunk):
    tc, gs, r = ut_ref.shape
    ut = ut_ref[...].reshape(tc * gs, r)
    s = _dot(ws_ref[0], ut)
    lane = lax.broadcasted_iota(jnp.int32, s.shape, 1) & (n_chunk - 1)
    apt = ap_ref[0].T
    x = s
    n_step = int(math.log2(n_chunk))
    for k in range(n_step):
        sh = 1 << k
        xs = jnp.where(lane >= sh, pltpu.roll(x, sh, axis=1), 0.0)
        xsw = jnp.concatenate([xs[S5_P:], xs[:S5_P]], axis=0)
        x = x + xs * apt[:, 2 * k:2 * k + 1] + xsw * apt[:, 2 * k + 1:2 * k + 2]
    xp = jnp.where(lane >= 1, pltpu.roll(x, 1, axis=1), 0.0)
    y = _dot(wi_ref[0], ut) + _dot(wo_ref[0], xp.astype(BF16))
    yt_ref[...] = y.reshape(tc, gs, r)


def _s5_core(at, tabs, n_chunk):
    tc, ch, r = at.shape
    w_intra_t, w_state_t, w_out_t, apow, _ = tabs
    g = w_intra_t.shape[0]
    w = tc * S5_GROUP
    return pl.pallas_call(
        functools.partial(_s5_kernel, n_chunk=n_chunk),
        grid=(g,),
        in_specs=[pl.BlockSpec((tc, S5_GROUP, r), lambda i: (0, i, 0)),
                  pl.BlockSpec((1, w, w), lambda i: (i, 0, 0)),
                  pl.BlockSpec((1, 2 * S5_P, w), lambda i: (i, 0, 0)),
                  pl.BlockSpec((1, w, 2 * S5_P), lambda i: (i, 0, 0)),
                  pl.BlockSpec((1, 16, 2 * S5_P), lambda i: (i, 0, 0))],
        out_specs=pl.BlockSpec((tc, S5_GROUP, r), lambda i: (0, i, 0)),
        out_shape=jax.ShapeDtypeStruct((tc, ch, r), F32),
        compiler_params=_cparams(("arbitrary",)),
        name="s5_core",
    )(at, w_intra_t, w_state_t, w_out_t, apow)


def _s5_out_kernel(yt_ref, u0_ref, u1_ref, u2_ref, u3_ref, d_ref, wg_ref, o_ref, y_scr):
    nc = yt_ref.shape[2]
    for t in range(S5_CHUNK):
        rows = pl.ds(t, nc, stride=S5_CHUNK)
        for j, u_ref in enumerate((u0_ref, u1_ref, u2_ref, u3_ref)):
            sl = slice(j * LANE, (j + 1) * LANE)
            y_scr[j, rows, :] = jax.nn.gelu(yt_ref[t, sl, :].T + d_ref[:, sl] * u_ref[rows, :])
    y = jnp.concatenate([y_scr[j] for j in range(D_GROUP // LANE)], axis=1)
    o_ref[...] = (y * jax.nn.sigmoid(_dot(y.astype(BF16), wg_ref[...]))).astype(BF16)


def _s5_out(yt, proj, d_row, w_glu_bf, bsz, seq):
    nc = seq // S5_CHUNK
    return pl.pallas_call(
        _s5_out_kernel,
        grid=(bsz,),
        in_specs=[pl.BlockSpec((S5_CHUNK, D_GROUP, nc), lambda b: (0, 0, b))] + _u_specs(seq)
        + [pl.BlockSpec((1, D_GROUP), lambda b: (0, 0)),
           pl.BlockSpec((D_GROUP, D_GROUP), lambda b: (0, 0))],
        out_specs=pl.BlockSpec((seq, D_GROUP), lambda b: (b, 0)),
        out_shape=jax.ShapeDtypeStruct((bsz * seq, D_GROUP), BF16),
        scratch_shapes=[pltpu.VMEM((D_GROUP // LANE, seq, LANE), F32)],
        compiler_params=_cparams(("arbitrary",)),
        name="s5_out",
    )(yt, proj, proj, proj, proj, d_row, w_glu_bf)


def _s5_mixer(proj, bsz, seq, tabs, w_glu_bf):
    yt = _s5_core(_s5_pack(proj, bsz, seq), tabs, seq // S5_CHUNK)
    return _s5_out(yt, proj, tabs[4], w_glu_bf, bsz, seq)


def _ret_tables():
    h, c = RET_HEADS, RET_CHUNK
    log_g = jnp.log(1.0 - 2.0 ** (-5.0 - jnp.arange(h, dtype=F32)))
    idx = jnp.arange(c, dtype=F32)
    diff = idx[:, None] - idx[None, :]
    dmat = jnp.where(diff >= 0, jnp.exp(jnp.maximum(diff, 0.0)[None] * log_g[:, None, None]), 0.0)
    zeta = jnp.exp((c - 1.0 - idx)[None, :] * log_g[:, None])
    xi = jnp.exp((idx + 1.0)[None, :] * log_g[:, None])
    g_chunk = jnp.exp(c * log_g)
    zeta_b = jnp.broadcast_to(zeta[:, :, None], (h, c, RET_DK))
    xi_b = jnp.broadcast_to(xi[:, :, None], (h, c, RET_DK))
    gch_b = jnp.broadcast_to(g_chunk[:, None, None], (h, 8, RET_DK))
    return dmat, zeta_b, xi_b, gch_b


def _ret_kernel(q_ref, k_ref, v_ref, g_ref, cos_ref, sin_ref, dm_ref, ze_ref, xi_ref, gc_ref,
                o_ref, *, n_chunk):
    c = RET_CHUNK
    dm = dm_ref[0]
    ze = ze_ref[0]
    xi = xi_ref[0]
    gch = gc_ref[0, 0:1, :]

    def body(n, r):
        sl = pl.ds(pl.multiple_of(n * c, c), c)
        cos = cos_ref[sl, :]
        sin = sin_ref[sl, :]
        qc = _rope128(q_ref[sl, :], cos, sin)
        kc = _rope128(k_ref[sl, :], cos, sin) * (RET_DK ** -0.5)
        vc = v_ref[sl, :].astype(BF16)
        inner = _dot_nt(qc.astype(BF16), kc.astype(BF16)) * dm
        o = _dot(inner.astype(BF16), vc) + _dot((qc * xi).astype(BF16), r.astype(BF16))
        kv = _dot_tn((kc * ze).astype(BF16), vc)
        r = r * gch + kv
        mu = jnp.mean(o, axis=-1, keepdims=True)
        oc = o - mu
        var = jnp.mean(oc * oc, axis=-1, keepdims=True)
        on = oc * lax.rsqrt(var + EPS)
        gg = g_ref[sl, :]
        o_ref[sl, :] = (gg * jax.nn.sigmoid(gg) * on).astype(BF16)
        return r

    lax.fori_loop(0, n_chunk, body, jnp.zeros((RET_DK, RET_DK), F32), unroll=2)


def _retention(proj, cos128, sin128, tabs, bsz, seq):
    dmat, zeta_b, xi_b, gch_b = tabs
    h = RET_HEADS
    col = lambda cb: pl.BlockSpec((seq, LANE), lambda b, hh, cb=cb: (b, cb + hh))
    tab = pl.BlockSpec((seq, LANE), lambda b, hh: (b, 0))
    head = lambda r: pl.BlockSpec((1, r, LANE), lambda b, hh: (hh, 0, 0))
    return pl.pallas_call(
        functools.partial(_ret_kernel, n_chunk=seq // RET_CHUNK),
        grid=(bsz, h),
        in_specs=[col(CB_RQ), col(CB_RK), col(CB_RV), col(CB_RG), tab, tab,
                  head(RET_CHUNK), head(RET_CHUNK), head(RET_CHUNK), head(8)],
        out_specs=pl.BlockSpec((seq, LANE), lambda b, hh: (b, hh)),
        out_shape=jax.ShapeDtypeStruct((bsz * seq, D_GROUP), BF16),
        compiler_params=_cparams(("arbitrary", "arbitrary")),
        name="retention",
    )(proj, proj, proj, proj, cos128, sin128, dmat, zeta_b, xi_b, gch_b)


def _nsa_prep_kernel(kc_ref, vc_ref, pek_ref, pev_ref, wk_ref, wv_ref, ks_ref, kw_ref, cos_ref, sin_ref,
                     kcmp_ref, vcmp_ref, ksr_ref, kwr_ref):
    half = NSA_CMP_STRIDE * NSA_DH
    n16 = kcmp_ref.shape[1]
    kparts, vparts = [], []
    for i in range(NSA_CMP_STRIDE):
        rows = pl.ds(i, n16, stride=NSA_CMP_STRIDE)
        kparts.append(_rope128(kc_ref[rows, :], cos_ref[rows, :], sin_ref[rows, :]))
        vparts.append(vc_ref[rows, :])
    xr = jnp.concatenate(kparts, axis=1)
    xv = jnp.concatenate(vparts, axis=1)
    a = _dot((xr + pek_ref[0:1, :]).astype(BF16), wk_ref[0:half, :])
    b = _dot((xr + pek_ref[1:2, :]).astype(BF16), wk_ref[half:2 * half, :])
    kcm = _rms_rows(a + pltpu.roll(b, n16 - 1, axis=0))
    rowi = lax.broadcasted_iota(jnp.int32, kcm.shape, 0)
    kcmp_ref[0] = jnp.where(rowi < n16 - 1, kcm, 0.0).astype(BF16)
    av = _dot((xv + pev_ref[0:1, :]).astype(BF16), wv_ref[0:half, :])
    bv = _dot((xv + pev_ref[1:2, :]).astype(BF16), wv_ref[half:2 * half, :])
    vcm = av + pltpu.roll(bv, n16 - 1, axis=0)
    vcmp_ref[0] = jnp.where(rowi < n16 - 1, vcm, 0.0).astype(BF16)
    cos = cos_ref[...]
    sin = sin_ref[...]
    ksr_ref[...] = _rope128(_rms_rows(ks_ref[...]), cos, sin).astype(BF16)
    kwr_ref[...] = _rope128(_rms_rows(kw_ref[...]), cos, sin).astype(BF16)


def _nsa_prep(proj, cos128, sin128, pe_k, pe_v, wk_bf, wv_bf, bsz, seq):
    n16 = seq // NSA_CMP_STRIDE
    half = NSA_CMP_STRIDE * NSA_DH
    pek = pe_k.astype(F32).reshape(2, half)
    pev = pe_v.astype(F32).reshape(2, half)
    full = lambda a: pl.BlockSpec(a.shape, lambda b: (0,) * a.ndim)
    col = lambda cb: pl.BlockSpec((seq, LANE), lambda b, cb=cb: (b, cb))
    tab = pl.BlockSpec((seq, LANE), lambda b: (b, 0))
    cmp_spec = pl.BlockSpec((1, n16, NSA_DH), lambda b: (b, 0, 0))
    seq_spec = pl.BlockSpec((seq, LANE), lambda b: (b, 0))
    return pl.pallas_call(
        _nsa_prep_kernel,
        grid=(bsz,),
        in_specs=[col(CB_NKC), col(CB_NVC), full(pek), full(pev), full(wk_bf), full(wv_bf),
                  col(CB_NKS), col(CB_NKW), tab, tab],
        out_specs=[cmp_spec, cmp_spec, seq_spec, seq_spec],
        out_shape=[jax.ShapeDtypeStruct((bsz, n16, NSA_DH), BF16),
                   jax.ShapeDtypeStruct((bsz, n16, NSA_DH), BF16),
                   jax.ShapeDtypeStruct((bsz * seq, NSA_DH), BF16),
                   jax.ShapeDtypeStruct((bsz * seq, NSA_DH), BF16)],
        compiler_params=_cparams(("arbitrary",)),
        name="nsa_prep",
    )(proj, proj, pek, pev, wk_bf, wv_bf, proj, proj, cos128, sin128)


def _attn_tile(s3, mask, vv):
    hh, qb, kk = s3.shape
    e = jnp.where(mask[None], jnp.exp(s3), 0.0)
    pv = _dot(e.astype(BF16).reshape(hh * qb, kk), vv).reshape(hh, qb, vv.shape[-1])
    return jnp.sum(e, axis=-1, keepdims=True), pv, e


def _attn_finish(l, acc):
    return jnp.where(l > 0.0, acc / jnp.where(l > 0.0, l, 1.0), 0.0)


def _nsa_kernel(q_ref, gate_ref, cos_ref, sin_ref, kc_ref, vc_ref, ks_ref, vs_ref, kw_ref, vw_ref,
                ov_ref, o_ref):
    hh, qb, dh = NSA_HEADS, Q_BLOCK, NSA_DH
    scale = dh ** -0.5
    n = pl.program_id(1)
    cos = cos_ref[...]
    sin = sin_ref[...]
    qs = [_rope128(_rms_rows(q_ref[:, h * dh:(h + 1) * dh]), cos, sin) for h in range(hh)]
    qq = jnp.concatenate(qs, axis=0).astype(BF16)
    col = lax.broadcasted_iota(jnp.int32, (qb, LANE), 1)
    t = n * qb + lax.broadcasted_iota(jnp.int32, (qb, LANE), 0)

    ncp = kc_ref.shape[1]
    colc = lax.broadcasted_iota(jnp.int32, (qb, ncp), 1)
    tc = n * qb + lax.broadcasted_iota(jnp.int32, (qb, ncp), 0)
    s3 = (_dot_nt(qq, kc_ref[0]) * scale).reshape(hh, qb, ncp)
    mask_c = (colc * NSA_CMP_STRIDE + (NSA_CMP_LEN - 1)) <= tc
    l_c, pv_c, e_c = _attn_tile(s3, mask_c, vc_ref[0])
    o_cmp = _attn_finish(l_c, pv_c)
    p = e_c / jnp.where(l_c > 0.0, l_c, 1.0)

    imp = _dot(jnp.sum(p, axis=0), ov_ref[...], precision=HIGHEST)
    cur = t >> NSA_SEL_SHIFT
    forced = (col == 0) | (col == cur) | (col == cur - 1)
    score = jnp.where(col <= cur, imp + jnp.where(forced, NSA_FORCE_BONUS, 0.0), NEG_INF)
    n_sel = ks_ref.shape[0] // NSA_SEL_LEN
    rank = jnp.zeros((qb, LANE), F32)
    for jp in range(n_sel):
        cj = score[:, jp:jp + 1]
        tie = jnp.where(col > jp, 1.0, 0.0)
        rank = rank + jnp.where(cj > score, 1.0, jnp.where(cj == score, tie, 0.0))
    selm = jnp.where(rank < float(min(NSA_TOPK, n_sel)), 1.0, 0.0).astype(BF16)

    rowi = lax.broadcasted_iota(jnp.int32, (LANE, LANE), 0)
    coli = lax.broadcasted_iota(jnp.int32, (LANE, LANE), 1)

    def sel_tile(kt):
        sl = pl.ds(pl.multiple_of(kt * qb, qb), qb)
        s = (_dot_nt(qq, ks_ref[sl, :]) * scale).reshape(hh, qb, qb)
        expand = jnp.where(rowi == kt * (qb // NSA_SEL_LEN) + (coli >> NSA_SEL_SHIFT), 1.0, 0.0)
        picked = _dot(selm, expand.astype(BF16))
        mask = jnp.where(kt * qb + col <= t, picked, 0.0) > 0.5
        l, pv, _ = _attn_tile(s, mask, vs_ref[sl, :].astype(BF16))
        return l, pv

    def sel_body(i, carry):
        l0, pv0 = sel_tile(2 * i)
        l1, pv1 = sel_tile(2 * i + 1)
        return carry[0] + (l0 + l1), carry[1] + (pv0 + pv1)

    zero = (jnp.zeros((hh, qb, 1), F32), jnp.zeros((hh, qb, dh), F32))
    o_sel = _attn_finish(*lax.fori_loop(0, (n + 2) >> 1, sel_body, zero))

    l_w, acc_w = zero
    for i in range(NSA_WINDOW // qb + 1):
        tile = n - NSA_WINDOW // qb + i
        tidx = jnp.maximum(tile, 0)
        sl = pl.ds(pl.multiple_of(tidx * qb, qb), qb)
        s = (_dot_nt(qq, kw_ref[sl, :]) * scale).reshape(hh, qb, qb)
        dist = t - (tile * qb + col)
        inwin = jnp.where(dist >= 0, jnp.where(dist < NSA_WINDOW, 1.0, 0.0), 0.0)
        mask = jnp.where(tile >= 0, inwin, 0.0) > 0.5
        l, pv, _ = _attn_tile(s, mask, vw_ref[sl, :].astype(BF16))
        l_w, acc_w = l_w + l, acc_w + pv
    o_win = _attn_finish(l_w, acc_w)

    gts = jax.nn.sigmoid(gate_ref[...])
    for h in range(hh):
        o = (gts[:, 3 * h:3 * h + 1] * o_cmp[h] + gts[:, 3 * h + 1:3 * h + 2] * o_sel[h]
             + gts[:, 3 * h + 2:3 * h + 3] * o_win[h])
        o_ref[:, h * dh:(h + 1) * dh] = o.astype(BF16)


def _nsa_overlap_table(seq):
    n_cmp = (seq - NSA_CMP_LEN) // NSA_CMP_STRIDE + 1
    n_sel = seq // NSA_SEL_LEN
    cmp_start = np.arange(n_cmp) * NSA_CMP_STRIDE
    sel_start = np.arange(n_sel) * NSA_SEL_LEN
    overlap = ((cmp_start[None, :] < sel_start[:, None] + NSA_SEL_LEN)
               & (cmp_start[None, :] + NSA_CMP_LEN > sel_start[:, None])).astype(np.float32)
    tab = np.zeros((n_cmp + 1, LANE), np.float32)
    tab[:n_cmp, :n_sel] = overlap.T
    return jnp.asarray(tab)


def _nsa_attn(proj, cos128, sin128, kcmp, vcmp, ksr, kwr, bsz, seq):
    nb = seq // Q_BLOCK
    qspec = pl.BlockSpec((Q_BLOCK, D_GROUP), lambda b, n: (b * nb + n, CB_NQ // 4))
    qcol = lambda cb: pl.BlockSpec((Q_BLOCK, LANE), lambda b, n, cb=cb: (b * nb + n, cb))
    qtab = pl.BlockSpec((Q_BLOCK, LANE), lambda b, n: (b * nb + n, 0))
    cmp_spec = pl.BlockSpec((1, kcmp.shape[1], NSA_DH), lambda b, n: (b, 0, 0))
    kseq = pl.BlockSpec((seq, LANE), lambda b, n: (b, 0))
    vcol = lambda cb: pl.BlockSpec((seq, LANE), lambda b, n, cb=cb: (b, cb))
    ov = _nsa_overlap_table(seq)
    return pl.pallas_call(
        _nsa_kernel,
        grid=(bsz, nb),
        in_specs=[qspec, qcol(CB_GATE), qtab, qtab, cmp_spec, cmp_spec,
                  kseq, vcol(CB_NVS), kseq, vcol(CB_NVW),
                  pl.BlockSpec(ov.shape, lambda b, n: (0, 0))],
        out_specs=pl.BlockSpec((Q_BLOCK, D_GROUP), lambda b, n: (b * nb + n, 0)),
        out_shape=jax.ShapeDtypeStruct((bsz * seq, D_GROUP), BF16),
        compiler_params=_cparams(("arbitrary", "arbitrary")),
        name="nsa_attn",
    )(proj, proj, cos128, sin128, kcmp, vcmp, ksr, proj, kwr, proj, ov)


def _diff_prep_kernel(q_ref, k_ref, cos_ref, sin_ref, qt_ref, ko_ref):
    cos = cos_ref[...]
    sin = sin_ref[...]
    lane = lax.broadcasted_iota(jnp.int32, cos.shape, 1)
    lo = lane < DIFF_DH
    first = (lane & (DIFF_DH - 1)) < DIFF_DH // 2

    def prep(x):
        x2 = x * x
        ss_lo = jnp.sum(jnp.where(lo, x2, 0.0), axis=-1, keepdims=True)
        ss_hi = jnp.sum(jnp.where(lo, 0.0, x2), axis=-1, keepdims=True)
        ms = jnp.where(lo, ss_lo, ss_hi) * (1.0 / DIFF_DH)
        xn = x * lax.rsqrt(ms + EPS)
        partner = jnp.where(first, pltpu.roll(xn, LANE - DIFF_DH // 2, axis=1),
                            pltpu.roll(xn, DIFF_DH // 2, axis=1))
        return xn * cos + partner * sin

    for h in range(DIFF_HEADS):
        sl = slice(h * LANE, (h + 1) * LANE)
        qt_ref[sl, :] = (prep(q_ref[:, sl]) * (DIFF_DH ** -0.5)).T.astype(BF16)
        ko_ref[:, sl] = prep(k_ref[:, sl]).astype(BF16)


def _diff_prep(proj, cos64, sin64, tm):
    t = proj.shape[0]
    blk = lambda cb: pl.BlockSpec((tm, D_GROUP), lambda i, cb=cb: (i, cb // 4))
    tab = pl.BlockSpec((tm, LANE), lambda i: (i, 0))
    return pl.pallas_call(
        _diff_prep_kernel,
        grid=(t // tm,),
        in_specs=[blk(CB_DQ), blk(CB_DK), tab, tab],
        out_specs=[pl.BlockSpec((D_GROUP, tm), lambda i: (0, i)),
                   pl.BlockSpec((tm, D_GROUP), lambda i: (i, 0))],
        out_shape=[jax.ShapeDtypeStruct((D_GROUP, t), BF16), jax.ShapeDtypeStruct((t, D_GROUP), BF16)],
        compiler_params=_cparams(("arbitrary",)),
        name="diff_prep",
    )(proj, proj, cos64, sin64)


def _diff_kernel(lam_ref, qt_ref, k_ref, v_ref, o_ref, acc_ref, l_ref, *, out_scale):
    nq = DIFF_TQ
    qi = pl.program_id(2)
    qt = qt_ref[...]
    row = lax.broadcasted_iota(jnp.int32, qt.shape, 0)
    zero = jnp.zeros_like(qt)
    qq = jnp.concatenate([jnp.where(row < DIFF_DH, qt, zero), jnp.where(row < DIFF_DH, zero, qt)], axis=1)
    acc_ref[...] = jnp.zeros_like(acc_ref)
    l_ref[...] = jnp.zeros_like(l_ref)

    def tile(kt, masked):
        sl = pl.ds(pl.multiple_of(kt * nq, nq), nq)
        p = jnp.exp(_dot(k_ref[sl, :], qq))
        if masked:
            kpos = lax.broadcasted_iota(jnp.int32, p.shape, 0)
            qpos = lax.broadcasted_iota(jnp.int32, p.shape, 1) & (nq - 1)
            p = jnp.where(kpos <= qpos, p, 0.0)
        return jnp.sum(p, axis=0, keepdims=True), _dot_tn(v_ref[sl, :].astype(BF16), p.astype(BF16))

    def body(i, c):
        l0, a0 = tile(2 * i, False)
        l1, a1 = tile(2 * i + 1, False)
        l_ref[...] += l0 + l1
        acc_ref[...] += a0 + a1
        return c

    lax.fori_loop(0, qi >> 1, body, 0)

    @pl.when((qi & 1) == 1)
    def _():
        l0, a0 = tile(qi - 1, False)
        l_ref[...] += l0
        acc_ref[...] += a0

    l0, a0 = tile(qi, True)
    l_ref[...] += l0
    acc_ref[...] += a0
    l = l_ref[...]
    acc = acc_ref[...]
    o = acc[:, :nq] / l[:, :nq] - lam_ref[0, 0] * (acc[:, nq:] / l[:, nq:])
    on = o * lax.rsqrt(jnp.mean(o * o, axis=0, keepdims=True) + EPS) * out_scale
    o_ref[...] = on.T.astype(BF16)


def _diff_attn(lam, qt, kd, proj, bsz, seq, lam_init):
    nq = DIFF_TQ
    nb = seq // nq
    h = DIFF_HEADS
    return pl.pallas_call(
        functools.partial(_diff_kernel, out_scale=1.0 - lam_init),
        grid=(bsz, h, nb),
        in_specs=[pl.BlockSpec(memory_space=pltpu.SMEM),
                  pl.BlockSpec((LANE, nq), lambda b, hh, n: (hh, b * nb + n)),
                  pl.BlockSpec((seq, LANE), lambda b, hh, n: (b, hh)),
                  pl.BlockSpec((seq, LANE), lambda b, hh, n: (b, CB_DV + hh))],
        out_specs=pl.BlockSpec((nq, LANE), lambda b, hh, n: (b * nb + n, hh)),
        out_shape=jax.ShapeDtypeStruct((bsz * seq, D_GROUP), BF16),
        scratch_shapes=[pltpu.VMEM((LANE, 2 * nq), F32), pltpu.VMEM((1, 2 * nq), F32)],
        compiler_params=_cparams(("arbitrary", "arbitrary", "arbitrary")),
        name="diff_attn",
    )(lam, qt, kd, proj)


def _outproj_kernel(ya_ref, yb_ref, yc_ref, yd_ref, w_ref, x_ref, g_ref, xo_ref, h_ref):
    mixed = jnp.concatenate([ya_ref[...], yb_ref[...], yc_ref[...], yd_ref[...]], axis=1)
    x = x_ref[...] + _dot(mixed, w_ref[0])
    xo_ref[...] = x
    h_ref[...] = (_rms_rows(x) * g_ref[...]).astype(BF16)


def _outproj(ys, w3_bf, layer, x2, g, tm):
    t, d = x2.shape
    yspec = pl.BlockSpec((tm, D_GROUP), lambda i: (i, 0))
    row = pl.BlockSpec((tm, d), lambda i: (i, 0))
    return pl.pallas_call(
        _outproj_kernel,
        grid=(t // tm,),
        in_specs=[yspec, yspec, yspec, yspec,
                  pl.BlockSpec((1, d, d), lambda i: (layer, 0, 0)), row,
                  pl.BlockSpec((1, d), lambda i: (0, 0))],
        out_specs=[row, row],
        out_shape=[jax.ShapeDtypeStruct((t, d), F32), jax.ShapeDtypeStruct((t, d), BF16)],
        compiler_params=_cparams(("arbitrary",)),
        name="outproj",
    )(*ys, w3_bf, x2, g)


def _mlp_kernel(h_ref, x_ref, w1_ref, w2_ref, o_ref):
    @pl.when(pl.program_id(1) == 0)
    def _():
        o_ref[...] = x_ref[...]

    a = _dot(h_ref[...], w1_ref[0].astype(BF16))
    a = jnp.square(jnp.maximum(a, 0.0)).astype(BF16)
    o_ref[...] += _dot(a, w2_ref[0].astype(BF16))


def _mlp(h2, x2, w1, w2, layer, tm, tf):
    t, d = x2.shape
    f = w1.shape[2]
    return pl.pallas_call(
        _mlp_kernel,
        grid=(t // tm, f // tf),
        in_specs=[pl.BlockSpec((tm, d), lambda i, j: (i, 0)),
                  pl.BlockSpec((tm, d), lambda i, j: (i, 0), pipeline_mode=pl.Buffered(1)),
                  pl.BlockSpec((1, d, tf), lambda i, j: (layer, 0, j)),
                  pl.BlockSpec((1, tf, d), lambda i, j: (layer, j, 0))],
        out_specs=pl.BlockSpec((tm, d), lambda i, j: (i, 0)),
        out_shape=jax.ShapeDtypeStruct((t, d), F32),
        compiler_params=_cparams(("arbitrary", "arbitrary")),
        name="mlp",
    )(h2, x2, w1, w2)


def _rope_tables(positions):
    bsz, seq = positions.shape
    pos = positions.astype(F32)[..., None]

    def tab(d):
        inv = ROPE_THETA ** (-jnp.arange(0, d, 2, dtype=F32) / d)
        ang = pos * inv
        return jnp.cos(ang), jnp.sin(ang)

    c, s = tab(NSA_DH)
    cos128 = jnp.concatenate([c, c], axis=-1).reshape(bsz * seq, LANE)
    sin128 = jnp.concatenate([-s, s], axis=-1).reshape(bsz * seq, LANE)
    c, s = tab(DIFF_DH)
    cos64 = jnp.concatenate([c, c, c, c], axis=-1).reshape(bsz * seq, LANE)
    sin64 = jnp.concatenate([-s, s, -s, s], axis=-1).reshape(bsz * seq, LANE)
    return cos128, sin128, cos64, sin64


def kernel(x, positions, norm1_g, w_in, s5_lambda_re, s5_lambda_im, s5_log_dt, s5_b_re, s5_b_im, s5_c_re, s5_c_im, s5_d, s5_w_glu, nsa_pe_k, nsa_pe_v, nsa_w_cmp_k, nsa_w_cmp_v, diff_lq1, diff_lk1, diff_lq2, diff_lk2, w_out, norm2_g, mlp_w1, mlp_w2):
    bsz, seq, d = x.shape
    t = bsz * seq
    depth = w_in.shape[0]
    tm = min(1024, t)
    cos128, sin128, cos64, sin64 = _rope_tables(positions)
    ret_tabs = _ret_tables()
    x2 = x.reshape(t, d).astype(F32)
    w_in = w_in.astype(F32)
    w_out_bf = w_out.astype(BF16)
    w_glu_bf = s5_w_glu.astype(BF16)
    wk_bf = nsa_w_cmp_k.astype(BF16)
    wv_bf = nsa_w_cmp_v.astype(BF16)
    s5_tabs = jax.vmap(functools.partial(_s5_tables, n_chunk=seq // S5_CHUNK))(
        s5_lambda_re, s5_lambda_im, s5_log_dt, s5_b_re, s5_b_im, s5_c_re, s5_c_im, s5_d)
    for layer in range(depth):
        g1 = norm1_g[layer].astype(F32)[None, :]
        proj = _inproj(x2, g1, w_in, layer, MAIN_TILES, tm, IN_TN)
        proj_d = _inproj(x2, g1, w_in[layer:layer + 1, :, GATE_RAW_END:], 0, 3 * D_GROUP // IN_TN, tm, IN_TN)

        y_a = _s5_mixer(proj, bsz, seq, tuple(a[layer] for a in s5_tabs), w_glu_bf[layer])

        y_b = _retention(proj, cos128, sin128, ret_tabs, bsz, seq)

        kcmp, vcmp, ksr, kwr = _nsa_prep(proj, cos128, sin128, nsa_pe_k[layer], nsa_pe_v[layer],
                                         wk_bf[layer], wv_bf[layer], bsz, seq)
        y_c = _nsa_attn(proj, cos128, sin128, kcmp, vcmp, ksr, kwr, bsz, seq)

        lam_init = 0.8 - 0.6 * math.exp(-0.3 * layer)
        lam = (jnp.exp(jnp.sum(diff_lq1[layer].astype(F32) * diff_lk1[layer].astype(F32)))
               - jnp.exp(jnp.sum(diff_lq2[layer].astype(F32) * diff_lk2[layer].astype(F32))) + lam_init)
        qt, kd = _diff_prep(proj_d, cos64, sin64, tm)
        y_d = _diff_attn(lam.reshape(1, 1).astype(F32), qt, kd, proj_d, bsz, seq, lam_init)

        x2, h2 = _outproj((y_a, y_b, y_c, y_d), w_out_bf, layer, x2,
                          norm2_g[layer].astype(F32)[None, :], min(512, t))
        x2 = _mlp(h2, x2, mlp_w1.astype(F32), mlp_w2.astype(F32), layer, tm, 512)
    return x2.reshape(bsz, seq, d).astype(x.dtype)
```

```python
import functools
import math

import jax
import jax.numpy as jnp
import numpy as np
from jax import lax
from jax.experimental import pallas as pl
from jax.experimental.pallas import tpu as pltpu

F32 = jnp.float32
BF16 = jnp.bfloat16
HIGHEST = lax.Precision.HIGHEST

D_MODEL = 2048
D_GROUP = 512
D_FF = 4 * D_MODEL
ROPE_THETA = 10000.0
EPS = 1e-6
Q_BLOCK = 128
NEG_INF = -1e30

S5_GROUP = 16
S5_G = 32
S5_P = 64
S5_CHUNK = 16

RET_HEADS = 4
RET_DK = 128
RET_CHUNK = 128

NSA_HEADS = 4
NSA_DH = 128
NSA_CMP_LEN = 32
NSA_CMP_STRIDE = 16
NSA_SEL_LEN = 64
NSA_TOPK = 8
NSA_WINDOW = 256
NSA_FORCE_BONUS = 1e4
NSA_SEL_SHIFT = 6
NSA_SEL_TK = 256

DIFF_HEADS = 4
DIFF_DH = 64
DIFF_TQ = 256

LANE = 128
GATE_RAW_END = 3852
IN_TN = 512
MAIN_TILES = 8
CB_U, CB_RQ, CB_RK, CB_RV, CB_RG, CB_NQ = 0, 4, 8, 12, 16, 20
CB_NKC, CB_NVC, CB_NKS, CB_NVS, CB_NKW, CB_NVW, CB_GATE = 24, 25, 26, 27, 28, 29, 30
CB_DQ, CB_DK, CB_DV = 32, 36, 40

VMEM_LIMIT = 56 * 1024 * 1024


def _cparams(sem):
    return pltpu.CompilerParams(dimension_semantics=sem, vmem_limit_bytes=VMEM_LIMIT)


def _dot(a, b, **kw):
    return jnp.dot(a, b, preferred_element_type=F32, **kw)


def _dot_nt(a, b):
    return lax.dot_general(a, b, (((1,), (1,)), ((), ())), preferred_element_type=F32)


def _dot_tn(a, b):
    return lax.dot_general(a, b, (((0,), (0,)), ((), ())), preferred_element_type=F32)


def _rms_rows(x):
    return x * lax.rsqrt(jnp.mean(x * x, axis=-1, keepdims=True) + EPS)


def _rope128(x, cos, sin_signed):
    return x * cos + pltpu.roll(x, 64, axis=1) * sin_signed


def _inproj_kernel(x_ref, g_ref, wm_ref, wd_ref, o_ref, xn_ref):
    j = pl.program_id(1)

    @pl.when(j == 0)
    def _():
        x = x_ref[...]
        xn_ref[...] = (_rms_rows(x) * g_ref[...]).astype(BF16)

    @pl.when(j < MAIN_TILES)
    def _():
        o_ref[...] = _dot(xn_ref[...], wm_ref[0].astype(BF16))

    @pl.when(j >= MAIN_TILES)
    def _():
        o_ref[...] = _dot(xn_ref[...], wd_ref[0].astype(BF16))


def _inproj(x2, g, w3, w_diff, layer, tm, tn):
    t, d = x2.shape
    n_tiles = MAIN_TILES + w_diff.shape[2] // tn
    return pl.pallas_call(
        _inproj_kernel,
        grid=(t // tm, n_tiles),
        in_specs=[pl.BlockSpec((tm, d), lambda i, j: (i, 0)),
                  pl.BlockSpec((1, d), lambda i, j: (0, 0)),
                  pl.BlockSpec((1, d, tn), lambda i, j: (layer, 0, jnp.minimum(j, MAIN_TILES - 1))),
                  pl.BlockSpec((1, d, tn), lambda i, j: (0, 0, jnp.maximum(j - MAIN_TILES, 0)))],
        out_specs=pl.BlockSpec((tm, tn), lambda i, j: (i, j)),
        out_shape=jax.ShapeDtypeStruct((t, n_tiles * tn), F32),
        scratch_shapes=[pltpu.VMEM((tm, d), BF16)],
        compiler_params=_cparams(("arbitrary", "arbitrary")),
        name="inproj",
    )(x2, g, w3, w_diff)


def _s5_tables(lam_re, lam_im, log_dt, b_re, b_im, c_re, c_im, d_skip, n_chunk):
    tc = S5_CHUNK
    dt = jnp.exp(log_dt.astype(F32))[:, None]
    lam_re = lam_re.astype(F32)
    lam_im = lam_im.astype(F32)
    lre = lam_re * dt
    lim = lam_im * dt

    def lbpow(k):
        k = jnp.asarray(k, F32)[..., None, None]
        mag = jnp.exp(lre * k)
        return mag * jnp.cos(lim * k), mag * jnp.sin(lim * k)

    lb_re, lb_im = lbpow(1.0)
    den = lam_re * lam_re + lam_im * lam_im
    f_re = ((lb_re - 1.0) * lam_re + lb_im * lam_im) / den
    f_im = (lb_im * lam_re - (lb_re - 1.0) * lam_im) / den
    b_re = b_re.astype(F32)
    b_im = b_im.astype(F32)
    bb_re = f_re[..., None] * b_re - f_im[..., None] * b_im
    bb_im = f_re[..., None] * b_im + f_im[..., None] * b_re
    c_re = c_re.astype(F32)
    c_im = c_im.astype(F32)

    pr, pi = lbpow(jnp.arange(tc + 1))
    lb_b_re = pr[..., None] * bb_re[None] - pi[..., None] * bb_im[None]
    lb_b_im = pr[..., None] * bb_im[None] + pi[..., None] * bb_re[None]
    kk = (jnp.einsum('gnp,kgph->gknh', c_re, lb_b_re, precision=HIGHEST)
          - jnp.einsum('gnp,kgph->gknh', c_im, lb_b_im, precision=HIGHEST))
    ss = np.arange(tc)[:, None]
    tt = np.arange(tc)[None, :]
    lag = np.clip(tt - ss, 0, tc)
    toe = kk[:, lag]
    toe = jnp.where(jnp.asarray(tt >= ss)[None, :, :, None, None], toe, 0.0)
    g = lam_re.shape[0]
    w_intra = toe.transpose(0, 1, 4, 2, 3).reshape(g, tc * S5_GROUP, tc * S5_GROUP)

    rev = np.arange(tc)[::-1].copy()
    st_re = lb_b_re[rev]
    st_im = lb_b_im[rev]
    w_state = jnp.stack([st_re, st_im], axis=0).transpose(2, 1, 4, 0, 3)
    w_state = w_state.reshape(g, tc * S5_GROUP, 2 * S5_P)

    ar = pr[1:]
    ai = pi[1:]
    o_re = c_re[None] * ar[:, :, None, :] - c_im[None] * ai[:, :, None, :]
    o_im = -c_re[None] * ai[:, :, None, :] - c_im[None] * ar[:, :, None, :]
    w_out = jnp.stack([o_re, o_im], axis=0).transpose(2, 0, 4, 1, 3)
    w_out = w_out.reshape(g, 2 * S5_P, tc * S5_GROUP)

    n_step = int(math.log2(n_chunk))
    sr, si = lbpow(tc * (2.0 ** jnp.arange(n_step)))
    row_a = jnp.concatenate([sr, sr], axis=-1)
    row_b = jnp.concatenate([-si, si], axis=-1)
    apow = jnp.stack([row_a, row_b], axis=1).reshape(2 * n_step, g, 2 * S5_P).transpose(1, 0, 2)
    apow = jnp.pad(apow, ((0, 0), (0, 16 - 2 * n_step), (0, 0)))
    d_row = d_skip.astype(F32).reshape(1, g * S5_GROUP)
    return (w_intra.transpose(0, 2, 1).astype(BF16), w_state.transpose(0, 2, 1).astype(BF16),
            w_out.transpose(0, 2, 1).astype(BF16), apow, d_row)


def _s5_pack_kernel(*refs):
    at_ref = refs[-1]
    nc = at_ref.shape[2]
    for t in range(S5_CHUNK):
        for j, u_ref in enumerate(refs[:-1]):
            at_ref[t, j * LANE:(j + 1) * LANE, :] = u_ref[pl.ds(t, nc, stride=S5_CHUNK), :].T.astype(BF16)


def _u_specs(seq):
    return [pl.BlockSpec((seq, LANE), lambda b, j=j: (b, CB_U + j)) for j in range(D_GROUP // LANE)]


def _s5_pack(proj, bsz, seq):
    nc = seq // S5_CHUNK
    return pl.pallas_call(
        _s5_pack_kernel,
        grid=(bsz,),
        in_specs=_u_specs(seq),
        out_specs=pl.BlockSpec((S5_CHUNK, D_GROUP, nc), lambda b: (0, 0, b)),
        out_shape=jax.ShapeDtypeStruct((S5_CHUNK, D_GROUP, bsz * nc), BF16),
        compiler_params=_cparams(("arbitrary",)),
        name="s5_pack",
    )(*([proj] * (D_GROUP // LANE)))


def _s5_kernel(ut_ref, wi_ref, ws_ref, wo_ref, ap_ref, yt_ref, *, n_chunk):
    tc, gs, r = ut_ref.shape
    ut = ut_ref[...].reshape(tc * gs, r)
    s = _dot(ws_ref[0], ut)
    lane = lax.broadcasted_iota(jnp.int32, s.shape, 1) & (n_chunk - 1)
    apt = ap_ref[0].T
    x = s
    n_step = int(math.log2(n_chunk))
    for k in range(n_step):
        sh = 1 << k
        xs = jnp.where(lane >= sh, pltpu.roll(x, sh, axis=1), 0.0)
        xsw = jnp.concatenate([xs[S5_P:], xs[:S5_P]], axis=0)
        x = x + xs * apt[:, 2 * k:2 * k + 1] + xsw * apt[:, 2 * k + 1:2 * k + 2]
    xp = jnp.where(lane >= 1, pltpu.roll(x, 1, axis=1), 0.0)
    y = _dot(wi_ref[0], ut) + _dot(wo_ref[0], xp.astype(BF16))
    yt_ref[...] = y.reshape(tc, gs, r)


def _s5_core(at, tabs, n_chunk):
    tc, ch, r = at.shape
    w_intra_t, w_state_t, w_out_t, apow, _ = tabs
    g = w_intra_t.shape[0]
    w = tc * S5_GROUP
    return pl.pallas_call(
        functools.partial(_s5_kernel, n_chunk=n_chunk),
        grid=(g,),
        in_specs=[pl.BlockSpec((tc, S5_GROUP, r), lambda i: (0, i, 0)),
                  pl.BlockSpec((1, w, w), lambda i: (i, 0, 0)),
                  pl.BlockSpec((1, 2 * S5_P, w), lambda i: (i, 0, 0)),
                  pl.BlockSpec((1, w, 2 * S5_P), lambda i: (i, 0, 0)),
                  pl.BlockSpec((1, 16, 2 * S5_P), lambda i: (i, 0, 0))],
        out_specs=pl.BlockSpec((tc, S5_GROUP, r), lambda i: (0, i, 0)),
        out_shape=jax.ShapeDtypeStruct((tc, ch, r), F32),
        compiler_params=_cparams(("arbitrary",)),
        name="s5_core",
    )(at, w_intra_t, w_state_t, w_out_t, apow)


def _s5_out_kernel(yt_ref, u0_ref, u1_ref, u2_ref, u3_ref, d_ref, wg_ref, o_ref, y_scr):
    nc = yt_ref.shape[2]
    for t in range(S5_CHUNK):
        rows = pl.ds(t, nc, stride=S5_CHUNK)
        for j, u_ref in enumerate((u0_ref, u1_ref, u2_ref, u3_ref)):
            sl = slice(j * LANE, (j + 1) * LANE)
            y_scr[j, rows, :] = jax.nn.gelu(yt_ref[t, sl, :].T + d_ref[:, sl] * u_ref[rows, :])
    y = jnp.concatenate([y_scr[j] for j in range(D_GROUP // LANE)], axis=1)
    o_ref[...] = (y * jax.nn.sigmoid(_dot(y.astype(BF16), wg_ref[...]))).astype(BF16)


def _s5_out(yt, proj, d_row, w_glu_bf, bsz, seq):
    nc = seq // S5_CHUNK
    return pl.pallas_call(
        _s5_out_kernel,
        grid=(bsz,),
        in_specs=[pl.BlockSpec((S5_CHUNK, D_GROUP, nc), lambda b: (0, 0, b))] + _u_specs(seq)
        + [pl.BlockSpec((1, D_GROUP), lambda b: (0, 0)),
           pl.BlockSpec((D_GROUP, D_GROUP), lambda b: (0, 0))],
        out_specs=pl.BlockSpec((seq, D_GROUP), lambda b: (b, 0)),
        out_shape=jax.ShapeDtypeStruct((bsz * seq, D_GROUP), BF16),
        scratch_shapes=[pltpu.VMEM((D_GROUP // LANE, seq, LANE), F32)],
        compiler_params=_cparams(("arbitrary",)),
        name="s5_out",
    )(yt, proj, proj, proj, proj, d_row, w_glu_bf)


def _s5_mixer(proj, bsz, seq, tabs, w_glu_bf):
    yt = _s5_core(_s5_pack(proj, bsz, seq), tabs, seq // S5_CHUNK)
    return _s5_out(yt, proj, tabs[4], w_glu_bf, bsz, seq)


def _ret_tables():
    h, c = RET_HEADS, RET_CHUNK
    log_g = jnp.log(1.0 - 2.0 ** (-5.0 - jnp.arange(h, dtype=F32)))
    idx = jnp.arange(c, dtype=F32)
    diff = idx[:, None] - idx[None, :]
    dmat = jnp.where(diff >= 0, jnp.exp(jnp.maximum(diff, 0.0)[None] * log_g[:, None, None]), 0.0)
    zeta = jnp.exp((c - 1.0 - idx)[None, :] * log_g[:, None])
    xi = jnp.exp((idx + 1.0)[None, :] * log_g[:, None])
    g_chunk = jnp.exp(c * log_g)
    zeta_b = jnp.broadcast_to(zeta[:, :, None], (h, c, RET_DK))
    xi_b = jnp.broadcast_to(xi[:, :, None], (h, c, RET_DK))
    gch_b = jnp.broadcast_to(g_chunk[:, None, None], (h, 8, RET_DK))
    return dmat, zeta_b, xi_b, gch_b


def _ret_kernel(q_ref, k_ref, v_ref, g_ref, cos_ref, sin_ref, dm_ref, ze_ref, xi_ref, gc_ref,
                o_ref, *, n_chunk):
    c = RET_CHUNK
    dm = dm_ref[0]
    ze = ze_ref[0]
    xi = xi_ref[0]
    gch = gc_ref[0, 0:1, :]

    def body(n, r):
        sl = pl.ds(pl.multiple_of(n * c, c), c)
        cos = cos_ref[sl, :]
        sin = sin_ref[sl, :]
        qc = _rope128(q_ref[sl, :], cos, sin)
        kc = _rope128(k_ref[sl, :], cos, sin) * (RET_DK ** -0.5)
        vc = v_ref[sl, :].astype(BF16)
        inner = _dot_nt(qc.astype(BF16), kc.astype(BF16)) * dm
        o = _dot(inner.astype(BF16), vc) + _dot((qc * xi).astype(BF16), r.astype(BF16))
        kv = _dot_tn((kc * ze).astype(BF16), vc)
        r = r * gch + kv
        mu = jnp.mean(o, axis=-1, keepdims=True)
        oc = o - mu
        var = jnp.mean(oc * oc, axis=-1, keepdims=True)
        on = oc * lax.rsqrt(var + EPS)
        gg = g_ref[sl, :]
        o_ref[sl, :] = (gg * jax.nn.sigmoid(gg) * on).astype(BF16)
        return r

    lax.fori_loop(0, n_chunk, body, jnp.zeros((RET_DK, RET_DK), F32), unroll=4)


def _retention(proj, cos128, sin128, tabs, bsz, seq):
    dmat, zeta_b, xi_b, gch_b = tabs
    h = RET_HEADS
    col = lambda cb: pl.BlockSpec((seq, LANE), lambda b, hh, cb=cb: (b, cb + hh))
    tab = pl.BlockSpec((seq, LANE), lambda b, hh: (b, 0))
    head = lambda r: pl.BlockSpec((1, r, LANE), lambda b, hh: (hh, 0, 0))
    return pl.pallas_call(
        functools.partial(_ret_kernel, n_chunk=seq // RET_CHUNK),
        grid=(bsz, h),
        in_specs=[col(CB_RQ), col(CB_RK), col(CB_RV), col(CB_RG), tab, tab,
                  head(RET_CHUNK), head(RET_CHUNK), head(RET_CHUNK), head(8)],
        out_specs=pl.BlockSpec((seq, LANE), lambda b, hh: (b, hh)),
        out_shape=jax.ShapeDtypeStruct((bsz * seq, D_GROUP), BF16),
        compiler_params=_cparams(("arbitrary", "arbitrary")),
        name="retention",
    )(proj, proj, proj, proj, cos128, sin128, dmat, zeta_b, xi_b, gch_b)


def _nsa_prep_kernel(kc_ref, vc_ref, pek_ref, pev_ref, wk_ref, wv_ref, ks_ref, kw_ref, cos_ref, sin_ref,
                     kcmp_ref, vcmp_ref, ksr_ref, kwr_ref):
    half = NSA_CMP_STRIDE * NSA_DH
    n16 = kcmp_ref.shape[1]
    kparts, vparts = [], []
    for i in range(NSA_CMP_STRIDE):
        rows = pl.ds(i, n16, stride=NSA_CMP_STRIDE)
        kparts.append(_rope128(kc_ref[rows, :], cos_ref[rows, :], sin_ref[rows, :]))
        vparts.append(vc_ref[rows, :])
    xr = jnp.concatenate(kparts, axis=1)
    xv = jnp.concatenate(vparts, axis=1)
    a = _dot((xr + pek_ref[0:1, :]).astype(BF16), wk_ref[0:half, :])
    b = _dot((xr + pek_ref[1:2, :]).astype(BF16), wk_ref[half:2 * half, :])
    kcm = _rms_rows(a + pltpu.roll(b, n16 - 1, axis=0))
    rowi = lax.broadcasted_iota(jnp.int32, kcm.shape, 0)
    kcmp_ref[0] = jnp.where(rowi < n16 - 1, kcm, 0.0).astype(BF16)
    av = _dot((xv + pev_ref[0:1, :]).astype(BF16), wv_ref[0:half, :])
    bv = _dot((xv + pev_ref[1:2, :]).astype(BF16), wv_ref[half:2 * half, :])
    vcm = av + pltpu.roll(bv, n16 - 1, axis=0)
    vcmp_ref[0] = jnp.where(rowi < n16 - 1, vcm, 0.0).astype(BF16)
    cos = cos_ref[...]
    sin = sin_ref[...]
    ksr_ref[...] = _rope128(_rms_rows(ks_ref[...]), cos, sin).astype(BF16)
    kwr_ref[...] = _rope128(_rms_rows(kw_ref[...]), cos, sin).astype(BF16)


def _nsa_prep(proj, cos128, sin128, pe_k, pe_v, wk_bf, wv_bf, bsz, seq):
    n16 = seq // NSA_CMP_STRIDE
    half = NSA_CMP_STRIDE * NSA_DH
    pek = pe_k.astype(F32).reshape(2, half)
    pev = pe_v.astype(F32).reshape(2, half)
    full = lambda a: pl.BlockSpec(a.shape, lambda b: (0,) * a.ndim)
    col = lambda cb: pl.BlockSpec((seq, LANE), lambda b, cb=cb: (b, cb))
    tab = pl.BlockSpec((seq, LANE), lambda b: (b, 0))
    cmp_spec = pl.BlockSpec((1, n16, NSA_DH), lambda b: (b, 0, 0))
    seq_spec = pl.BlockSpec((seq, LANE), lambda b: (b, 0))
    return pl.pallas_call(
        _nsa_prep_kernel,
        grid=(bsz,),
        in_specs=[col(CB_NKC), col(CB_NVC), full(pek), full(pev), full(wk_bf), full(wv_bf),
                  col(CB_NKS), col(CB_NKW), tab, tab],
        out_specs=[cmp_spec, cmp_spec, seq_spec, seq_spec],
        out_shape=[jax.ShapeDtypeStruct((bsz, n16, NSA_DH), BF16),
                   jax.ShapeDtypeStruct((bsz, n16, NSA_DH), BF16),
                   jax.ShapeDtypeStruct((bsz * seq, NSA_DH), BF16),
                   jax.ShapeDtypeStruct((bsz * seq, NSA_DH), BF16)],
        compiler_params=_cparams(("arbitrary",)),
        name="nsa_prep",
    )(proj, proj, pek, pev, wk_bf, wv_bf, proj, proj, cos128, sin128)


def _attn_tile(s3, mask, vv):
    hh, qb, kk = s3.shape
    e = jnp.where(mask[None], jnp.exp(s3), 0.0)
    pv = _dot(e.astype(BF16).reshape(hh * qb, kk), vv).reshape(hh, qb, vv.shape[-1])
    return jnp.sum(e, axis=-1, keepdims=True), pv, e


def _attn_finish(l, acc):
    return jnp.where(l > 0.0, acc / jnp.where(l > 0.0, l, 1.0), 0.0)


def _nsa_kernel(q_ref, gate_ref, cos_ref, sin_ref, kc_ref, vc_ref, ks_ref, vs_ref, kw_ref, vw_ref,
                ov_ref, o_ref):
    hh, qb, dh = NSA_HEADS, Q_BLOCK, NSA_DH
    scale = dh ** -0.5
    n = pl.program_id(1)
    cos = cos_ref[...]
    sin = sin_ref[...]
    qs = [_rope128(_rms_rows(q_ref[:, h * dh:(h + 1) * dh]), cos, sin) for h in range(hh)]
    qq = jnp.concatenate(qs, axis=0).astype(BF16)
    col = lax.broadcasted_iota(jnp.int32, (qb, LANE), 1)
    t = n * qb + lax.broadcasted_iota(jnp.int32, (qb, LANE), 0)

    ncp = kc_ref.shape[1]
    colc = lax.broadcasted_iota(jnp.int32, (qb, ncp), 1)
    tc = n * qb + lax.broadcasted_iota(jnp.int32, (qb, ncp), 0)
    s3 = (_dot_nt(qq, kc_ref[0]) * scale).reshape(hh, qb, ncp)
    mask_c = (colc * NSA_CMP_STRIDE + (NSA_CMP_LEN - 1)) <= tc
    l_c, pv_c, e_c = _attn_tile(s3, mask_c, vc_ref[0])
    o_cmp = _attn_finish(l_c, pv_c)
    p = e_c / jnp.where(l_c > 0.0, l_c, 1.0)

    imp = _dot(jnp.sum(p, axis=0), ov_ref[...], precision=HIGHEST)
    cur = t >> NSA_SEL_SHIFT
    forced = (col == 0) | (col == cur) | (col == cur - 1)
    score = jnp.where(col <= cur, imp + jnp.where(forced, NSA_FORCE_BONUS, 0.0), NEG_INF)
    n_sel = ks_ref.shape[0] // NSA_SEL_LEN
    rank = jnp.zeros((qb, LANE), F32)
    for jp in range(n_sel):
        cj = score[:, jp:jp + 1]
        tie = jnp.where(col > jp, 1.0, 0.0)
        rank = rank + jnp.where(cj > score, 1.0, jnp.where(cj == score, tie, 0.0))
    selm = jnp.where(rank < float(min(NSA_TOPK, n_sel)), 1.0, 0.0).astype(BF16)

    tk = NSA_SEL_TK
    rowi = lax.broadcasted_iota(jnp.int32, (LANE, tk), 0)
    coli = lax.broadcasted_iota(jnp.int32, (LANE, tk), 1)
    colk = lax.broadcasted_iota(jnp.int32, (qb, tk), 1)
    tq = n * qb + lax.broadcasted_iota(jnp.int32, (qb, tk), 0)

    def sel_tile(kt):
        sl = pl.ds(pl.multiple_of(kt * tk, tk), tk)
        s = (_dot_nt(qq, ks_ref[sl, :]) * scale).reshape(hh, qb, tk)
        expand = jnp.where(rowi == kt * (tk // NSA_SEL_LEN) + (coli >> NSA_SEL_SHIFT), 1.0, 0.0)
        picked = _dot(selm, expand.astype(BF16))
        mask = jnp.where(kt * tk + colk <= tq, picked, 0.0) > 0.5
        l, pv, _ = _attn_tile(s, mask, vs_ref[sl, :].astype(BF16))
        return l, pv

    def sel_body(i, carry):
        l0, pv0 = sel_tile(2 * i)
        l1, pv1 = sel_tile(2 * i + 1)
        return carry[0] + (l0 + l1), carry[1] + (pv0 + pv1)

    zero = (jnp.zeros((hh, qb, 1), F32), jnp.zeros((hh, qb, dh), F32))
    trips = (n * qb + qb + 2 * tk - 1) // (2 * tk)
    o_sel = _attn_finish(*lax.fori_loop(0, trips, sel_body, zero))

    wk = NSA_WINDOW + qb
    start = jnp.maximum(n * qb - NSA_WINDOW, 0)
    sl = pl.ds(pl.multiple_of(start, qb), wk)
    s = (_dot_nt(qq, kw_ref[sl, :]) * scale).reshape(hh, qb, wk)
    dist = (n * qb + lax.broadcasted_iota(jnp.int32, (qb, wk), 0)
            - (start + lax.broadcasted_iota(jnp.int32, (qb, wk), 1)))
    mask = jnp.where(dist >= 0, jnp.where(dist < NSA_WINDOW, 1.0, 0.0), 0.0) > 0.5
    l_w, acc_w, _ = _attn_tile(s, mask, vw_ref[sl, :].astype(BF16))
    o_win = _attn_finish(l_w, acc_w)

    gts = jax.nn.sigmoid(gate_ref[...])
    for h in range(hh):
        o = (gts[:, 3 * h:3 * h + 1] * o_cmp[h] + gts[:, 3 * h + 1:3 * h + 2] * o_sel[h]
             + gts[:, 3 * h + 2:3 * h + 3] * o_win[h])
        o_ref[:, h * dh:(h + 1) * dh] = o.astype(BF16)


def _nsa_overlap_table(seq):
    n_cmp = (seq - NSA_CMP_LEN) // NSA_CMP_STRIDE + 1
    n_sel = seq // NSA_SEL_LEN
    cmp_start = np.arange(n_cmp) * NSA_CMP_STRIDE
    sel_start = np.arange(n_sel) * NSA_SEL_LEN
    overlap = ((cmp_start[None, :] < sel_start[:, None] + NSA_SEL_LEN)
               & (cmp_start[None, :] + NSA_CMP_LEN > sel_start[:, None])).astype(np.float32)
    tab = np.zeros((n_cmp + 1, LANE), np.float32)
    tab[:n_cmp, :n_sel] = overlap.T
    return jnp.asarray(tab)


def _nsa_attn(proj, cos128, sin128, kcmp, vcmp, ksr, kwr, bsz, seq):
    nb = seq // Q_BLOCK
    qspec = pl.BlockSpec((Q_BLOCK, D_GROUP), lambda b, n: (b * nb + n, CB_NQ // 4))
    qcol = lambda cb: pl.BlockSpec((Q_BLOCK, LANE), lambda b, n, cb=cb: (b * nb + n, cb))
    qtab = pl.BlockSpec((Q_BLOCK, LANE), lambda b, n: (b * nb + n, 0))
    cmp_spec = pl.BlockSpec((1, kcmp.shape[1], NSA_DH), lambda b, n: (b, 0, 0))
    kseq = pl.BlockSpec((seq, LANE), lambda b, n: (b, 0))
    vcol = lambda cb: pl.BlockSpec((seq, LANE), lambda b, n, cb=cb: (b, cb))
    ov = _nsa_overlap_table(seq)
    return pl.pallas_call(
        _nsa_kernel,
        grid=(bsz, nb),
        in_specs=[qspec, qcol(CB_GATE), qtab, qtab, cmp_spec, cmp_spec,
                  kseq, vcol(CB_NVS), kseq, vcol(CB_NVW),
                  pl.BlockSpec(ov.shape, lambda b, n: (0, 0))],
        out_specs=pl.BlockSpec((Q_BLOCK, D_GROUP), lambda b, n: (b * nb + n, 0)),
        out_shape=jax.ShapeDtypeStruct((bsz * seq, D_GROUP), BF16),
        compiler_params=_cparams(("arbitrary", "arbitrary")),
        name="nsa_attn",
    )(proj, proj, cos128, sin128, kcmp, vcmp, ksr, proj, kwr, proj, ov)


def _diff_prep_kernel(q_ref, k_ref, cos_ref, sin_ref, qt_ref, ko_ref):
    cos = cos_ref[...]
    sin = sin_ref[...]
    lane = lax.broadcasted_iota(jnp.int32, cos.shape, 1)
    lo = lane < DIFF_DH
    first = (lane & (DIFF_DH - 1)) < DIFF_DH // 2

    def prep(x):
        x2 = x * x
        ss_lo = jnp.sum(jnp.where(lo, x2, 0.0), axis=-1, keepdims=True)
        ss_hi = jnp.sum(jnp.where(lo, 0.0, x2), axis=-1, keepdims=True)
        ms = jnp.where(lo, ss_lo, ss_hi) * (1.0 / DIFF_DH)
        xn = x * lax.rsqrt(ms + EPS)
        partner = jnp.where(first, pltpu.roll(xn, LANE - DIFF_DH // 2, axis=1),
                            pltpu.roll(xn, DIFF_DH // 2, axis=1))
        return xn * cos + partner * sin

    for h in range(DIFF_HEADS):
        sl = slice(h * LANE, (h + 1) * LANE)
        qt_ref[sl, :] = (prep(q_ref[:, sl]) * (DIFF_DH ** -0.5)).T.astype(BF16)
        ko_ref[:, sl] = prep(k_ref[:, sl]).astype(BF16)


def _diff_prep(proj, cos64, sin64, tm):
    t = proj.shape[0]
    blk = lambda cb: pl.BlockSpec((tm, D_GROUP), lambda i, cb=cb: (i, cb // 4))
    tab = pl.BlockSpec((tm, LANE), lambda i: (i, 0))
    return pl.pallas_call(
        _diff_prep_kernel,
        grid=(t // tm,),
        in_specs=[blk(CB_DQ), blk(CB_DK), tab, tab],
        out_specs=[pl.BlockSpec((D_GROUP, tm), lambda i: (0, i)),
                   pl.BlockSpec((tm, D_GROUP), lambda i: (i, 0))],
        out_shape=[jax.ShapeDtypeStruct((D_GROUP, t), BF16), jax.ShapeDtypeStruct((t, D_GROUP), BF16)],
        compiler_params=_cparams(("arbitrary",)),
        name="diff_prep",
    )(proj, proj, cos64, sin64)


def _diff_kernel(lam_ref, qt_ref, k_ref, v_ref, o_ref, acc_ref, l_ref, *, out_scale):
    nq = DIFF_TQ
    qi = pl.program_id(2)
    qt = qt_ref[...]
    row = lax.broadcasted_iota(jnp.int32, qt.shape, 0)
    zero = jnp.zeros_like(qt)
    qq = jnp.concatenate([jnp.where(row < DIFF_DH, qt, zero), jnp.where(row < DIFF_DH, zero, qt)], axis=1)
    acc_ref[...] = jnp.zeros_like(acc_ref)
    l_ref[...] = jnp.zeros_like(l_ref)

    def tile(kt, masked):
        sl = pl.ds(pl.multiple_of(kt * nq, nq), nq)
        p = jnp.exp(_dot(k_ref[sl, :], qq))
        if masked:
            kpos = lax.broadcasted_iota(jnp.int32, p.shape, 0)
            qpos = lax.broadcasted_iota(jnp.int32, p.shape, 1) & (nq - 1)
            p = jnp.where(kpos <= qpos, p, 0.0)
        return jnp.sum(p, axis=0, keepdims=True), _dot_tn(v_ref[sl, :].astype(BF16), p.astype(BF16))

    def body(i, c):
        l0, a0 = tile(2 * i, False)
        l1, a1 = tile(2 * i + 1, False)
        l_ref[...] += l0 + l1
        acc_ref[...] += a0 + a1
        return c

    lax.fori_loop(0, qi >> 1, body, 0)

    @pl.when((qi & 1) == 1)
    def _():
        l0, a0 = tile(qi - 1, False)
        l_ref[...] += l0
        acc_ref[...] += a0

    l0, a0 = tile(qi, True)
    l_ref[...] += l0
    acc_ref[...] += a0
    l = l_ref[...]
    acc = acc_ref[...]
    o = acc[:, :nq] / l[:, :nq] - lam_ref[0, 0] * (acc[:, nq:] / l[:, nq:])
    on = o * lax.rsqrt(jnp.mean(o * o, axis=0, keepdims=True) + EPS) * out_scale
    o_ref[...] = on.T.astype(BF16)


def _diff_attn(lam, qt, kd, proj, bsz, seq, lam_init):
    nq = DIFF_TQ
    nb = seq // nq
    h = DIFF_HEADS
    return pl.pallas_call(
        functools.partial(_diff_kernel, out_scale=1.0 - lam_init),
        grid=(bsz, h, nb),
        in_specs=[pl.BlockSpec(memory_space=pltpu.SMEM),
                  pl.BlockSpec((LANE, nq), lambda b, hh, n: (hh, b * nb + n)),
                  pl.BlockSpec((seq, LANE), lambda b, hh, n: (b, hh)),
                  pl.BlockSpec((seq, LANE), lambda b, hh, n: (b, CB_DV + hh))],
        out_specs=pl.BlockSpec((nq, LANE), lambda b, hh, n: (b * nb + n, hh)),
        out_shape=jax.ShapeDtypeStruct((bsz * seq, D_GROUP), BF16),
        scratch_shapes=[pltpu.VMEM((LANE, 2 * nq), F32), pltpu.VMEM((1, 2 * nq), F32)],
        compiler_params=_cparams(("arbitrary", "arbitrary", "arbitrary")),
        name="diff_attn",
    )(lam, qt, kd, proj)


def _outproj_kernel(ya_ref, yb_ref, yc_ref, yd_ref, w_ref, x_ref, g_ref, xo_ref, h_ref):
    mixed = jnp.concatenate([ya_ref[...], yb_ref[...], yc_ref[...], yd_ref[...]], axis=1)
    x = x_ref[...] + _dot(mixed, w_ref[0])
    xo_ref[...] = x
    h_ref[...] = (_rms_rows(x) * g_ref[...]).astype(BF16)


def _outproj(ys, w3_bf, layer, x2, g, tm):
    t, d = x2.shape
    yspec = pl.BlockSpec((tm, D_GROUP), lambda i: (i, 0))
    row = pl.BlockSpec((tm, d), lambda i: (i, 0))
    return pl.pallas_call(
        _outproj_kernel,
        grid=(t // tm,),
        in_specs=[yspec, yspec, yspec, yspec,
                  pl.BlockSpec((1, d, d), lambda i: (layer, 0, 0)), row,
                  pl.BlockSpec((1, d), lambda i: (0, 0))],
        out_specs=[row, row],
        out_shape=[jax.ShapeDtypeStruct((t, d), F32), jax.ShapeDtypeStruct((t, d), BF16)],
        compiler_params=_cparams(("arbitrary",)),
        name="outproj",
    )(*ys, w3_bf, x2, g)


def _mlp_kernel(h_ref, x_ref, w1_ref, w2_ref, o_ref):
    @pl.when(pl.program_id(1) == 0)
    def _():
        o_ref[...] = x_ref[...]

    a = _dot(h_ref[...], w1_ref[0].astype(BF16))
    a = jnp.square(jnp.maximum(a, 0.0)).astype(BF16)
    o_ref[...] += _dot(a, w2_ref[0].astype(BF16))


def _mlp(h2, x2, w1, w2, layer, tm, tf):
    t, d = x2.shape
    f = w1.shape[2]
    return pl.pallas_call(
        _mlp_kernel,
        grid=(t // tm, f // tf),
        in_specs=[pl.BlockSpec((tm, d), lambda i, j: (i, 0)),
                  pl.BlockSpec((tm, d), lambda i, j: (i, 0), pipeline_mode=pl.Buffered(1)),
                  pl.BlockSpec((1, d, tf), lambda i, j: (layer, 0, j)),
                  pl.BlockSpec((1, tf, d), lambda i, j: (layer, j, 0))],
        out_specs=pl.BlockSpec((tm, d), lambda i, j: (i, 0)),
        out_shape=jax.ShapeDtypeStruct((t, d), F32),
        compiler_params=_cparams(("arbitrary", "arbitrary")),
        name="mlp",
    )(h2, x2, w1, w2)


def _rope_tables(positions):
    bsz, seq = positions.shape
    pos = positions.astype(F32)[..., None]

    def tab(d):
        inv = ROPE_THETA ** (-jnp.arange(0, d, 2, dtype=F32) / d)
        ang = pos * inv
        return jnp.cos(ang), jnp.sin(ang)

    c, s = tab(NSA_DH)
    cos128 = jnp.concatenate([c, c], axis=-1).reshape(bsz * seq, LANE)
    sin128 = jnp.concatenate([-s, s], axis=-1).reshape(bsz * seq, LANE)
    c, s = tab(DIFF_DH)
    cos64 = jnp.concatenate([c, c, c, c], axis=-1).reshape(bsz * seq, LANE)
    sin64 = jnp.concatenate([-s, s, -s, s], axis=-1).reshape(bsz * seq, LANE)
    return cos128, sin128, cos64, sin64


def kernel(x, positions, norm1_g, w_in, s5_lambda_re, s5_lambda_im, s5_log_dt, s5_b_re, s5_b_im, s5_c_re, s5_c_im, s5_d, s5_w_glu, nsa_pe_k, nsa_pe_v, nsa_w_cmp_k, nsa_w_cmp_v, diff_lq1, diff_lk1, diff_lq2, diff_lk2, w_out, norm2_g, mlp_w1, mlp_w2):
    bsz, seq, d = x.shape
    t = bsz * seq
    depth = w_in.shape[0]
    tm = min(1024, t)
    cos128, sin128, cos64, sin64 = _rope_tables(positions)
    ret_tabs = _ret_tables()
    x2 = x.reshape(t, d).astype(F32)
    w_in = w_in.astype(F32)
    w_out_bf = w_out.astype(BF16)
    w_glu_bf = s5_w_glu.astype(BF16)
    wk_bf = nsa_w_cmp_k.astype(BF16)
    wv_bf = nsa_w_cmp_v.astype(BF16)
    s5_tabs = jax.vmap(functools.partial(_s5_tables, n_chunk=seq // S5_CHUNK))(
        s5_lambda_re, s5_lambda_im, s5_log_dt, s5_b_re, s5_b_im, s5_c_re, s5_c_im, s5_d)
    for layer in range(depth):
        g1 = norm1_g[layer].astype(F32)[None, :]
        proj = _inproj(x2, g1, w_in, w_in[layer:layer + 1, :, GATE_RAW_END:], layer, tm, IN_TN)
        proj_d = proj

        y_a = _s5_mixer(proj, bsz, seq, tuple(a[layer] for a in s5_tabs), w_glu_bf[layer])

        y_b = _retention(proj, cos128, sin128, ret_tabs, bsz, seq)

        kcmp, vcmp, ksr, kwr = _nsa_prep(proj, cos128, sin128, nsa_pe_k[layer], nsa_pe_v[layer],
                                         wk_bf[layer], wv_bf[layer], bsz, seq)
        y_c = _nsa_attn(proj, cos128, sin128, kcmp, vcmp, ksr, kwr, bsz, seq)

        lam_init = 0.8 - 0.6 * math.exp(-0.3 * layer)
        lam = (jnp.exp(jnp.sum(diff_lq1[layer].astype(F32) * diff_lk1[layer].astype(F32)))
               - jnp.exp(jnp.sum(diff_lq2[layer].astype(F32) * diff_lk2[layer].astype(F32))) + lam_init)
        qt, kd = _diff_prep(proj_d, cos64, sin64, tm)
        y_d = _diff_attn(lam.reshape(1, 1).astype(F32), qt, kd, proj_d, bsz, seq, lam_init)

        x2, h2 = _outproj((y_a, y_b, y_c, y_d), w_out_bf, layer, x2,
                          norm2_g[layer].astype(F32)[None, :], min(512, t))
        x2 = _mlp(h2, x2, mlp_w1.astype(F32), mlp_w2.astype(F32), layer, tm, 512)
    return x2.reshape(bsz, seq, d).astype(x.dtype)
```

```python
import functools
import math

import jax
import jax.numpy as jnp
import numpy as np
from jax import lax
from jax.experimental import pallas as pl
from jax.experimental.pallas import tpu as pltpu

F32 = jnp.float32
BF16 = jnp.bfloat16
HIGHEST = lax.Precision.HIGHEST

D_MODEL = 2048
D_GROUP = 512
D_FF = 4 * D_MODEL
ROPE_THETA = 10000.0
EPS = 1e-6
Q_BLOCK = 128
NEG_INF = -1e30

S5_GROUP = 16
S5_G = 32
S5_P = 64
S5_CHUNK = 16

RET_HEADS = 4
RET_DK = 128
RET_CHUNK = 128

NSA_HEADS = 4
NSA_DH = 128
NSA_CMP_LEN = 32
NSA_CMP_STRIDE = 16
NSA_SEL_LEN = 64
NSA_TOPK = 8
NSA_WINDOW = 256
NSA_FORCE_BONUS = 1e4
NSA_SEL_SHIFT = 6
NSA_SEL_TK = 256

DIFF_HEADS = 4
DIFF_DH = 64
DIFF_TQ = 256

LANE = 128
GATE_RAW_END = 3852
IN_TN = 512
MAIN_TILES = 8
CB_U, CB_RQ, CB_RK, CB_RV, CB_RG, CB_NQ = 0, 4, 8, 12, 16, 20
CB_NKC, CB_NVC, CB_NKS, CB_NVS, CB_NKW, CB_NVW, CB_GATE = 24, 25, 26, 27, 28, 29, 30
CB_DQ, CB_DK, CB_DV = 32, 36, 40

VMEM_LIMIT = 56 * 1024 * 1024


def _cparams(sem):
    return pltpu.CompilerParams(dimension_semantics=sem, vmem_limit_bytes=VMEM_LIMIT)


def _dot(a, b, **kw):
    return jnp.dot(a, b, preferred_element_type=F32, **kw)


def _dot_nt(a, b):
    return lax.dot_general(a, b, (((1,), (1,)), ((), ())), preferred_element_type=F32)


def _dot_tn(a, b):
    return lax.dot_general(a, b, (((0,), (0,)), ((), ())), preferred_element_type=F32)


def _rms_rows(x):
    return x * lax.rsqrt(jnp.mean(x * x, axis=-1, keepdims=True) + EPS)


def _rope128(x, cos, sin_signed):
    return x * cos + pltpu.roll(x, 64, axis=1) * sin_signed


def _inproj_kernel(x_ref, g_ref, wm_ref, wd_ref, o_ref, xn_ref):
    j = pl.program_id(1)

    @pl.when(j == 0)
    def _():
        x = x_ref[...]
        xn_ref[...] = (_rms_rows(x) * g_ref[...]).astype(BF16)

    @pl.when(j < MAIN_TILES)
    def _():
        o_ref[...] = _dot(xn_ref[...], wm_ref[0].astype(BF16))

    @pl.when(j >= MAIN_TILES)
    def _():
        o_ref[...] = _dot(xn_ref[...], wd_ref[0].astype(BF16))


def _inproj(x2, g, w3, w_diff, layer, tm, tn):
    t, d = x2.shape
    n_tiles = MAIN_TILES + w_diff.shape[2] // tn
    return pl.pallas_call(
        _inproj_kernel,
        grid=(t // tm, n_tiles),
        in_specs=[pl.BlockSpec((tm, d), lambda i, j: (i, 0)),
                  pl.BlockSpec((1, d), lambda i, j: (0, 0)),
                  pl.BlockSpec((1, d, tn), lambda i, j: (layer, 0, jnp.minimum(j, MAIN_TILES - 1))),
                  pl.BlockSpec((1, d, tn), lambda i, j: (0, 0, jnp.maximum(j - MAIN_TILES, 0)))],
        out_specs=pl.BlockSpec((tm, tn), lambda i, j: (i, j)),
        out_shape=jax.ShapeDtypeStruct((t, n_tiles * tn), F32),
        scratch_shapes=[pltpu.VMEM((tm, d), BF16)],
        compiler_params=_cparams(("arbitrary", "arbitrary")),
        name="inproj",
    )(x2, g, w3, w_diff)


def _s5_tab_kernel(lre_ref, lim_ref, ldt_ref, b_re_ref, b_im_ref, c_re_ref, c_im_ref,
                   wi_ref, ws_ref, wo_ref, ap_ref):
    tc, p, gs = S5_CHUNK, S5_P, S5_GROUP
    w = tc * gs
    lam_re = lre_ref[0, 0]
    lam_im = lim_ref[0, 0]
    dt = jnp.exp(ldt_ref[0, 0])
    lre = lam_re * dt
    lim = lam_im * dt

    def lbpow(e):
        mag = jnp.exp(lre * e)
        return mag * jnp.cos(lim * e), mag * jnp.sin(lim * e)

    pr, pi = lbpow(lax.broadcasted_iota(jnp.int32, (24, 2 * p), 0).astype(F32))
    lb_re, lb_im = pr[1:2], pi[1:2]
    den = lam_re * lam_re + lam_im * lam_im
    f_re = ((lb_re - 1.0) * lam_re + lb_im * lam_im) / den
    f_im = (lb_im * lam_re - (lb_re - 1.0) * lam_im) / den

    r8 = lax.broadcasted_iota(jnp.int32, (8, 2 * p), 0)
    f8 = jnp.where(r8 == 0, f_re, jnp.where(r8 == 1, f_im, 0.0))
    cols = jnp.concatenate([pr, pi, f8, jnp.zeros((72, 2 * p), F32)], axis=0).T
    pr_c, pi_c = cols[:p, 0:tc], cols[:p, 24:24 + tc]
    f_re_c, f_im_c = cols[:p, 48:49], cols[:p, 49:50]

    b_re = b_re_ref[0, 0]
    b_im = b_im_ref[0, 0]
    bb_re = f_re_c * b_re - f_im_c * b_im
    bb_im = f_re_c * b_im + f_im_c * b_re
    hrow = lax.broadcasted_iota(jnp.int32, (gs, w), 0)
    lane_w = lax.broadcasted_iota(jnp.int32, (gs, w), 1)
    tile_m = jnp.where((lane_w & (gs - 1)) == hrow, 1.0, 0.0)
    rev_m = jnp.where(hrow == (tc - 1) - (lane_w >> 4), 1.0, 0.0)
    bbt_re = _dot(bb_re, tile_m, precision=HIGHEST)
    bbt_im = _dot(bb_im, tile_m, precision=HIGHEST)
    p15_re = _dot(pr_c, rev_m, precision=HIGHEST)
    p15_im = _dot(pi_c, rev_m, precision=HIGHEST)
    ws_ref[0, 0] = jnp.concatenate([p15_re * bbt_re - p15_im * bbt_im,
                                    p15_re * bbt_im + p15_im * bbt_re], axis=0).astype(BF16)

    c_re = c_re_ref[0, 0]
    c_im = c_im_ref[0, 0]
    lo = lax.broadcasted_iota(jnp.int32, (1, 2 * p), 1) < p
    row_a = jnp.where(lo, pr, -pi)
    row_b = jnp.where(lo, -pi, -pr)
    wo_ref[0, 0] = jnp.concatenate([c_re * row_a[t + 1:t + 2] + c_im * row_b[t + 1:t + 2]
                                    for t in range(tc)], axis=0).astype(BF16)

    x_re = jnp.concatenate([c_re * pr[k:k + 1] - c_im * pi[k:k + 1] for k in range(tc)], axis=0)
    x_im = jnp.concatenate([c_re * pi[k:k + 1] + c_im * pr[k:k + 1] for k in range(tc)], axis=0)
    zpad = jnp.zeros((p, w), F32)
    kt = (_dot(x_re, jnp.concatenate([bbt_re, zpad], axis=0), precision=HIGHEST)
          - _dot(x_im, jnp.concatenate([bbt_im, zpad], axis=0), precision=HIGHEST))
    sblk = lane_w >> 4
    rows = []
    for t in range(tc):
        acc = jnp.zeros((gs, w), F32)
        for k in range(t + 1):
            acc = jnp.where(sblk == t - k, kt[k * gs:(k + 1) * gs], acc)
        rows.append(acc)
    wi_ref[0, 0] = jnp.concatenate(rows, axis=0).astype(BF16)

    r16 = lax.broadcasted_iota(jnp.int32, (16, 2 * p), 0)
    sr, si = lbpow(lax.shift_left(jnp.full((16, 2 * p), tc, jnp.int32), r16 >> 1).astype(F32))
    ap_ref[0, 0] = jnp.where((r16 & 1) == 1, jnp.where(lo, -si, si), sr)


def _s5_tables(lam_re, lam_im, log_dt, b_re, b_im, c_re, c_im, d_skip):
    depth, g, p = lam_re.shape
    w = S5_CHUNK * S5_GROUP
    dup = lambda a: jnp.concatenate([a.astype(F32), a.astype(F32)], axis=-1)
    ldt = jnp.broadcast_to(log_dt.astype(F32)[:, :, None, None], (depth, g, 1, 2 * p))
    row = pl.BlockSpec((1, 1, 1, 2 * p), lambda l, i: (l, i, 0, 0))
    bspec = pl.BlockSpec((1, 1, p, S5_GROUP), lambda l, i: (l, i, 0, 0))
    cspec = pl.BlockSpec((1, 1, S5_GROUP, 2 * p), lambda l, i: (l, i, 0, 0))
    out = lambda r, c: pl.BlockSpec((1, 1, r, c), lambda l, i: (l, i, 0, 0))
    tabs = pl.pallas_call(
        _s5_tab_kernel,
        grid=(depth, g),
        in_specs=[row, row, row, bspec, bspec, cspec, cspec],
        out_specs=[out(w, w), out(2 * p, w), out(w, 2 * p), out(16, 2 * p)],
        out_shape=[jax.ShapeDtypeStruct((depth, g, w, w), BF16),
                   jax.ShapeDtypeStruct((depth, g, 2 * p, w), BF16),
                   jax.ShapeDtypeStruct((depth, g, w, 2 * p), BF16),
                   jax.ShapeDtypeStruct((depth, g, 16, 2 * p), F32)],
        compiler_params=_cparams(("arbitrary", "arbitrary")),
        name="s5_tables",
    )(dup(lam_re)[:, :, None, :], dup(lam_im)[:, :, None, :], ldt, b_re.astype(F32), b_im.astype(F32),
      dup(c_re), dup(c_im))
    d_row = d_skip.astype(F32).reshape(depth, 1, g * S5_GROUP)
    return (*tabs, d_row)


def _s5_pack_kernel(*refs):
    at_ref = refs[-1]
    nc = at_ref.shape[2]
    for t in range(S5_CHUNK):
        for j, u_ref in enumerate(refs[:-1]):
            at_ref[t, j * LANE:(j + 1) * LANE, :] = u_ref[pl.ds(t, nc, stride=S5_CHUNK), :].T.astype(BF16)


def _u_specs(seq):
    return [pl.BlockSpec((seq, LANE), lambda b, j=j: (b, CB_U + j)) for j in range(D_GROUP // LANE)]


def _s5_pack(proj, bsz, seq):
    nc = seq // S5_CHUNK
    return pl.pallas_call(
        _s5_pack_kernel,
        grid=(bsz,),
        in_specs=_u_specs(seq),
        out_specs=pl.BlockSpec((S5_CHUNK, D_GROUP, nc), lambda b: (0, 0, b)),
        out_shape=jax.ShapeDtypeStruct((S5_CHUNK, D_GROUP, bsz * nc), BF16),
        compiler_params=_cparams(("arbitrary",)),
        name="s5_pack",
    )(*([proj] * (D_GROUP // LANE)))


def _s5_kernel(ut_ref, wi_ref, ws_ref, wo_ref, ap_ref, yt_ref, *, n_chunk):
    tc, gs, r = ut_ref.shape
    ut = ut_ref[...].reshape(tc * gs, r)
    s = _dot(ws_ref[0], ut)
    lane = lax.broadcasted_iota(jnp.int32, s.shape, 1) & (n_chunk - 1)
    apt = ap_ref[0].T
    x = s
    n_step = int(math.log2(n_chunk))
    for k in range(n_step):
        sh = 1 << k
        xs = jnp.where(lane >= sh, pltpu.roll(x, sh, axis=1), 0.0)
        xsw = jnp.concatenate([xs[S5_P:], xs[:S5_P]], axis=0)
        x = x + xs * apt[:, 2 * k:2 * k + 1] + xsw * apt[:, 2 * k + 1:2 * k + 2]
    xp = jnp.where(lane >= 1, pltpu.roll(x, 1, axis=1), 0.0)
    y = _dot(wi_ref[0], ut) + _dot(wo_ref[0], xp.astype(BF16))
    yt_ref[...] = y.reshape(tc, gs, r)


def _s5_core(at, tabs, n_chunk):
    tc, ch, r = at.shape
    w_intra_t, w_state_t, w_out_t, apow, _ = tabs
    g = w_intra_t.shape[0]
    w = tc * S5_GROUP
    return pl.pallas_call(
        functools.partial(_s5_kernel, n_chunk=n_chunk),
        grid=(g,),
        in_specs=[pl.BlockSpec((tc, S5_GROUP, r), lambda i: (0, i, 0)),
                  pl.BlockSpec((1, w, w), lambda i: (i, 0, 0)),
                  pl.BlockSpec((1, 2 * S5_P, w), lambda i: (i, 0, 0)),
                  pl.BlockSpec((1, w, 2 * S5_P), lambda i: (i, 0, 0)),
                  pl.BlockSpec((1, 16, 2 * S5_P), lambda i: (i, 0, 0))],
        out_specs=pl.BlockSpec((tc, S5_GROUP, r), lambda i: (0, i, 0)),
        out_shape=jax.ShapeDtypeStruct((tc, ch, r), F32),
        compiler_params=_cparams(("arbitrary",)),
        name="s5_core",
    )(at, w_intra_t, w_state_t, w_out_t, apow)


def _s5_out_kernel(yt_ref, u0_ref, u1_ref, u2_ref, u3_ref, d_ref, wg_ref, o_ref, y_scr):
    nc = yt_ref.shape[2]
    for t in range(S5_CHUNK):
        rows = pl.ds(t, nc, stride=S5_CHUNK)
        for j, u_ref in enumerate((u0_ref, u1_ref, u2_ref, u3_ref)):
            sl = slice(j * LANE, (j + 1) * LANE)
            y_scr[j, rows, :] = jax.nn.gelu(yt_ref[t, sl, :].T + d_ref[:, sl] * u_ref[rows, :])
    y = jnp.concatenate([y_scr[j] for j in range(D_GROUP // LANE)], axis=1)
    o_ref[...] = (y * jax.nn.sigmoid(_dot(y.astype(BF16), wg_ref[...]))).astype(BF16)


def _s5_out(yt, proj, d_row, w_glu_bf, bsz, seq):
    nc = seq // S5_CHUNK
    return pl.pallas_call(
        _s5_out_kernel,
        grid=(bsz,),
        in_specs=[pl.BlockSpec((S5_CHUNK, D_GROUP, nc), lambda b: (0, 0, b))] + _u_specs(seq)
        + [pl.BlockSpec((1, D_GROUP), lambda b: (0, 0)),
           pl.BlockSpec((D_GROUP, D_GROUP), lambda b: (0, 0))],
        out_specs=pl.BlockSpec((seq, D_GROUP), lambda b: (b, 0)),
        out_shape=jax.ShapeDtypeStruct((bsz * seq, D_GROUP), BF16),
        scratch_shapes=[pltpu.VMEM((D_GROUP // LANE, seq, LANE), F32)],
        compiler_params=_cparams(("arbitrary",)),
        name="s5_out",
    )(yt, proj, proj, proj, proj, d_row, w_glu_bf)


def _s5_mixer(proj, bsz, seq, tabs, w_glu_bf):
    yt = _s5_core(_s5_pack(proj, bsz, seq), tabs, seq // S5_CHUNK)
    return _s5_out(yt, proj, tabs[4], w_glu_bf, bsz, seq)


def _ret_tables():
    h, c = RET_HEADS, RET_CHUNK
    log_g = jnp.log(1.0 - 2.0 ** (-5.0 - jnp.arange(h, dtype=F32)))
    idx = jnp.arange(c, dtype=F32)
    diff = idx[:, None] - idx[None, :]
    dmat = jnp.where(diff >= 0, jnp.exp(jnp.maximum(diff, 0.0)[None] * log_g[:, None, None]), 0.0)
    zeta = jnp.exp((c - 1.0 - idx)[None, :] * log_g[:, None])
    xi = jnp.exp((idx + 1.0)[None, :] * log_g[:, None])
    g_chunk = jnp.exp(c * log_g)
    zeta_b = jnp.broadcast_to(zeta[:, :, None], (h, c, RET_DK))
    xi_b = jnp.broadcast_to(xi[:, :, None], (h, c, RET_DK))
    gch_b = jnp.broadcast_to(g_chunk[:, None, None], (h, 8, RET_DK))
    return dmat, zeta_b, xi_b, gch_b


def _ret_kernel(q_ref, k_ref, v_ref, g_ref, cos_ref, sin_ref, dm_ref, ze_ref, xi_ref, gc_ref,
                o_ref, *, n_chunk):
    c = RET_CHUNK
    dm = dm_ref[0]
    ze = ze_ref[0]
    xi = xi_ref[0]
    gch = gc_ref[0, 0:1, :]

    def body(n, r):
        sl = pl.ds(pl.multiple_of(n * c, c), c)
        cos = cos_ref[sl, :]
        sin = sin_ref[sl, :]
        qc = _rope128(q_ref[sl, :], cos, sin)
        kc = _rope128(k_ref[sl, :], cos, sin) * (RET_DK ** -0.5)
        vc = v_ref[sl, :].astype(BF16)
        inner = _dot_nt(qc.astype(BF16), kc.astype(BF16)) * dm
        o = _dot(inner.astype(BF16), vc) + _dot((qc * xi).astype(BF16), r.astype(BF16))
        kv = _dot_tn((kc * ze).astype(BF16), vc)
        r = r * gch + kv
        mu = jnp.mean(o, axis=-1, keepdims=True)
        oc = o - mu
        var = jnp.mean(oc * oc, axis=-1, keepdims=True)
        on = oc * lax.rsqrt(var + EPS)
        gg = g_ref[sl, :]
        o_ref[sl, :] = (gg * jax.nn.sigmoid(gg) * on).astype(BF16)
        return r

    lax.fori_loop(0, n_chunk, body, jnp.zeros((RET_DK, RET_DK), F32), unroll=4)


def _retention(proj, cos128, sin128, tabs, bsz, seq):
    dmat, zeta_b, xi_b, gch_b = tabs
    h = RET_HEADS
    col = lambda cb: pl.BlockSpec((seq, LANE), lambda b, hh, cb=cb: (b, cb + hh))
    tab = pl.BlockSpec((seq, LANE), lambda b, hh: (b, 0))
    head = lambda r: pl.BlockSpec((1, r, LANE), lambda b, hh: (hh, 0, 0))
    return pl.pallas_call(
        functools.partial(_ret_kernel, n_chunk=seq // RET_CHUNK),
        grid=(bsz, h),
        in_specs=[col(CB_RQ), col(CB_RK), col(CB_RV), col(CB_RG), tab, tab,
                  head(RET_CHUNK), head(RET_CHUNK), head(RET_CHUNK), head(8)],
        out_specs=pl.BlockSpec((seq, LANE), lambda b, hh: (b, hh)),
        out_shape=jax.ShapeDtypeStruct((bsz * seq, D_GROUP), BF16),
        compiler_params=_cparams(("arbitrary", "arbitrary")),
        name="retention",
    )(proj, proj, proj, proj, cos128, sin128, dmat, zeta_b, xi_b, gch_b)


def _nsa_prep_kernel(kc_ref, vc_ref, pek_ref, pev_ref, wk_ref, wv_ref, ks_ref, kw_ref, cos_ref, sin_ref,
                     kcmp_ref, vcmp_ref, ksr_ref, kwr_ref):
    half = NSA_CMP_STRIDE * NSA_DH
    n16 = kcmp_ref.shape[1]
    kparts, vparts = [], []
    for i in range(NSA_CMP_STRIDE):
        rows = pl.ds(i, n16, stride=NSA_CMP_STRIDE)
        kparts.append(_rope128(kc_ref[rows, :], cos_ref[rows, :], sin_ref[rows, :]))
        vparts.append(vc_ref[rows, :])
    xr = jnp.concatenate(kparts, axis=1)
    xv = jnp.concatenate(vparts, axis=1)
    a = _dot((xr + pek_ref[0:1, :]).astype(BF16), wk_ref[0:half, :])
    b = _dot((xr + pek_ref[1:2, :]).astype(BF16), wk_ref[half:2 * half, :])
    kcm = _rms_rows(a + pltpu.roll(b, n16 - 1, axis=0))
    rowi = lax.broadcasted_iota(jnp.int32, kcm.shape, 0)
    kcmp_ref[0] = jnp.where(rowi < n16 - 1, kcm, 0.0).astype(BF16)
    av = _dot((xv + pev_ref[0:1, :]).astype(BF16), wv_ref[0:half, :])
    bv = _dot((xv + pev_ref[1:2, :]).astype(BF16), wv_ref[half:2 * half, :])
    vcm = av + pltpu.roll(bv, n16 - 1, axis=0)
    vcmp_ref[0] = jnp.where(rowi < n16 - 1, vcm, 0.0).astype(BF16)
    cos = cos_ref[...]
    sin = sin_ref[...]
    ksr_ref[...] = _rope128(_rms_rows(ks_ref[...]), cos, sin).astype(BF16)
    kwr_ref[...] = _rope128(_rms_rows(kw_ref[...]), cos, sin).astype(BF16)


def _nsa_prep(proj, cos128, sin128, pe_k, pe_v, wk_bf, wv_bf, bsz, seq):
    n16 = seq // NSA_CMP_STRIDE
    half = NSA_CMP_STRIDE * NSA_DH
    pek = pe_k.astype(F32).reshape(2, half)
    pev = pe_v.astype(F32).reshape(2, half)
    full = lambda a: pl.BlockSpec(a.shape, lambda b: (0,) * a.ndim)
    col = lambda cb: pl.BlockSpec((seq, LANE), lambda b, cb=cb: (b, cb))
    tab = pl.BlockSpec((seq, LANE), lambda b: (b, 0))
    cmp_spec = pl.BlockSpec((1, n16, NSA_DH), lambda b: (b, 0, 0))
    seq_spec = pl.BlockSpec((seq, LANE), lambda b: (b, 0))
    return pl.pallas_call(
        _nsa_prep_kernel,
        grid=(bsz,),
        in_specs=[col(CB_NKC), col(CB_NVC), full(pek), full(pev), full(wk_bf), full(wv_bf),
                  col(CB_NKS), col(CB_NKW), tab, tab],
        out_specs=[cmp_spec, cmp_spec, seq_spec, seq_spec],
        out_shape=[jax.ShapeDtypeStruct((bsz, n16, NSA_DH), BF16),
                   jax.ShapeDtypeStruct((bsz, n16, NSA_DH), BF16),
                   jax.ShapeDtypeStruct((bsz * seq, NSA_DH), BF16),
                   jax.ShapeDtypeStruct((bsz * seq, NSA_DH), BF16)],
        compiler_params=_cparams(("arbitrary",)),
        name="nsa_prep",
    )(proj, proj, pek, pev, wk_bf, wv_bf, proj, proj, cos128, sin128)


def _attn_tile(s3, mask, vv):
    hh, qb, kk = s3.shape
    e = jnp.where(mask[None], jnp.exp(s3), 0.0)
    pv = _dot(e.astype(BF16).reshape(hh * qb, kk), vv).reshape(hh, qb, vv.shape[-1])
    return jnp.sum(e, axis=-1, keepdims=True), pv, e


def _attn_finish(l, acc):
    return jnp.where(l > 0.0, acc / jnp.where(l > 0.0, l, 1.0), 0.0)


def _nsa_kernel(q_ref, gate_ref, cos_ref, sin_ref, kc_ref, vc_ref, ks_ref, vs_ref, kw_ref, vw_ref,
                ov_ref, o_ref):
    hh, qb, dh = NSA_HEADS, Q_BLOCK, NSA_DH
    scale = dh ** -0.5
    n = pl.program_id(1)
    cos = cos_ref[...]
    sin = sin_ref[...]
    qs = [_rope128(_rms_rows(q_ref[:, h * dh:(h + 1) * dh]), cos, sin) for h in range(hh)]
    qq = jnp.concatenate(qs, axis=0).astype(BF16)
    col = lax.broadcasted_iota(jnp.int32, (qb, LANE), 1)
    t = n * qb + lax.broadcasted_iota(jnp.int32, (qb, LANE), 0)

    ncp = kc_ref.shape[1]
    colc = lax.broadcasted_iota(jnp.int32, (qb, ncp), 1)
    tc = n * qb + lax.broadcasted_iota(jnp.int32, (qb, ncp), 0)
    s3 = (_dot_nt(qq, kc_ref[0]) * scale).reshape(hh, qb, ncp)
    mask_c = (colc * NSA_CMP_STRIDE + (NSA_CMP_LEN - 1)) <= tc
    l_c, pv_c, e_c = _attn_tile(s3, mask_c, vc_ref[0])
    o_cmp = _attn_finish(l_c, pv_c)
    p = e_c / jnp.where(l_c > 0.0, l_c, 1.0)

    imp = _dot(jnp.sum(p, axis=0), ov_ref[...], precision=HIGHEST)
    cur = t >> NSA_SEL_SHIFT
    forced = (col == 0) | (col == cur) | (col == cur - 1)
    score = jnp.where(col <= cur, imp + jnp.where(forced, NSA_FORCE_BONUS, 0.0), NEG_INF)
    n_sel = ks_ref.shape[0] // NSA_SEL_LEN
    rank = jnp.zeros((qb, LANE), F32)
    for jp in range(n_sel):
        cj = score[:, jp:jp + 1]
        tie = jnp.where(col > jp, 1.0, 0.0)
        rank = rank + jnp.where(cj > score, 1.0, jnp.where(cj == score, tie, 0.0))
    selm = jnp.where(rank < float(min(NSA_TOPK, n_sel)), 1.0, 0.0).astype(BF16)

    tk = NSA_SEL_TK
    rowi = lax.broadcasted_iota(jnp.int32, (LANE, tk), 0)
    coli = lax.broadcasted_iota(jnp.int32, (LANE, tk), 1)
    colk = lax.broadcasted_iota(jnp.int32, (qb, tk), 1)
    tq = n * qb + lax.broadcasted_iota(jnp.int32, (qb, tk), 0)

    def sel_tile(kt):
        sl = pl.ds(pl.multiple_of(kt * tk, tk), tk)
        s = (_dot_nt(qq, ks_ref[sl, :]) * scale).reshape(hh, qb, tk)
        expand = jnp.where(rowi == kt * (tk // NSA_SEL_LEN) + (coli >> NSA_SEL_SHIFT), 1.0, 0.0)
        picked = _dot(selm, expand.astype(BF16))
        mask = jnp.where(kt * tk + colk <= tq, picked, 0.0) > 0.5
        l, pv, _ = _attn_tile(s, mask, vs_ref[sl, :].astype(BF16))
        return l, pv

    def sel_body(i, carry):
        l0, pv0 = sel_tile(2 * i)
        l1, pv1 = sel_tile(2 * i + 1)
        return carry[0] + (l0 + l1), carry[1] + (pv0 + pv1)

    zero = (jnp.zeros((hh, qb, 1), F32), jnp.zeros((hh, qb, dh), F32))
    trips = (n * qb + qb + 2 * tk - 1) // (2 * tk)
    o_sel = _attn_finish(*lax.fori_loop(0, trips, sel_body, zero))

    wk = NSA_WINDOW + qb
    start = jnp.maximum(n * qb - NSA_WINDOW, 0)
    sl = pl.ds(pl.multiple_of(start, qb), wk)
    s = (_dot_nt(qq, kw_ref[sl, :]) * scale).reshape(hh, qb, wk)
    dist = (n * qb + lax.broadcasted_iota(jnp.int32, (qb, wk), 0)
            - (start + lax.broadcasted_iota(jnp.int32, (qb, wk), 1)))
    mask = jnp.where(dist >= 0, jnp.where(dist < NSA_WINDOW, 1.0, 0.0), 0.0) > 0.5
    l_w, acc_w, _ = _attn_tile(s, mask, vw_ref[sl, :].astype(BF16))
    o_win = _attn_finish(l_w, acc_w)

    gts = jax.nn.sigmoid(gate_ref[...])
    for h in range(hh):
        o = (gts[:, 3 * h:3 * h + 1] * o_cmp[h] + gts[:, 3 * h + 1:3 * h + 2] * o_sel[h]
             + gts[:, 3 * h + 2:3 * h + 3] * o_win[h])
        o_ref[:, h * dh:(h + 1) * dh] = o.astype(BF16)


def _nsa_overlap_table(seq):
    n_cmp = (seq - NSA_CMP_LEN) // NSA_CMP_STRIDE + 1
    n_sel = seq // NSA_SEL_LEN
    cmp_start = np.arange(n_cmp) * NSA_CMP_STRIDE
    sel_start = np.arange(n_sel) * NSA_SEL_LEN
    overlap = ((cmp_start[None, :] < sel_start[:, None] + NSA_SEL_LEN)
               & (cmp_start[None, :] + NSA_CMP_LEN > sel_start[:, None])).astype(np.float32)
    tab = np.zeros((n_cmp + 1, LANE), np.float32)
    tab[:n_cmp, :n_sel] = overlap.T
    return jnp.asarray(tab)


def _nsa_attn(proj, cos128, sin128, kcmp, vcmp, ksr, kwr, bsz, seq):
    nb = seq // Q_BLOCK
    qspec = pl.BlockSpec((Q_BLOCK, D_GROUP), lambda b, n: (b * nb + n, CB_NQ // 4))
    qcol = lambda cb: pl.BlockSpec((Q_BLOCK, LANE), lambda b, n, cb=cb: (b * nb + n, cb))
    qtab = pl.BlockSpec((Q_BLOCK, LANE), lambda b, n: (b * nb + n, 0))
    cmp_spec = pl.BlockSpec((1, kcmp.shape[1], NSA_DH), lambda b, n: (b, 0, 0))
    kseq = pl.BlockSpec((seq, LANE), lambda b, n: (b, 0))
    vcol = lambda cb: pl.BlockSpec((seq, LANE), lambda b, n, cb=cb: (b, cb))
    ov = _nsa_overlap_table(seq)
    return pl.pallas_call(
        _nsa_kernel,
        grid=(bsz, nb),
        in_specs=[qspec, qcol(CB_GATE), qtab, qtab, cmp_spec, cmp_spec,
                  kseq, vcol(CB_NVS), kseq, vcol(CB_NVW),
                  pl.BlockSpec(ov.shape, lambda b, n: (0, 0))],
        out_specs=pl.BlockSpec((Q_BLOCK, D_GROUP), lambda b, n: (b * nb + n, 0)),
        out_shape=jax.ShapeDtypeStruct((bsz * seq, D_GROUP), BF16),
        compiler_params=_cparams(("arbitrary", "arbitrary")),
        name="nsa_attn",
    )(proj, proj, cos128, sin128, kcmp, vcmp, ksr, proj, kwr, proj, ov)


def _diff_prep_kernel(q_ref, k_ref, cos_ref, sin_ref, qt_ref, ko_ref):
    cos = cos_ref[...]
    sin = sin_ref[...]
    lane = lax.broadcasted_iota(jnp.int32, cos.shape, 1)
    lo = lane < DIFF_DH
    first = (lane & (DIFF_DH - 1)) < DIFF_DH // 2

    def prep(x):
        x2 = x * x
        ss_lo = jnp.sum(jnp.where(lo, x2, 0.0), axis=-1, keepdims=True)
        ss_hi = jnp.sum(jnp.where(lo, 0.0, x2), axis=-1, keepdims=True)
        ms = jnp.where(lo, ss_lo, ss_hi) * (1.0 / DIFF_DH)
        xn = x * lax.rsqrt(ms + EPS)
        partner = jnp.where(first, pltpu.roll(xn, LANE - DIFF_DH // 2, axis=1),
                            pltpu.roll(xn, DIFF_DH // 2, axis=1))
        return xn * cos + partner * sin

    for h in range(DIFF_HEADS):
        sl = slice(h * LANE, (h + 1) * LANE)
        qt_ref[sl, :] = (prep(q_ref[:, sl]) * (DIFF_DH ** -0.5)).T.astype(BF16)
        ko_ref[:, sl] = prep(k_ref[:, sl]).astype(BF16)


def _diff_prep(proj, cos64, sin64, tm):
    t = proj.shape[0]
    blk = lambda cb: pl.BlockSpec((tm, D_GROUP), lambda i, cb=cb: (i, cb // 4))
    tab = pl.BlockSpec((tm, LANE), lambda i: (i, 0))
    return pl.pallas_call(
        _diff_prep_kernel,
        grid=(t // tm,),
        in_specs=[blk(CB_DQ), blk(CB_DK), tab, tab],
        out_specs=[pl.BlockSpec((D_GROUP, tm), lambda i: (0, i)),
                   pl.BlockSpec((tm, D_GROUP), lambda i: (i, 0))],
        out_shape=[jax.ShapeDtypeStruct((D_GROUP, t), BF16), jax.ShapeDtypeStruct((t, D_GROUP), BF16)],
        compiler_params=_cparams(("arbitrary",)),
        name="diff_prep",
    )(proj, proj, cos64, sin64)


def _diff_kernel(lam_ref, qt_ref, k_ref, v_ref, o_ref, acc_ref, l_ref, *, out_scale):
    nq = DIFF_TQ
    qi = pl.program_id(2)
    qt = qt_ref[...]
    row = lax.broadcasted_iota(jnp.int32, qt.shape, 0)
    zero = jnp.zeros_like(qt)
    qq = jnp.concatenate([jnp.where(row < DIFF_DH, qt, zero), jnp.where(row < DIFF_DH, zero, qt)], axis=1)
    acc_ref[...] = jnp.zeros_like(acc_ref)
    l_ref[...] = jnp.zeros_like(l_ref)

    def tile(kt, masked):
        sl = pl.ds(pl.multiple_of(kt * nq, nq), nq)
        p = jnp.exp(_dot(k_ref[sl, :], qq))
        if masked:
            kpos = lax.broadcasted_iota(jnp.int32, p.shape, 0)
            qpos = lax.broadcasted_iota(jnp.int32, p.shape, 1) & (nq - 1)
            p = jnp.where(kpos <= qpos, p, 0.0)
        return jnp.sum(p, axis=0, keepdims=True), _dot_tn(v_ref[sl, :].astype(BF16), p.astype(BF16))

    def body(i, c):
        l0, a0 = tile(2 * i, False)
        l1, a1 = tile(2 * i + 1, False)
        l_ref[...] += l0 + l1
        acc_ref[...] += a0 + a1
        return c

    lax.fori_loop(0, qi >> 1, body, 0)

    @pl.when((qi & 1) == 1)
    def _():
        l0, a0 = tile(qi - 1, False)
        l_ref[...] += l0
        acc_ref[...] += a0

    l0, a0 = tile(qi, True)
    l_ref[...] += l0
    acc_ref[...] += a0
    l = l_ref[...]
    acc = acc_ref[...]
    o = acc[:, :nq] / l[:, :nq] - lam_ref[0, 0] * (acc[:, nq:] / l[:, nq:])
    on = o * lax.rsqrt(jnp.mean(o * o, axis=0, keepdims=True) + EPS) * out_scale
    o_ref[...] = on.T.astype(BF16)


def _diff_attn(lam, qt, kd, proj, bsz, seq, lam_init):
    nq = DIFF_TQ
    nb = seq // nq
    h = DIFF_HEADS
    return pl.pallas_call(
        functools.partial(_diff_kernel, out_scale=1.0 - lam_init),
        grid=(bsz, h, nb),
        in_specs=[pl.BlockSpec(memory_space=pltpu.SMEM),
                  pl.BlockSpec((LANE, nq), lambda b, hh, n: (hh, b * nb + n)),
                  pl.BlockSpec((seq, LANE), lambda b, hh, n: (b, hh)),
                  pl.BlockSpec((seq, LANE), lambda b, hh, n: (b, CB_DV + hh))],
        out_specs=pl.BlockSpec((nq, LANE), lambda b, hh, n: (b * nb + n, hh)),
        out_shape=jax.ShapeDtypeStruct((bsz * seq, D_GROUP), BF16),
        scratch_shapes=[pltpu.VMEM((LANE, 2 * nq), F32), pltpu.VMEM((1, 2 * nq), F32)],
        compiler_params=_cparams(("arbitrary", "arbitrary", "arbitrary")),
        name="diff_attn",
    )(lam, qt, kd, proj)


def _outproj_kernel(ya_ref, yb_ref, yc_ref, yd_ref, w_ref, x_ref, g_ref, xo_ref, h_ref):
    mixed = jnp.concatenate([ya_ref[...], yb_ref[...], yc_ref[...], yd_ref[...]], axis=1)
    x = x_ref[...] + _dot(mixed, w_ref[0])
    xo_ref[...] = x
    h_ref[...] = (_rms_rows(x) * g_ref[...]).astype(BF16)


def _outproj(ys, w3_bf, layer, x2, g, tm):
    t, d = x2.shape
    yspec = pl.BlockSpec((tm, D_GROUP), lambda i: (i, 0))
    row = pl.BlockSpec((tm, d), lambda i: (i, 0))
    return pl.pallas_call(
        _outproj_kernel,
        grid=(t // tm,),
        in_specs=[yspec, yspec, yspec, yspec,
                  pl.BlockSpec((1, d, d), lambda i: (layer, 0, 0)), row,
                  pl.BlockSpec((1, d), lambda i: (0, 0))],
        out_specs=[row, row],
        out_shape=[jax.ShapeDtypeStruct((t, d), F32), jax.ShapeDtypeStruct((t, d), BF16)],
        compiler_params=_cparams(("arbitrary",)),
        name="outproj",
    )(*ys, w3_bf, x2, g)


def _mlp_kernel(h_ref, x_ref, w1_ref, w2_ref, o_ref):
    @pl.when(pl.program_id(1) == 0)
    def _():
        o_ref[...] = x_ref[...]

    a = _dot(h_ref[...], w1_ref[0].astype(BF16))
    a = jnp.square(jnp.maximum(a, 0.0)).astype(BF16)
    o_ref[...] += _dot(a, w2_ref[0].astype(BF16))


def _mlp(h2, x2, w1, w2, layer, tm, tf):
    t, d = x2.shape
    f = w1.shape[2]
    return pl.pallas_call(
        _mlp_kernel,
        grid=(t // tm, f // tf),
        in_specs=[pl.BlockSpec((tm, d), lambda i, j: (i, 0)),
                  pl.BlockSpec((tm, d), lambda i, j: (i, 0), pipeline_mode=pl.Buffered(1)),
                  pl.BlockSpec((1, d, tf), lambda i, j: (layer, 0, j)),
                  pl.BlockSpec((1, tf, d), lambda i, j: (layer, j, 0))],
        out_specs=pl.BlockSpec((tm, d), lambda i, j: (i, 0)),
        out_shape=jax.ShapeDtypeStruct((t, d), F32),
        compiler_params=_cparams(("arbitrary", "arbitrary")),
        name="mlp",
    )(h2, x2, w1, w2)


def _rope_tables(positions):
    bsz, seq = positions.shape
    pos = positions.astype(F32)[..., None]

    def tab(d):
        inv = ROPE_THETA ** (-jnp.arange(0, d, 2, dtype=F32) / d)
        ang = pos * inv
        return jnp.cos(ang), jnp.sin(ang)

    c, s = tab(NSA_DH)
    cos128 = jnp.concatenate([c, c], axis=-1).reshape(bsz * seq, LANE)
    sin128 = jnp.concatenate([-s, s], axis=-1).reshape(bsz * seq, LANE)
    c, s = tab(DIFF_DH)
    cos64 = jnp.concatenate([c, c, c, c], axis=-1).reshape(bsz * seq, LANE)
    sin64 = jnp.concatenate([-s, s, -s, s], axis=-1).reshape(bsz * seq, LANE)
    return cos128, sin128, cos64, sin64


def kernel(x, positions, norm1_g, w_in, s5_lambda_re, s5_lambda_im, s5_log_dt, s5_b_re, s5_b_im, s5_c_re, s5_c_im, s5_d, s5_w_glu, nsa_pe_k, nsa_pe_v, nsa_w_cmp_k, nsa_w_cmp_v, diff_lq1, diff_lk1, diff_lq2, diff_lk2, w_out, norm2_g, mlp_w1, mlp_w2):
    bsz, seq, d = x.shape
    t = bsz * seq
    depth = w_in.shape[0]
    tm = min(1024, t)
    cos128, sin128, cos64, sin64 = _rope_tables(positions)
    ret_tabs = _ret_tables()
    x2 = x.reshape(t, d).astype(F32)
    w_in = w_in.astype(F32)
    w_out_bf = w_out.astype(BF16)
    w_glu_bf = s5_w_glu.astype(BF16)
    wk_bf = nsa_w_cmp_k.astype(BF16)
    wv_bf = nsa_w_cmp_v.astype(BF16)
    s5_tabs = _s5_tables(s5_lambda_re, s5_lambda_im, s5_log_dt, s5_b_re, s5_b_im, s5_c_re, s5_c_im, s5_d)
    for layer in range(depth):
        g1 = norm1_g[layer].astype(F32)[None, :]
        proj = _inproj(x2, g1, w_in, w_in[layer:layer + 1, :, GATE_RAW_END:], layer, tm, IN_TN)
        proj_d = proj

        y_a = _s5_mixer(proj, bsz, seq, tuple(a[layer] for a in s5_tabs), w_glu_bf[layer])

        y_b = _retention(proj, cos128, sin128, ret_tabs, bsz, seq)

        kcmp, vcmp, ksr, kwr = _nsa_prep(proj, cos128, sin128, nsa_pe_k[layer], nsa_pe_v[layer],
                                         wk_bf[layer], wv_bf[layer], bsz, seq)
        y_c = _nsa_attn(proj, cos128, sin128, kcmp, vcmp, ksr, kwr, bsz, seq)

        lam_init = 0.8 - 0.6 * math.exp(-0.3 * layer)
        lam = (jnp.exp(jnp.sum(diff_lq1[layer].astype(F32) * diff_lk1[layer].astype(F32)))
               - jnp.exp(jnp.sum(diff_lq2[layer].astype(F32) * diff_lk2[layer].astype(F32))) + lam_init)
        qt, kd = _diff_prep(proj_d, cos64, sin64, tm)
        y_d = _diff_attn(lam.reshape(1, 1).astype(F32), qt, kd, proj_d, bsz, seq, lam_init)

        x2, h2 = _outproj((y_a, y_b, y_c, y_d), w_out_bf, layer, x2,
                          norm2_g[layer].astype(F32)[None, :], min(512, t))
        x2 = _mlp(h2, x2, mlp_w1.astype(F32), mlp_w2.astype(F32), layer, tm, 512)
    return x2.reshape(bsz, seq, d).astype(x.dtype)
```

```python
import functools
import math

import jax
import jax.numpy as jnp
import numpy as np
from jax import lax
from jax.experimental import pallas as pl
from jax.experimental.pallas import tpu as pltpu

F32 = jnp.float32
BF16 = jnp.bfloat16
HIGHEST = lax.Precision.HIGHEST

D_MODEL = 2048
D_GROUP = 512
D_FF = 4 * D_MODEL
ROPE_THETA = 10000.0
EPS = 1e-6
Q_BLOCK = 128
NEG_INF = -1e30

S5_GROUP = 16
S5_G = 32
S5_P = 64
S5_CHUNK = 16

RET_HEADS = 4
RET_DK = 128
RET_CHUNK = 128

NSA_HEADS = 4
NSA_DH = 128
NSA_CMP_LEN = 32
NSA_CMP_STRIDE = 16
NSA_SEL_LEN = 64
NSA_TOPK = 8
NSA_WINDOW = 256
NSA_FORCE_BONUS = 1e4
NSA_SEL_SHIFT = 6
NSA_SEL_TK = 256

DIFF_HEADS = 4
DIFF_DH = 64
DIFF_TQ = 256

LANE = 128
GATE_RAW_END = 3852
IN_TN = 512
MAIN_TILES = 8
CB_U, CB_RQ, CB_RK, CB_RV, CB_RG, CB_NQ = 0, 4, 8, 12, 16, 20
CB_NKC, CB_NVC, CB_NKS, CB_NVS, CB_NKW, CB_NVW, CB_GATE = 24, 25, 26, 27, 28, 29, 30
CB_DQ, CB_DK, CB_DV = 32, 36, 40

VMEM_LIMIT = 56 * 1024 * 1024


def _cparams(sem):
    return pltpu.CompilerParams(dimension_semantics=sem, vmem_limit_bytes=VMEM_LIMIT)


def _dot(a, b, **kw):
    return jnp.dot(a, b, preferred_element_type=F32, **kw)


def _dot_nt(a, b):
    return lax.dot_general(a, b, (((1,), (1,)), ((), ())), preferred_element_type=F32)


def _dot_tn(a, b):
    return lax.dot_general(a, b, (((0,), (0,)), ((), ())), preferred_element_type=F32)


def _rms_rows(x):
    return x * lax.rsqrt(jnp.mean(x * x, axis=-1, keepdims=True) + EPS)


def _rope128(x, cos, sin_signed):
    return x * cos + pltpu.roll(x, 64, axis=1) * sin_signed


def _inproj_kernel(x_ref, g_ref, wm_ref, wd_ref, o_ref, xn_ref):
    j = pl.program_id(1)

    @pl.when(j == 0)
    def _():
        x = x_ref[...]
        xn_ref[...] = (_rms_rows(x) * g_ref[...]).astype(BF16)

    @pl.when(j < MAIN_TILES)
    def _():
        o_ref[...] = _dot(xn_ref[...], wm_ref[0])

    @pl.when(j >= MAIN_TILES)
    def _():
        o_ref[...] = _dot(xn_ref[...], wd_ref[0])


def _inproj(x2, g, w3, w_diff, layer, tm, tn):
    t, d = x2.shape
    n_tiles = MAIN_TILES + w_diff.shape[2] // tn
    return pl.pallas_call(
        _inproj_kernel,
        grid=(t // tm, n_tiles),
        in_specs=[pl.BlockSpec((tm, d), lambda i, j: (i, 0)),
                  pl.BlockSpec((1, d), lambda i, j: (0, 0)),
                  pl.BlockSpec((1, d, tn), lambda i, j: (layer, 0, jnp.minimum(j, MAIN_TILES - 1))),
                  pl.BlockSpec((1, d, tn), lambda i, j: (0, 0, jnp.maximum(j - MAIN_TILES, 0)))],
        out_specs=pl.BlockSpec((tm, tn), lambda i, j: (i, j)),
        out_shape=jax.ShapeDtypeStruct((t, n_tiles * tn), F32),
        scratch_shapes=[pltpu.VMEM((tm, d), BF16)],
        compiler_params=_cparams(("arbitrary", "arbitrary")),
        name="inproj",
    )(x2, g, w3, w_diff)


def _s5_tab_kernel(lre_ref, lim_ref, ldt_ref, b_re_ref, b_im_ref, c_re_ref, c_im_ref,
                   wi_ref, ws_ref, wo_ref, ap_ref):
    tc, p, gs = S5_CHUNK, S5_P, S5_GROUP
    w = tc * gs
    lam_re = lre_ref[0, 0]
    lam_im = lim_ref[0, 0]
    dt = jnp.exp(ldt_ref[0, 0])
    lre = lam_re * dt
    lim = lam_im * dt

    def lbpow(e):
        mag = jnp.exp(lre * e)
        return mag * jnp.cos(lim * e), mag * jnp.sin(lim * e)

    pr, pi = lbpow(lax.broadcasted_iota(jnp.int32, (24, 2 * p), 0).astype(F32))
    lb_re, lb_im = pr[1:2], pi[1:2]
    den = lam_re * lam_re + lam_im * lam_im
    f_re = ((lb_re - 1.0) * lam_re + lb_im * lam_im) / den
    f_im = (lb_im * lam_re - (lb_re - 1.0) * lam_im) / den

    r8 = lax.broadcasted_iota(jnp.int32, (8, 2 * p), 0)
    f8 = jnp.where(r8 == 0, f_re, jnp.where(r8 == 1, f_im, 0.0))
    cols = jnp.concatenate([pr, pi, f8, jnp.zeros((72, 2 * p), F32)], axis=0).T
    pr_c, pi_c = cols[:p, 0:tc], cols[:p, 24:24 + tc]
    f_re_c, f_im_c = cols[:p, 48:49], cols[:p, 49:50]

    b_re = b_re_ref[0, 0]
    b_im = b_im_ref[0, 0]
    bb_re = f_re_c * b_re - f_im_c * b_im
    bb_im = f_re_c * b_im + f_im_c * b_re
    hrow = lax.broadcasted_iota(jnp.int32, (gs, w), 0)
    lane_w = lax.broadcasted_iota(jnp.int32, (gs, w), 1)
    tile_m = jnp.where((lane_w & (gs - 1)) == hrow, 1.0, 0.0)
    rev_m = jnp.where(hrow == (tc - 1) - (lane_w >> 4), 1.0, 0.0)
    bbt_re = _dot(bb_re, tile_m, precision=HIGHEST)
    bbt_im = _dot(bb_im, tile_m, precision=HIGHEST)
    p15_re = _dot(pr_c, rev_m, precision=HIGHEST)
    p15_im = _dot(pi_c, rev_m, precision=HIGHEST)
    ws_ref[0, 0] = jnp.concatenate([p15_re * bbt_re - p15_im * bbt_im,
                                    p15_re * bbt_im + p15_im * bbt_re], axis=0).astype(BF16)

    c_re = c_re_ref[0, 0]
    c_im = c_im_ref[0, 0]
    lo = lax.broadcasted_iota(jnp.int32, (1, 2 * p), 1) < p
    row_a = jnp.where(lo, pr, -pi)
    row_b = jnp.where(lo, -pi, -pr)
    wo_ref[0, 0] = jnp.concatenate([c_re * row_a[t + 1:t + 2] + c_im * row_b[t + 1:t + 2]
                                    for t in range(tc)], axis=0).astype(BF16)

    x_re = jnp.concatenate([c_re * pr[k:k + 1] - c_im * pi[k:k + 1] for k in range(tc)], axis=0)
    x_im = jnp.concatenate([c_re * pi[k:k + 1] + c_im * pr[k:k + 1] for k in range(tc)], axis=0)
    zpad = jnp.zeros((p, w), F32)
    kt = (_dot(x_re, jnp.concatenate([bbt_re, zpad], axis=0), precision=HIGHEST)
          - _dot(x_im, jnp.concatenate([bbt_im, zpad], axis=0), precision=HIGHEST))
    sblk = lane_w >> 4
    rows = []
    for t in range(tc):
        acc = jnp.zeros((gs, w), F32)
        for k in range(t + 1):
            acc = jnp.where(sblk == t - k, kt[k * gs:(k + 1) * gs], acc)
        rows.append(acc)
    wi_ref[0, 0] = jnp.concatenate(rows, axis=0).astype(BF16)

    r16 = lax.broadcasted_iota(jnp.int32, (16, 2 * p), 0)
    sr, si = lbpow(lax.shift_left(jnp.full((16, 2 * p), tc, jnp.int32), r16 >> 1).astype(F32))
    ap_ref[0, 0] = jnp.where((r16 & 1) == 1, jnp.where(lo, -si, si), sr)


def _s5_tables(lam_re, lam_im, log_dt, b_re, b_im, c_re, c_im, d_skip):
    depth, g, p = lam_re.shape
    w = S5_CHUNK * S5_GROUP
    dup = lambda a: jnp.concatenate([a.astype(F32), a.astype(F32)], axis=-1)
    ldt = jnp.broadcast_to(log_dt.astype(F32)[:, :, None, None], (depth, g, 1, 2 * p))
    row = pl.BlockSpec((1, 1, 1, 2 * p), lambda l, i: (l, i, 0, 0))
    bspec = pl.BlockSpec((1, 1, p, S5_GROUP), lambda l, i: (l, i, 0, 0))
    cspec = pl.BlockSpec((1, 1, S5_GROUP, 2 * p), lambda l, i: (l, i, 0, 0))
    out = lambda r, c: pl.BlockSpec((1, 1, r, c), lambda l, i: (l, i, 0, 0))
    tabs = pl.pallas_call(
        _s5_tab_kernel,
        grid=(depth, g),
        in_specs=[row, row, row, bspec, bspec, cspec, cspec],
        out_specs=[out(w, w), out(2 * p, w), out(w, 2 * p), out(16, 2 * p)],
        out_shape=[jax.ShapeDtypeStruct((depth, g, w, w), BF16),
                   jax.ShapeDtypeStruct((depth, g, 2 * p, w), BF16),
                   jax.ShapeDtypeStruct((depth, g, w, 2 * p), BF16),
                   jax.ShapeDtypeStruct((depth, g, 16, 2 * p), F32)],
        compiler_params=_cparams(("arbitrary", "arbitrary")),
        name="s5_tables",
    )(dup(lam_re)[:, :, None, :], dup(lam_im)[:, :, None, :], ldt, b_re.astype(F32), b_im.astype(F32),
      dup(c_re), dup(c_im))
    d_row = d_skip.astype(F32).reshape(depth, 1, g * S5_GROUP)
    return (*tabs, d_row)


def _s5_pack_kernel(*refs):
    at_ref = refs[-1]
    nc = at_ref.shape[2]
    for t in range(S5_CHUNK):
        for j, u_ref in enumerate(refs[:-1]):
            at_ref[t, j * LANE:(j + 1) * LANE, :] = u_ref[pl.ds(t, nc, stride=S5_CHUNK), :].T.astype(BF16)


def _u_specs(seq):
    return [pl.BlockSpec((seq, LANE), lambda b, j=j: (b, CB_U + j)) for j in range(D_GROUP // LANE)]


def _s5_pack(proj, bsz, seq):
    nc = seq // S5_CHUNK
    return pl.pallas_call(
        _s5_pack_kernel,
        grid=(bsz,),
        in_specs=_u_specs(seq),
        out_specs=pl.BlockSpec((S5_CHUNK, D_GROUP, nc), lambda b: (0, 0, b)),
        out_shape=jax.ShapeDtypeStruct((S5_CHUNK, D_GROUP, bsz * nc), BF16),
        compiler_params=_cparams(("arbitrary",)),
        name="s5_pack",
    )(*([proj] * (D_GROUP // LANE)))


def _s5_kernel(ut_ref, wi_ref, ws_ref, wo_ref, ap_ref, yt_ref, *, n_chunk):
    tc, gs, r = ut_ref.shape
    ut = ut_ref[...].reshape(tc * gs, r)
    s = _dot(ws_ref[0], ut)
    lane = lax.broadcasted_iota(jnp.int32, s.shape, 1) & (n_chunk - 1)
    apt = ap_ref[0].T
    x = s
    n_step = int(math.log2(n_chunk))
    for k in range(n_step):
        sh = 1 << k
        xs = jnp.where(lane >= sh, pltpu.roll(x, sh, axis=1), 0.0)
        xsw = jnp.concatenate([xs[S5_P:], xs[:S5_P]], axis=0)
        x = x + xs * apt[:, 2 * k:2 * k + 1] + xsw * apt[:, 2 * k + 1:2 * k + 2]
    xp = jnp.where(lane >= 1, pltpu.roll(x, 1, axis=1), 0.0)
    y = _dot(wi_ref[0], ut) + _dot(wo_ref[0], xp.astype(BF16))
    yt_ref[...] = y.reshape(tc, gs, r)


def _s5_core(at, tabs, n_chunk):
    tc, ch, r = at.shape
    w_intra_t, w_state_t, w_out_t, apow, _ = tabs
    g = w_intra_t.shape[0]
    w = tc * S5_GROUP
    return pl.pallas_call(
        functools.partial(_s5_kernel, n_chunk=n_chunk),
        grid=(g,),
        in_specs=[pl.BlockSpec((tc, S5_GROUP, r), lambda i: (0, i, 0)),
                  pl.BlockSpec((1, w, w), lambda i: (i, 0, 0)),
                  pl.BlockSpec((1, 2 * S5_P, w), lambda i: (i, 0, 0)),
                  pl.BlockSpec((1, w, 2 * S5_P), lambda i: (i, 0, 0)),
                  pl.BlockSpec((1, 16, 2 * S5_P), lambda i: (i, 0, 0))],
        out_specs=pl.BlockSpec((tc, S5_GROUP, r), lambda i: (0, i, 0)),
        out_shape=jax.ShapeDtypeStruct((tc, ch, r), F32),
        compiler_params=_cparams(("arbitrary",)),
        name="s5_core",
    )(at, w_intra_t, w_state_t, w_out_t, apow)


def _s5_out_kernel(yt_ref, u0_ref, u1_ref, u2_ref, u3_ref, d_ref, wg_ref, o_ref, y_scr):
    nc = yt_ref.shape[2]
    for t in range(S5_CHUNK):
        rows = pl.ds(t, nc, stride=S5_CHUNK)
        for j, u_ref in enumerate((u0_ref, u1_ref, u2_ref, u3_ref)):
            sl = slice(j * LANE, (j + 1) * LANE)
            y_scr[j, rows, :] = jax.nn.gelu(yt_ref[t, sl, :].T + d_ref[:, sl] * u_ref[rows, :])
    y = jnp.concatenate([y_scr[j] for j in range(D_GROUP // LANE)], axis=1)
    o_ref[...] = (y * jax.nn.sigmoid(_dot(y.astype(BF16), wg_ref[...]))).astype(BF16)


def _s5_out(yt, proj, d_row, w_glu_bf, bsz, seq):
    nc = seq // S5_CHUNK
    return pl.pallas_call(
        _s5_out_kernel,
        grid=(bsz,),
        in_specs=[pl.BlockSpec((S5_CHUNK, D_GROUP, nc), lambda b: (0, 0, b))] + _u_specs(seq)
        + [pl.BlockSpec((1, D_GROUP), lambda b: (0, 0)),
           pl.BlockSpec((D_GROUP, D_GROUP), lambda b: (0, 0))],
        out_specs=pl.BlockSpec((seq, D_GROUP), lambda b: (b, 0)),
        out_shape=jax.ShapeDtypeStruct((bsz * seq, D_GROUP), BF16),
        scratch_shapes=[pltpu.VMEM((D_GROUP // LANE, seq, LANE), F32)],
        compiler_params=_cparams(("arbitrary",)),
        name="s5_out",
    )(yt, proj, proj, proj, proj, d_row, w_glu_bf)


def _s5_mixer(proj, bsz, seq, tabs, w_glu_bf):
    yt = _s5_core(_s5_pack(proj, bsz, seq), tabs, seq // S5_CHUNK)
    return _s5_out(yt, proj, tabs[4], w_glu_bf, bsz, seq)


def _ret_tables():
    h, c = RET_HEADS, RET_CHUNK
    log_g = jnp.log(1.0 - 2.0 ** (-5.0 - jnp.arange(h, dtype=F32)))
    idx = jnp.arange(c, dtype=F32)
    diff = idx[:, None] - idx[None, :]
    dmat = jnp.where(diff >= 0, jnp.exp(jnp.maximum(diff, 0.0)[None] * log_g[:, None, None]), 0.0)
    zeta = jnp.exp((c - 1.0 - idx)[None, :] * log_g[:, None])
    xi = jnp.exp((idx + 1.0)[None, :] * log_g[:, None])
    g_chunk = jnp.exp(c * log_g)
    zeta_b = jnp.broadcast_to(zeta[:, :, None], (h, c, RET_DK))
    xi_b = jnp.broadcast_to(xi[:, :, None], (h, c, RET_DK))
    gch_b = jnp.broadcast_to(g_chunk[:, None, None], (h, 8, RET_DK))
    return dmat, zeta_b, xi_b, gch_b


def _ret_kernel(q_ref, k_ref, v_ref, g_ref, cos_ref, sin_ref, dm_ref, ze_ref, xi_ref, gc_ref,
                o_ref, *, n_chunk):
    c = RET_CHUNK
    dm = dm_ref[0]
    ze = ze_ref[0]
    xi = xi_ref[0]
    gch = gc_ref[0, 0:1, :]

    def body(n, r):
        sl = pl.ds(pl.multiple_of(n * c, c), c)
        cos = cos_ref[sl, :]
        sin = sin_ref[sl, :]
        qc = _rope128(q_ref[sl, :], cos, sin)
        kc = _rope128(k_ref[sl, :], cos, sin) * (RET_DK ** -0.5)
        vc = v_ref[sl, :].astype(BF16)
        inner = _dot_nt(qc.astype(BF16), kc.astype(BF16)) * dm
        o = _dot(inner.astype(BF16), vc) + _dot((qc * xi).astype(BF16), r.astype(BF16))
        kv = _dot_tn((kc * ze).astype(BF16), vc)
        r = r * gch + kv
        mu = jnp.mean(o, axis=-1, keepdims=True)
        oc = o - mu
        var = jnp.mean(oc * oc, axis=-1, keepdims=True)
        on = oc * lax.rsqrt(var + EPS)
        gg = g_ref[sl, :]
        o_ref[sl, :] = (gg * jax.nn.sigmoid(gg) * on).astype(BF16)
        return r

    lax.fori_loop(0, n_chunk, body, jnp.zeros((RET_DK, RET_DK), F32), unroll=4)


def _retention(proj, cos128, sin128, tabs, bsz, seq):
    dmat, zeta_b, xi_b, gch_b = tabs
    h = RET_HEADS
    col = lambda cb: pl.BlockSpec((seq, LANE), lambda b, hh, cb=cb: (b, cb + hh))
    tab = pl.BlockSpec((seq, LANE), lambda b, hh: (b, 0))
    head = lambda r: pl.BlockSpec((1, r, LANE), lambda b, hh: (hh, 0, 0))
    return pl.pallas_call(
        functools.partial(_ret_kernel, n_chunk=seq // RET_CHUNK),
        grid=(bsz, h),
        in_specs=[col(CB_RQ), col(CB_RK), col(CB_RV), col(CB_RG), tab, tab,
                  head(RET_CHUNK), head(RET_CHUNK), head(RET_CHUNK), head(8)],
        out_specs=pl.BlockSpec((seq, LANE), lambda b, hh: (b, hh)),
        out_shape=jax.ShapeDtypeStruct((bsz * seq, D_GROUP), BF16),
        compiler_params=_cparams(("arbitrary", "arbitrary")),
        name="retention",
    )(proj, proj, proj, proj, cos128, sin128, dmat, zeta_b, xi_b, gch_b)


def _nsa_prep_kernel(kc_ref, vc_ref, pek_ref, pev_ref, wk_ref, wv_ref, ks_ref, kw_ref, cos_ref, sin_ref,
                     q_ref, g_ref, kcmp_ref, vcmp_ref, ksr_ref, kwr_ref, qt_ref, gt_ref):
    half = NSA_CMP_STRIDE * NSA_DH
    for h in range(NSA_HEADS):
        sl = slice(h * NSA_DH, (h + 1) * NSA_DH)
        qt_ref[sl, :] = _rope128(_rms_rows(q_ref[:, sl]), cos_ref[...], sin_ref[...]).T.astype(BF16)
    gt_ref[...] = jax.nn.sigmoid(g_ref[...]).T
    n16 = kcmp_ref.shape[1]
    kparts, vparts = [], []
    for i in range(NSA_CMP_STRIDE):
        rows = pl.ds(i, n16, stride=NSA_CMP_STRIDE)
        kparts.append(_rope128(kc_ref[rows, :], cos_ref[rows, :], sin_ref[rows, :]))
        vparts.append(vc_ref[rows, :])
    xr = jnp.concatenate(kparts, axis=1)
    xv = jnp.concatenate(vparts, axis=1)
    a = _dot((xr + pek_ref[0:1, :]).astype(BF16), wk_ref[0:half, :])
    b = _dot((xr + pek_ref[1:2, :]).astype(BF16), wk_ref[half:2 * half, :])
    kcm = _rms_rows(a + pltpu.roll(b, n16 - 1, axis=0))
    rowi = lax.broadcasted_iota(jnp.int32, kcm.shape, 0)
    kcmp_ref[0] = jnp.where(rowi < n16 - 1, kcm, 0.0).astype(BF16)
    av = _dot((xv + pev_ref[0:1, :]).astype(BF16), wv_ref[0:half, :])
    bv = _dot((xv + pev_ref[1:2, :]).astype(BF16), wv_ref[half:2 * half, :])
    vcm = av + pltpu.roll(bv, n16 - 1, axis=0)
    vcmp_ref[0] = jnp.where(rowi < n16 - 1, vcm, 0.0).astype(BF16)
    cos = cos_ref[...]
    sin = sin_ref[...]
    ksr_ref[...] = _rope128(_rms_rows(ks_ref[...]), cos, sin).astype(BF16)
    kwr_ref[...] = _rope128(_rms_rows(kw_ref[...]), cos, sin).astype(BF16)


def _nsa_prep(proj, cos128, sin128, pe_k, pe_v, wk_bf, wv_bf, bsz, seq):
    n16 = seq // NSA_CMP_STRIDE
    half = NSA_CMP_STRIDE * NSA_DH
    pek = pe_k.astype(F32).reshape(2, half)
    pev = pe_v.astype(F32).reshape(2, half)
    full = lambda a: pl.BlockSpec(a.shape, lambda b: (0,) * a.ndim)
    col = lambda cb: pl.BlockSpec((seq, LANE), lambda b, cb=cb: (b, cb))
    tab = pl.BlockSpec((seq, LANE), lambda b: (b, 0))
    cmp_spec = pl.BlockSpec((1, n16, NSA_DH), lambda b: (b, 0, 0))
    seq_spec = pl.BlockSpec((seq, LANE), lambda b: (b, 0))
    return pl.pallas_call(
        _nsa_prep_kernel,
        grid=(bsz,),
        in_specs=[col(CB_NKC), col(CB_NVC), full(pek), full(pev), full(wk_bf), full(wv_bf),
                  col(CB_NKS), col(CB_NKW), tab, tab,
                  pl.BlockSpec((seq, D_GROUP), lambda b: (b, CB_NQ // 4)), col(CB_GATE)],
        out_specs=[cmp_spec, cmp_spec, seq_spec, seq_spec,
                   pl.BlockSpec((D_GROUP, seq), lambda b: (0, b)), pl.BlockSpec((LANE, seq), lambda b: (0, b))],
        out_shape=[jax.ShapeDtypeStruct((bsz, n16, NSA_DH), BF16),
                   jax.ShapeDtypeStruct((bsz, n16, NSA_DH), BF16),
                   jax.ShapeDtypeStruct((bsz * seq, NSA_DH), BF16),
                   jax.ShapeDtypeStruct((bsz * seq, NSA_DH), BF16),
                   jax.ShapeDtypeStruct((D_GROUP, bsz * seq), BF16),
                   jax.ShapeDtypeStruct((LANE, bsz * seq), F32)],
        compiler_params=_cparams(("arbitrary",)),
        name="nsa_prep",
    )(proj, proj, pek, pev, wk_bf, wv_bf, proj, proj, cos128, sin128, proj, proj)


def _attn_finish(l, acc):
    return jnp.where(l > 0.0, acc / jnp.where(l > 0.0, l, 1.0), 0.0)


def _nsa_kernel(qt_ref, gt_ref, kc_ref, vc_ref, ks_ref, vs_ref, kw_ref, vw_ref, ov_ref, o_ref):
    hh, qb, dh = NSA_HEADS, Q_BLOCK, NSA_DH
    scale = dh ** -0.5
    n = pl.program_id(1)
    qq = jnp.concatenate([qt_ref[h * dh:(h + 1) * dh, :] for h in range(hh)], axis=1)

    def branch(keys, vals, maskf):
        s = _dot(keys, qq) * scale
        e = jnp.where(jnp.concatenate([maskf] * hh, axis=1) > 0.5, jnp.exp(s), 0.0)
        return jnp.sum(e, axis=0, keepdims=True), _dot_tn(vals, e.astype(BF16)), e

    def qpos(rows):
        return n * qb + lax.broadcasted_iota(jnp.int32, (rows, qb), 1)

    def krow(rows):
        return lax.broadcasted_iota(jnp.int32, (rows, qb), 0)

    ncp = kc_ref.shape[1]
    vis = jnp.where(krow(ncp) * NSA_CMP_STRIDE + (NSA_CMP_LEN - 1) <= qpos(ncp), 1.0, 0.0)
    l_c, acc_c, e_c = branch(kc_ref[0], vc_ref[0], vis)
    o_cmp = _attn_finish(l_c, acc_c)
    p = e_c / jnp.where(l_c > 0.0, l_c, 1.0)
    psum = p[:, 0:qb]
    for h in range(1, hh):
        psum = psum + p[:, h * qb:(h + 1) * qb]

    n_sel = ks_ref.shape[0] // NSA_SEL_LEN
    imp = _dot(ov_ref[...], psum, precision=HIGHEST)[:n_sel]
    j = krow(n_sel)
    cur = qpos(n_sel) >> NSA_SEL_SHIFT
    forced = (j == 0) | (j == cur) | (j == cur - 1)
    score = jnp.where(j <= cur, imp + jnp.where(forced, NSA_FORCE_BONUS, 0.0), NEG_INF)
    rank = jnp.zeros((n_sel, qb), F32)
    for jp in range(n_sel):
        cj = score[jp:jp + 1, :]
        tie = jnp.where(j > jp, 1.0, 0.0)
        rank = rank + jnp.where(cj > score, 1.0, jnp.where(cj == score, tie, 0.0))
    selm = jnp.where(rank < float(min(NSA_TOPK, n_sel)), 1.0, 0.0)
    selm = jnp.concatenate([selm, jnp.zeros((LANE - n_sel, qb), F32)], axis=0).astype(BF16)

    tk = NSA_SEL_TK
    blk = lax.broadcasted_iota(jnp.int32, (tk, LANE), 0) >> NSA_SEL_SHIFT
    selcol = lax.broadcasted_iota(jnp.int32, (tk, LANE), 1)

    def sel_tile(kt):
        sl = pl.ds(pl.multiple_of(kt * tk, tk), tk)
        expand = jnp.where(selcol == kt * (tk // NSA_SEL_LEN) + blk, 1.0, 0.0).astype(BF16)
        picked = _dot(expand, selm)
        maskf = jnp.where(kt * tk + krow(tk) <= qpos(tk), picked, 0.0)
        l, acc, _ = branch(ks_ref[sl, :], vs_ref[sl, :].astype(BF16), maskf)
        return l, acc

    def sel_body(i, carry):
        l0, a0 = sel_tile(2 * i)
        l1, a1 = sel_tile(2 * i + 1)
        return carry[0] + (l0 + l1), carry[1] + (a0 + a1)

    zero = (jnp.zeros((1, hh * qb), F32), jnp.zeros((dh, hh * qb), F32))
    trips = (n * qb + qb + 2 * tk - 1) // (2 * tk)
    o_sel = _attn_finish(*lax.fori_loop(0, trips, sel_body, zero))

    wk = NSA_WINDOW + qb
    start = jnp.maximum(n * qb - NSA_WINDOW, 0)
    sl = pl.ds(pl.multiple_of(start, qb), wk)
    dist = qpos(wk) - (start + krow(wk))
    inwin = jnp.where(dist >= 0, jnp.where(dist < NSA_WINDOW, 1.0, 0.0), 0.0)
    l_w, acc_w, _ = branch(kw_ref[sl, :], vw_ref[sl, :].astype(BF16), inwin)
    o_win = _attn_finish(l_w, acc_w)

    gt = gt_ref[...]
    for h in range(hh):
        hq = slice(h * qb, (h + 1) * qb)
        o = (gt[3 * h:3 * h + 1, :] * o_cmp[:, hq] + gt[3 * h + 1:3 * h + 2, :] * o_sel[:, hq]
             + gt[3 * h + 2:3 * h + 3, :] * o_win[:, hq])
        o_ref[:, h * dh:(h + 1) * dh] = o.T.astype(BF16)


def _nsa_overlap_table(seq):
    n_cmp = (seq - NSA_CMP_LEN) // NSA_CMP_STRIDE + 1
    n_sel = seq // NSA_SEL_LEN
    cmp_start = np.arange(n_cmp) * NSA_CMP_STRIDE
    sel_start = np.arange(n_sel) * NSA_SEL_LEN
    overlap = ((cmp_start[None, :] < sel_start[:, None] + NSA_SEL_LEN)
               & (cmp_start[None, :] + NSA_CMP_LEN > sel_start[:, None])).astype(np.float32)
    tab = np.zeros((LANE, n_cmp + 1), np.float32)
    tab[:n_sel, :n_cmp] = overlap
    return jnp.asarray(tab)


def _nsa_attn(proj, qt, gt, kcmp, vcmp, ksr, kwr, bsz, seq):
    nb = seq // Q_BLOCK
    cmp_spec = pl.BlockSpec((1, kcmp.shape[1], NSA_DH), lambda b, n: (b, 0, 0))
    kseq = pl.BlockSpec((seq, LANE), lambda b, n: (b, 0))
    vcol = lambda cb: pl.BlockSpec((seq, LANE), lambda b, n, cb=cb: (b, cb))
    ov = _nsa_overlap_table(seq)
    return pl.pallas_call(
        _nsa_kernel,
        grid=(bsz, nb),
        in_specs=[pl.BlockSpec((D_GROUP, Q_BLOCK), lambda b, n: (0, b * nb + n)),
                  pl.BlockSpec((LANE, Q_BLOCK), lambda b, n: (0, b * nb + n)),
                  cmp_spec, cmp_spec, kseq, vcol(CB_NVS), kseq, vcol(CB_NVW),
                  pl.BlockSpec(ov.shape, lambda b, n: (0, 0))],
        out_specs=pl.BlockSpec((Q_BLOCK, D_GROUP), lambda b, n: (b * nb + n, 0)),
        out_shape=jax.ShapeDtypeStruct((bsz * seq, D_GROUP), BF16),
        compiler_params=_cparams(("arbitrary", "arbitrary")),
        name="nsa_attn",
    )(qt, gt, kcmp, vcmp, ksr, proj, kwr, proj, ov)


def _diff_prep_kernel(q_ref, k_ref, cos_ref, sin_ref, qt_ref, ko_ref):
    cos = cos_ref[...]
    sin = sin_ref[...]
    lane = lax.broadcasted_iota(jnp.int32, cos.shape, 1)
    lo = lane < DIFF_DH
    first = (lane & (DIFF_DH - 1)) < DIFF_DH // 2

    def prep(x):
        x2 = x * x
        ss_lo = jnp.sum(jnp.where(lo, x2, 0.0), axis=-1, keepdims=True)
        ss_hi = jnp.sum(jnp.where(lo, 0.0, x2), axis=-1, keepdims=True)
        ms = jnp.where(lo, ss_lo, ss_hi) * (1.0 / DIFF_DH)
        xn = x * lax.rsqrt(ms + EPS)
        partner = jnp.where(first, pltpu.roll(xn, LANE - DIFF_DH // 2, axis=1),
                            pltpu.roll(xn, DIFF_DH // 2, axis=1))
        return xn * cos + partner * sin

    for h in range(DIFF_HEADS):
        sl = slice(h * LANE, (h + 1) * LANE)
        qt_ref[sl, :] = (prep(q_ref[:, sl]) * (DIFF_DH ** -0.5)).T.astype(BF16)
        ko_ref[:, sl] = prep(k_ref[:, sl]).astype(BF16)


def _diff_prep(proj, cos64, sin64, tm):
    t = proj.shape[0]
    blk = lambda cb: pl.BlockSpec((tm, D_GROUP), lambda i, cb=cb: (i, cb // 4))
    tab = pl.BlockSpec((tm, LANE), lambda i: (i, 0))
    return pl.pallas_call(
        _diff_prep_kernel,
        grid=(t // tm,),
        in_specs=[blk(CB_DQ), blk(CB_DK), tab, tab],
        out_specs=[pl.BlockSpec((D_GROUP, tm), lambda i: (0, i)),
                   pl.BlockSpec((tm, D_GROUP), lambda i: (i, 0))],
        out_shape=[jax.ShapeDtypeStruct((D_GROUP, t), BF16), jax.ShapeDtypeStruct((t, D_GROUP), BF16)],
        compiler_params=_cparams(("arbitrary",)),
        name="diff_prep",
    )(proj, proj, cos64, sin64)


def _diff_kernel(lam_ref, qt_ref, k_ref, v_ref, o_ref, acc_ref, l_ref, *, out_scale):
    nq = DIFF_TQ
    qi = pl.program_id(2)
    qt = qt_ref[...]
    row = lax.broadcasted_iota(jnp.int32, qt.shape, 0)
    zero = jnp.zeros_like(qt)
    qq = jnp.concatenate([jnp.where(row < DIFF_DH, qt, zero), jnp.where(row < DIFF_DH, zero, qt)], axis=1)
    acc_ref[...] = jnp.zeros_like(acc_ref)
    l_ref[...] = jnp.zeros_like(l_ref)

    def tile(kt, masked):
        sl = pl.ds(pl.multiple_of(kt * nq, nq), nq)
        p = jnp.exp(_dot(k_ref[sl, :], qq))
        if masked:
            kpos = lax.broadcasted_iota(jnp.int32, p.shape, 0)
            qpos = lax.broadcasted_iota(jnp.int32, p.shape, 1) & (nq - 1)
            p = jnp.where(kpos <= qpos, p, 0.0)
        return jnp.sum(p, axis=0, keepdims=True), _dot_tn(v_ref[sl, :].astype(BF16), p.astype(BF16))

    def body(i, c):
        l0, a0 = tile(2 * i, False)
        l1, a1 = tile(2 * i + 1, False)
        l_ref[...] += l0 + l1
        acc_ref[...] += a0 + a1
        return c

    lax.fori_loop(0, qi >> 1, body, 0)

    @pl.when((qi & 1) == 1)
    def _():
        l0, a0 = tile(qi - 1, False)
        l_ref[...] += l0
        acc_ref[...] += a0

    l0, a0 = tile(qi, True)
    l_ref[...] += l0
    acc_ref[...] += a0
    l = l_ref[...]
    acc = acc_ref[...]
    o = acc[:, :nq] / l[:, :nq] - lam_ref[0, 0] * (acc[:, nq:] / l[:, nq:])
    on = o * lax.rsqrt(jnp.mean(o * o, axis=0, keepdims=True) + EPS) * out_scale
    o_ref[...] = on.T.astype(BF16)


def _diff_attn(lam, qt, kd, proj, bsz, seq, lam_init):
    nq = DIFF_TQ
    nb = seq // nq
    h = DIFF_HEADS
    return pl.pallas_call(
        functools.partial(_diff_kernel, out_scale=1.0 - lam_init),
        grid=(bsz, h, nb),
        in_specs=[pl.BlockSpec(memory_space=pltpu.SMEM),
                  pl.BlockSpec((LANE, nq), lambda b, hh, n: (hh, b * nb + n)),
                  pl.BlockSpec((seq, LANE), lambda b, hh, n: (b, hh)),
                  pl.BlockSpec((seq, LANE), lambda b, hh, n: (b, CB_DV + hh))],
        out_specs=pl.BlockSpec((nq, LANE), lambda b, hh, n: (b * nb + n, hh)),
        out_shape=jax.ShapeDtypeStruct((bsz * seq, D_GROUP), BF16),
        scratch_shapes=[pltpu.VMEM((LANE, 2 * nq), F32), pltpu.VMEM((1, 2 * nq), F32)],
        compiler_params=_cparams(("arbitrary", "arbitrary", "arbitrary")),
        name="diff_attn",
    )(lam, qt, kd, proj)


def _outproj_kernel(ya_ref, yb_ref, yc_ref, yd_ref, w_ref, x_ref, g_ref, xo_ref, h_ref):
    mixed = jnp.concatenate([ya_ref[...], yb_ref[...], yc_ref[...], yd_ref[...]], axis=1)
    x = x_ref[...] + _dot(mixed, w_ref[0])
    xo_ref[...] = x
    h_ref[...] = (_rms_rows(x) * g_ref[...]).astype(BF16)


def _outproj(ys, w3_bf, layer, x2, g, tm):
    t, d = x2.shape
    yspec = pl.BlockSpec((tm, D_GROUP), lambda i: (i, 0))
    row = pl.BlockSpec((tm, d), lambda i: (i, 0))
    return pl.pallas_call(
        _outproj_kernel,
        grid=(t // tm,),
        in_specs=[yspec, yspec, yspec, yspec,
                  pl.BlockSpec((1, d, d), lambda i: (layer, 0, 0)), row,
                  pl.BlockSpec((1, d), lambda i: (0, 0))],
        out_specs=[row, row],
        out_shape=[jax.ShapeDtypeStruct((t, d), F32), jax.ShapeDtypeStruct((t, d), BF16)],
        compiler_params=_cparams(("arbitrary",)),
        name="outproj",
    )(*ys, w3_bf, x2, g)


def _mlp_kernel(h_ref, x_ref, w1_ref, w2_ref, o_ref):
    @pl.when(pl.program_id(1) == 0)
    def _():
        o_ref[...] = x_ref[...]

    a = _dot(h_ref[...], w1_ref[0].astype(BF16))
    a = jnp.square(jnp.maximum(a, 0.0)).astype(BF16)
    o_ref[...] += _dot(a, w2_ref[0].astype(BF16))


def _mlp(h2, x2, w1, w2, layer, tm, tf):
    t, d = x2.shape
    f = w1.shape[2]
    return pl.pallas_call(
        _mlp_kernel,
        grid=(t // tm, f // tf),
        in_specs=[pl.BlockSpec((tm, d), lambda i, j: (i, 0)),
                  pl.BlockSpec((tm, d), lambda i, j: (i, 0), pipeline_mode=pl.Buffered(1)),
                  pl.BlockSpec((1, d, tf), lambda i, j: (layer, 0, j)),
                  pl.BlockSpec((1, tf, d), lambda i, j: (layer, j, 0))],
        out_specs=pl.BlockSpec((tm, d), lambda i, j: (i, 0)),
        out_shape=jax.ShapeDtypeStruct((t, d), F32),
        compiler_params=_cparams(("arbitrary", "arbitrary")),
        name="mlp",
    )(h2, x2, w1, w2)


def _rope_tables(positions):
    bsz, seq = positions.shape
    pos = positions.astype(F32)[..., None]

    def tab(d):
        inv = ROPE_THETA ** (-jnp.arange(0, d, 2, dtype=F32) / d)
        ang = pos * inv
        return jnp.cos(ang), jnp.sin(ang)

    c, s = tab(NSA_DH)
    cos128 = jnp.concatenate([c, c], axis=-1).reshape(bsz * seq, LANE)
    sin128 = jnp.concatenate([-s, s], axis=-1).reshape(bsz * seq, LANE)
    c, s = tab(DIFF_DH)
    cos64 = jnp.concatenate([c, c, c, c], axis=-1).reshape(bsz * seq, LANE)
    sin64 = jnp.concatenate([-s, s, -s, s], axis=-1).reshape(bsz * seq, LANE)
    return cos128, sin128, cos64, sin64


def kernel(x, positions, norm1_g, w_in, s5_lambda_re, s5_lambda_im, s5_log_dt, s5_b_re, s5_b_im, s5_c_re, s5_c_im, s5_d, s5_w_glu, nsa_pe_k, nsa_pe_v, nsa_w_cmp_k, nsa_w_cmp_v, diff_lq1, diff_lk1, diff_lq2, diff_lk2, w_out, norm2_g, mlp_w1, mlp_w2):
    bsz, seq, d = x.shape
    t = bsz * seq
    depth = w_in.shape[0]
    tm = min(1024, t)
    cos128, sin128, cos64, sin64 = _rope_tables(positions)
    ret_tabs = _ret_tables()
    x2 = x.reshape(t, d).astype(F32)
    w_in = w_in.astype(BF16)
    w_out_bf = w_out.astype(BF16)
    w_glu_bf = s5_w_glu.astype(BF16)
    wk_bf = nsa_w_cmp_k.astype(BF16)
    wv_bf = nsa_w_cmp_v.astype(BF16)
    s5_tabs = _s5_tables(s5_lambda_re, s5_lambda_im, s5_log_dt, s5_b_re, s5_b_im, s5_c_re, s5_c_im, s5_d)
    for layer in range(depth):
        g1 = norm1_g[layer].astype(F32)[None, :]
        proj = _inproj(x2, g1, w_in, w_in[layer:layer + 1, :, GATE_RAW_END:], layer, tm, IN_TN)
        proj_d = proj

        y_a = _s5_mixer(proj, bsz, seq, tuple(a[layer] for a in s5_tabs), w_glu_bf[layer])

        y_b = _retention(proj, cos128, sin128, ret_tabs, bsz, seq)

        kcmp, vcmp, ksr, kwr, nqt, ngt = _nsa_prep(proj, cos128, sin128, nsa_pe_k[layer], nsa_pe_v[layer],
                                                   wk_bf[layer], wv_bf[layer], bsz, seq)
        y_c = _nsa_attn(proj, nqt, ngt, kcmp, vcmp, ksr, kwr, bsz, seq)

        lam_init = 0.8 - 0.6 * math.exp(-0.3 * layer)
        lam = (jnp.exp(jnp.sum(diff_lq1[layer].astype(F32) * diff_lk1[layer].astype(F32)))
               - jnp.exp(jnp.sum(diff_lq2[layer].astype(F32) * diff_lk2[layer].astype(F32))) + lam_init)
        qt, kd = _diff_prep(proj_d, cos64, sin64, tm)
        y_d = _diff_attn(lam.reshape(1, 1).astype(F32), qt, kd, proj_d, bsz, seq, lam_init)

        x2, h2 = _outproj((y_a, y_b, y_c, y_d), w_out_bf, layer, x2,
                          norm2_g[layer].astype(F32)[None, :], min(512, t))
        x2 = _mlp(h2, x2, mlp_w1.astype(F32), mlp_w2.astype(F32), layer, tm, 512)
    return x2.reshape(bsz, seq, d).astype(x.dtype)
```

```python
import functools
import math

import jax
import jax.numpy as jnp
import numpy as np
from jax import lax
from jax.experimental import pallas as pl
from jax.experimental.pallas import tpu as pltpu

F32 = jnp.float32
BF16 = jnp.bfloat16
HIGHEST = lax.Precision.HIGHEST

D_MODEL = 2048
D_GROUP = 512
D_FF = 4 * D_MODEL
ROPE_THETA = 10000.0
EPS = 1e-6
Q_BLOCK = 128
NEG_INF = -1e30

S5_GROUP = 16
S5_G = 32
S5_P = 64
S5_CHUNK = 16

RET_HEADS = 4
RET_DK = 128
RET_CHUNK = 128

NSA_HEADS = 4
NSA_DH = 128
NSA_CMP_LEN = 32
NSA_CMP_STRIDE = 16
NSA_SEL_LEN = 64
NSA_TOPK = 8
NSA_WINDOW = 256
NSA_FORCE_BONUS = 1e4
NSA_SEL_SHIFT = 6
NSA_SEL_TK = 256

DIFF_HEADS = 4
DIFF_DH = 64
DIFF_TQ = 512
DIFF_TK = 256

LANE = 128
GATE_RAW_END = 3852
IN_TN = 512
MAIN_TILES = 8
CB_U, CB_RQ, CB_RK, CB_RV, CB_RG, CB_NQ = 0, 4, 8, 12, 16, 20
CB_NKC, CB_NVC, CB_NKS, CB_NVS, CB_NKW, CB_NVW, CB_GATE = 24, 25, 26, 27, 28, 29, 30
CB_DQ, CB_DK, CB_DV = 32, 36, 40

VMEM_LIMIT = 56 * 1024 * 1024
IN_TM = 2048
IN_VMEM_LIMIT = 60 * 1024 * 1024


def _cparams(sem):
    return pltpu.CompilerParams(dimension_semantics=sem, vmem_limit_bytes=VMEM_LIMIT)


def _dot(a, b, **kw):
    return jnp.dot(a, b, preferred_element_type=F32, **kw)


def _dot_nt(a, b):
    return lax.dot_general(a, b, (((1,), (1,)), ((), ())), preferred_element_type=F32)


def _dot_tn(a, b):
    return lax.dot_general(a, b, (((0,), (0,)), ((), ())), preferred_element_type=F32)


def _rms_rows(x):
    return x * lax.rsqrt(jnp.mean(x * x, axis=-1, keepdims=True) + EPS)


def _rope128(x, cos, sin_signed):
    return x * cos + pltpu.roll(x, 64, axis=1) * sin_signed


def _inproj_kernel(x_ref, g_ref, wm_ref, wd_ref, o_ref, xn_ref):
    j = pl.program_id(1)

    @pl.when(j == 0)
    def _():
        x = x_ref[...]
        xn_ref[...] = (_rms_rows(x) * g_ref[...]).astype(BF16)

    @pl.when(j < MAIN_TILES)
    def _():
        o_ref[...] = _dot(xn_ref[...], wm_ref[0])

    @pl.when(j >= MAIN_TILES)
    def _():
        o_ref[...] = _dot(xn_ref[...], wd_ref[0])


def _inproj(x2, g, w3, w_diff, layer, tm, tn):
    t, d = x2.shape
    n_tiles = MAIN_TILES + w_diff.shape[2] // tn
    return pl.pallas_call(
        _inproj_kernel,
        grid=(t // tm, n_tiles),
        in_specs=[pl.BlockSpec((tm, d), lambda i, j: (i, 0)),
                  pl.BlockSpec((1, d), lambda i, j: (0, 0)),
                  pl.BlockSpec((1, d, tn), lambda i, j: (layer, 0, jnp.minimum(j, MAIN_TILES - 1))),
                  pl.BlockSpec((1, d, tn), lambda i, j: (0, 0, jnp.maximum(j - MAIN_TILES, 0)))],
        out_specs=pl.BlockSpec((tm, tn), lambda i, j: (i, j)),
        out_shape=jax.ShapeDtypeStruct((t, n_tiles * tn), F32),
        scratch_shapes=[pltpu.VMEM((tm, d), BF16)],
        compiler_params=pltpu.CompilerParams(dimension_semantics=("arbitrary", "arbitrary"),
                                             vmem_limit_bytes=IN_VMEM_LIMIT),
        name="inproj",
    )(x2, g, w3, w_diff)


def _s5_tab_kernel(lre_ref, lim_ref, ldt_ref, b_re_ref, b_im_ref, c_re_ref, c_im_ref,
                   wi_ref, ws_ref, wo_ref, ap_ref):
    tc, p, gs = S5_CHUNK, S5_P, S5_GROUP
    w = tc * gs
    lam_re = lre_ref[0, 0]
    lam_im = lim_ref[0, 0]
    dt = jnp.exp(ldt_ref[0, 0])
    lre = lam_re * dt
    lim = lam_im * dt

    def lbpow(e):
        mag = jnp.exp(lre * e)
        return mag * jnp.cos(lim * e), mag * jnp.sin(lim * e)

    pr, pi = lbpow(lax.broadcasted_iota(jnp.int32, (24, 2 * p), 0).astype(F32))
    lb_re, lb_im = pr[1:2], pi[1:2]
    den = lam_re * lam_re + lam_im * lam_im
    f_re = ((lb_re - 1.0) * lam_re + lb_im * lam_im) / den
    f_im = (lb_im * lam_re - (lb_re - 1.0) * lam_im) / den

    r8 = lax.broadcasted_iota(jnp.int32, (8, 2 * p), 0)
    f8 = jnp.where(r8 == 0, f_re, jnp.where(r8 == 1, f_im, 0.0))
    cols = jnp.concatenate([pr, pi, f8, jnp.zeros((72, 2 * p), F32)], axis=0).T
    pr_c, pi_c = cols[:p, 0:tc], cols[:p, 24:24 + tc]
    f_re_c, f_im_c = cols[:p, 48:49], cols[:p, 49:50]

    b_re = b_re_ref[0, 0]
    b_im = b_im_ref[0, 0]
    bb_re = f_re_c * b_re - f_im_c * b_im
    bb_im = f_re_c * b_im + f_im_c * b_re
    hrow = lax.broadcasted_iota(jnp.int32, (gs, w), 0)
    lane_w = lax.broadcasted_iota(jnp.int32, (gs, w), 1)
    tile_m = jnp.where((lane_w & (gs - 1)) == hrow, 1.0, 0.0)
    rev_m = jnp.where(hrow == (tc - 1) - (lane_w >> 4), 1.0, 0.0)
    bbt_re = _dot(bb_re, tile_m, precision=HIGHEST)
    bbt_im = _dot(bb_im, tile_m, precision=HIGHEST)
    p15_re = _dot(pr_c, rev_m, precision=HIGHEST)
    p15_im = _dot(pi_c, rev_m, precision=HIGHEST)
    ws_ref[0, 0] = jnp.concatenate([p15_re * bbt_re - p15_im * bbt_im,
                                    p15_re * bbt_im + p15_im * bbt_re], axis=0).astype(BF16)

    c_re = c_re_ref[0, 0]
    c_im = c_im_ref[0, 0]
    lo = lax.broadcasted_iota(jnp.int32, (1, 2 * p), 1) < p
    row_a = jnp.where(lo, pr, -pi)
    row_b = jnp.where(lo, -pi, -pr)
    wo_ref[0, 0] = jnp.concatenate([c_re * row_a[t + 1:t + 2] + c_im * row_b[t + 1:t + 2]
                                    for t in range(tc)], axis=0).astype(BF16)

    x_re = jnp.concatenate([c_re * pr[k:k + 1] - c_im * pi[k:k + 1] for k in range(tc)], axis=0)
    x_im = jnp.concatenate([c_re * pi[k:k + 1] + c_im * pr[k:k + 1] for k in range(tc)], axis=0)
    zpad = jnp.zeros((p, w), F32)
    kt = (_dot(x_re, jnp.concatenate([bbt_re, zpad], axis=0), precision=HIGHEST)
          - _dot(x_im, jnp.concatenate([bbt_im, zpad], axis=0), precision=HIGHEST))
    sblk = lane_w >> 4
    rows = []
    for t in range(tc):
        acc = jnp.zeros((gs, w), F32)
        for k in range(t + 1):
            acc = jnp.where(sblk == t - k, kt[k * gs:(k + 1) * gs], acc)
        rows.append(acc)
    wi_ref[0, 0] = jnp.concatenate(rows, axis=0).astype(BF16)

    r16 = lax.broadcasted_iota(jnp.int32, (16, 2 * p), 0)
    sr, si = lbpow(lax.shift_left(jnp.full((16, 2 * p), tc, jnp.int32), r16 >> 1).astype(F32))
    ap_ref[0, 0] = jnp.where((r16 & 1) == 1, jnp.where(lo, -si, si), sr)


def _s5_tables(lam_re, lam_im, log_dt, b_re, b_im, c_re, c_im, d_skip):
    depth, g, p = lam_re.shape
    w = S5_CHUNK * S5_GROUP
    dup = lambda a: jnp.concatenate([a.astype(F32), a.astype(F32)], axis=-1)
    ldt = jnp.broadcast_to(log_dt.astype(F32)[:, :, None, None], (depth, g, 1, 2 * p))
    row = pl.BlockSpec((1, 1, 1, 2 * p), lambda l, i: (l, i, 0, 0))
    bspec = pl.BlockSpec((1, 1, p, S5_GROUP), lambda l, i: (l, i, 0, 0))
    cspec = pl.BlockSpec((1, 1, S5_GROUP, 2 * p), lambda l, i: (l, i, 0, 0))
    out = lambda r, c: pl.BlockSpec((1, 1, r, c), lambda l, i: (l, i, 0, 0))
    tabs = pl.pallas_call(
        _s5_tab_kernel,
        grid=(depth, g),
        in_specs=[row, row, row, bspec, bspec, cspec, cspec],
        out_specs=[out(w, w), out(2 * p, w), out(w, 2 * p), out(16, 2 * p)],
        out_shape=[jax.ShapeDtypeStruct((depth, g, w, w), BF16),
                   jax.ShapeDtypeStruct((depth, g, 2 * p, w), BF16),
                   jax.ShapeDtypeStruct((depth, g, w, 2 * p), BF16),
                   jax.ShapeDtypeStruct((depth, g, 16, 2 * p), F32)],
        compiler_params=_cparams(("arbitrary", "arbitrary")),
        name="s5_tables",
    )(dup(lam_re)[:, :, None, :], dup(lam_im)[:, :, None, :], ldt, b_re.astype(F32), b_im.astype(F32),
      dup(c_re), dup(c_im))
    d_row = d_skip.astype(F32).reshape(depth, 1, g * S5_GROUP)
    return (*tabs, d_row)


def _s5_pack_kernel(*refs):
    at_ref = refs[-1]
    nc = at_ref.shape[2]
    for t in range(S5_CHUNK):
        for j, u_ref in enumerate(refs[:-1]):
            at_ref[t, j * LANE:(j + 1) * LANE, :] = u_ref[pl.ds(t, nc, stride=S5_CHUNK), :].T.astype(BF16)


def _u_specs(seq):
    return [pl.BlockSpec((seq, LANE), lambda b, j=j: (b, CB_U + j)) for j in range(D_GROUP // LANE)]


def _s5_pack(proj, bsz, seq):
    nc = seq // S5_CHUNK
    return pl.pallas_call(
        _s5_pack_kernel,
        grid=(bsz,),
        in_specs=_u_specs(seq),
        out_specs=pl.BlockSpec((S5_CHUNK, D_GROUP, nc), lambda b: (0, 0, b)),
        out_shape=jax.ShapeDtypeStruct((S5_CHUNK, D_GROUP, bsz * nc), BF16),
        compiler_params=_cparams(("arbitrary",)),
        name="s5_pack",
    )(*([proj] * (D_GROUP // LANE)))


def _s5_kernel(ut_ref, wi_ref, ws_ref, wo_ref, ap_ref, yt_ref, *, n_chunk):
    tc, gs, r = ut_ref.shape
    ut = ut_ref[...].reshape(tc * gs, r)
    s = _dot(ws_ref[0], ut)
    lane = lax.broadcasted_iota(jnp.int32, s.shape, 1) & (n_chunk - 1)
    apt = ap_ref[0].T
    x = s
    n_step = int(math.log2(n_chunk))
    for k in range(n_step):
        sh = 1 << k
        xs = jnp.where(lane >= sh, pltpu.roll(x, sh, axis=1), 0.0)
        xsw = jnp.concatenate([xs[S5_P:], xs[:S5_P]], axis=0)
        x = x + xs * apt[:, 2 * k:2 * k + 1] + xsw * apt[:, 2 * k + 1:2 * k + 2]
    xp = jnp.where(lane >= 1, pltpu.roll(x, 1, axis=1), 0.0)
    y = _dot(wi_ref[0], ut) + _dot(wo_ref[0], xp.astype(BF16))
    yt_ref[...] = y.reshape(tc, gs, r)


def _s5_core(at, tabs, n_chunk):
    tc, ch, r = at.shape
    w_intra_t, w_state_t, w_out_t, apow, _ = tabs
    g = w_intra_t.shape[0]
    w = tc * S5_GROUP
    return pl.pallas_call(
        functools.partial(_s5_kernel, n_chunk=n_chunk),
        grid=(g,),
        in_specs=[pl.BlockSpec((tc, S5_GROUP, r), lambda i: (0, i, 0)),
                  pl.BlockSpec((1, w, w), lambda i: (i, 0, 0)),
                  pl.BlockSpec((1, 2 * S5_P, w), lambda i: (i, 0, 0)),
                  pl.BlockSpec((1, w, 2 * S5_P), lambda i: (i, 0, 0)),
                  pl.BlockSpec((1, 16, 2 * S5_P), lambda i: (i, 0, 0))],
        out_specs=pl.BlockSpec((tc, S5_GROUP, r), lambda i: (0, i, 0)),
        out_shape=jax.ShapeDtypeStruct((tc, ch, r), F32),
        compiler_params=_cparams(("arbitrary",)),
        name="s5_core",
    )(at, w_intra_t, w_state_t, w_out_t, apow)


def _s5_out_kernel(yt_ref, u0_ref, u1_ref, u2_ref, u3_ref, d_ref, wg_ref, o_ref, y_scr):
    nc = yt_ref.shape[2]
    for t in range(S5_CHUNK):
        rows = pl.ds(t, nc, stride=S5_CHUNK)
        for j, u_ref in enumerate((u0_ref, u1_ref, u2_ref, u3_ref)):
            sl = slice(j * LANE, (j + 1) * LANE)
            y_scr[j, rows, :] = jax.nn.gelu(yt_ref[t, sl, :].T + d_ref[:, sl] * u_ref[rows, :])
    y = jnp.concatenate([y_scr[j] for j in range(D_GROUP // LANE)], axis=1)
    o_ref[...] = (y * jax.nn.sigmoid(_dot(y.astype(BF16), wg_ref[...]))).astype(BF16)


def _s5_out(yt, proj, d_row, w_glu_bf, bsz, seq):
    nc = seq // S5_CHUNK
    return pl.pallas_call(
        _s5_out_kernel,
        grid=(bsz,),
        in_specs=[pl.BlockSpec((S5_CHUNK, D_GROUP, nc), lambda b: (0, 0, b))] + _u_specs(seq)
        + [pl.BlockSpec((1, D_GROUP), lambda b: (0, 0)),
           pl.BlockSpec((D_GROUP, D_GROUP), lambda b: (0, 0))],
        out_specs=pl.BlockSpec((seq, D_GROUP), lambda b: (b, 0)),
        out_shape=jax.ShapeDtypeStruct((bsz * seq, D_GROUP), BF16),
        scratch_shapes=[pltpu.VMEM((D_GROUP // LANE, seq, LANE), F32)],
        compiler_params=_cparams(("arbitrary",)),
        name="s5_out",
    )(yt, proj, proj, proj, proj, d_row, w_glu_bf)


def _s5_mixer(proj, bsz, seq, tabs, w_glu_bf):
    yt = _s5_core(_s5_pack(proj, bsz, seq), tabs, seq // S5_CHUNK)
    return _s5_out(yt, proj, tabs[4], w_glu_bf, bsz, seq)


def _ret_tables():
    h, c = RET_HEADS, RET_CHUNK
    log_g = jnp.log(1.0 - 2.0 ** (-5.0 - jnp.arange(h, dtype=F32)))
    idx = jnp.arange(c, dtype=F32)
    diff = idx[:, None] - idx[None, :]
    dmat = jnp.where(diff >= 0, jnp.exp(jnp.maximum(diff, 0.0)[None] * log_g[:, None, None]), 0.0)
    zeta = jnp.exp((c - 1.0 - idx)[None, :] * log_g[:, None])
    xi = jnp.exp((idx + 1.0)[None, :] * log_g[:, None])
    g_chunk = jnp.exp(c * log_g)
    zeta_b = jnp.broadcast_to(zeta[:, :, None], (h, c, RET_DK))
    xi_b = jnp.broadcast_to(xi[:, :, None], (h, c, RET_DK))
    gch_b = jnp.broadcast_to(g_chunk[:, None, None], (h, 8, RET_DK))
    return dmat, zeta_b, xi_b, gch_b


def _ret_kernel(q_ref, k_ref, v_ref, g_ref, cos_ref, sin_ref, dm_ref, ze_ref, xi_ref, gc_ref,
                o_ref, *, n_chunk):
    c = RET_CHUNK
    dm = dm_ref[0]
    ze = ze_ref[0]
    xi = xi_ref[0]
    gch = gc_ref[0, 0:1, :]

    def body(n, r):
        sl = pl.ds(pl.multiple_of(n * c, c), c)
        cos = cos_ref[sl, :]
        sin = sin_ref[sl, :]
        qc = _rope128(q_ref[sl, :], cos, sin)
        kc = _rope128(k_ref[sl, :], cos, sin) * (RET_DK ** -0.5)
        vc = v_ref[sl, :].astype(BF16)
        inner = _dot_nt(qc.astype(BF16), kc.astype(BF16)) * dm
        o = _dot(inner.astype(BF16), vc) + _dot((qc * xi).astype(BF16), r.astype(BF16))
        kv = _dot_tn((kc * ze).astype(BF16), vc)
        r = r * gch + kv
        mu = jnp.mean(o, axis=-1, keepdims=True)
        oc = o - mu
        var = jnp.mean(oc * oc, axis=-1, keepdims=True)
        on = oc * lax.rsqrt(var + EPS)
        gg = g_ref[sl, :]
        o_ref[sl, :] = (gg * jax.nn.sigmoid(gg) * on).astype(BF16)
        return r

    lax.fori_loop(0, n_chunk, body, jnp.zeros((RET_DK, RET_DK), F32), unroll=4)


def _retention(proj, cos128, sin128, tabs, bsz, seq):
    dmat, zeta_b, xi_b, gch_b = tabs
    h = RET_HEADS
    col = lambda cb: pl.BlockSpec((seq, LANE), lambda b, hh, cb=cb: (b, cb + hh))
    tab = pl.BlockSpec((seq, LANE), lambda b, hh: (b, 0))
    head = lambda r: pl.BlockSpec((1, r, LANE), lambda b, hh: (hh, 0, 0))
    return pl.pallas_call(
        functools.partial(_ret_kernel, n_chunk=seq // RET_CHUNK),
        grid=(bsz, h),
        in_specs=[col(CB_RQ), col(CB_RK), col(CB_RV), col(CB_RG), tab, tab,
                  head(RET_CHUNK), head(RET_CHUNK), head(RET_CHUNK), head(8)],
        out_specs=pl.BlockSpec((seq, LANE), lambda b, hh: (b, hh)),
        out_shape=jax.ShapeDtypeStruct((bsz * seq, D_GROUP), BF16),
        compiler_params=_cparams(("arbitrary", "arbitrary")),
        name="retention",
    )(proj, proj, proj, proj, cos128, sin128, dmat, zeta_b, xi_b, gch_b)


def _nsa_prep_kernel(kc_ref, vc_ref, pek_ref, pev_ref, wk_ref, wv_ref, ks_ref, kw_ref, cos_ref, sin_ref,
                     q_ref, g_ref, kcmp_ref, vcmp_ref, ksr_ref, kwr_ref, qt_ref, gt_ref):
    half = NSA_CMP_STRIDE * NSA_DH
    for h in range(NSA_HEADS):
        sl = slice(h * NSA_DH, (h + 1) * NSA_DH)
        qt_ref[sl, :] = _rope128(_rms_rows(q_ref[:, sl]), cos_ref[...], sin_ref[...]).T.astype(BF16)
    gt_ref[...] = jax.nn.sigmoid(g_ref[...]).T
    n16 = kcmp_ref.shape[1]
    kparts, vparts = [], []
    for i in range(NSA_CMP_STRIDE):
        rows = pl.ds(i, n16, stride=NSA_CMP_STRIDE)
        kparts.append(_rope128(kc_ref[rows, :], cos_ref[rows, :], sin_ref[rows, :]))
        vparts.append(vc_ref[rows, :])
    xr = jnp.concatenate(kparts, axis=1)
    xv = jnp.concatenate(vparts, axis=1)
    a = _dot((xr + pek_ref[0:1, :]).astype(BF16), wk_ref[0:half, :])
    b = _dot((xr + pek_ref[1:2, :]).astype(BF16), wk_ref[half:2 * half, :])
    kcm = _rms_rows(a + pltpu.roll(b, n16 - 1, axis=0))
    rowi = lax.broadcasted_iota(jnp.int32, kcm.shape, 0)
    kcmp_ref[0] = jnp.where(rowi < n16 - 1, kcm, 0.0).astype(BF16)
    av = _dot((xv + pev_ref[0:1, :]).astype(BF16), wv_ref[0:half, :])
    bv = _dot((xv + pev_ref[1:2, :]).astype(BF16), wv_ref[half:2 * half, :])
    vcm = av + pltpu.roll(bv, n16 - 1, axis=0)
    vcmp_ref[0] = jnp.where(rowi < n16 - 1, vcm, 0.0).astype(BF16)
    cos = cos_ref[...]
    sin = sin_ref[...]
    ksr_ref[...] = _rope128(_rms_rows(ks_ref[...]), cos, sin).astype(BF16)
    kwr_ref[...] = _rope128(_rms_rows(kw_ref[...]), cos, sin).astype(BF16)


def _nsa_prep(proj, cos128, sin128, pe_k, pe_v, wk_bf, wv_bf, bsz, seq):
    n16 = seq // NSA_CMP_STRIDE
    half = NSA_CMP_STRIDE * NSA_DH
    pek = pe_k.astype(F32).reshape(2, half)
    pev = pe_v.astype(F32).reshape(2, half)
    full = lambda a: pl.BlockSpec(a.shape, lambda b: (0,) * a.ndim)
    col = lambda cb: pl.BlockSpec((seq, LANE), lambda b, cb=cb: (b, cb))
    tab = pl.BlockSpec((seq, LANE), lambda b: (b, 0))
    cmp_spec = pl.BlockSpec((1, n16, NSA_DH), lambda b: (b, 0, 0))
    seq_spec = pl.BlockSpec((seq, LANE), lambda b: (b, 0))
    return pl.pallas_call(
        _nsa_prep_kernel,
        grid=(bsz,),
        in_specs=[col(CB_NKC), col(CB_NVC), full(pek), full(pev), full(wk_bf), full(wv_bf),
                  col(CB_NKS), col(CB_NKW), tab, tab,
                  pl.BlockSpec((seq, D_GROUP), lambda b: (b, CB_NQ // 4)), col(CB_GATE)],
        out_specs=[cmp_spec, cmp_spec, seq_spec, seq_spec,
                   pl.BlockSpec((D_GROUP, seq), lambda b: (0, b)), pl.BlockSpec((LANE, seq), lambda b: (0, b))],
        out_shape=[jax.ShapeDtypeStruct((bsz, n16, NSA_DH), BF16),
                   jax.ShapeDtypeStruct((bsz, n16, NSA_DH), BF16),
                   jax.ShapeDtypeStruct((bsz * seq, NSA_DH), BF16),
                   jax.ShapeDtypeStruct((bsz * seq, NSA_DH), BF16),
                   jax.ShapeDtypeStruct((D_GROUP, bsz * seq), BF16),
                   jax.ShapeDtypeStruct((LANE, bsz * seq), F32)],
        compiler_params=_cparams(("arbitrary",)),
        name="nsa_prep",
    )(proj, proj, pek, pev, wk_bf, wv_bf, proj, proj, cos128, sin128, proj, proj)


def _attn_finish(l, acc):
    return jnp.where(l > 0.0, acc / jnp.where(l > 0.0, l, 1.0), 0.0)


def _nsa_kernel(qt_ref, gt_ref, kc_ref, vc_ref, ks_ref, vs_ref, kw_ref, vw_ref, ov_ref, o_ref):
    hh, qb, dh = NSA_HEADS, Q_BLOCK, NSA_DH
    scale = dh ** -0.5
    n = pl.program_id(1)
    qq = jnp.concatenate([qt_ref[h * dh:(h + 1) * dh, :] for h in range(hh)], axis=1)

    def branch(keys, vals, maskf):
        s = _dot(keys, qq) * scale
        e = jnp.where(jnp.concatenate([maskf] * hh, axis=1) > 0.5, jnp.exp(s), 0.0)
        return jnp.sum(e, axis=0, keepdims=True), _dot_tn(vals, e.astype(BF16)), e

    def qpos(rows):
        return n * qb + lax.broadcasted_iota(jnp.int32, (rows, qb), 1)

    def krow(rows):
        return lax.broadcasted_iota(jnp.int32, (rows, qb), 0)

    ncp = kc_ref.shape[1]
    vis = jnp.where(krow(ncp) * NSA_CMP_STRIDE + (NSA_CMP_LEN - 1) <= qpos(ncp), 1.0, 0.0)
    l_c, acc_c, e_c = branch(kc_ref[0], vc_ref[0], vis)
    o_cmp = _attn_finish(l_c, acc_c)
    p = e_c / jnp.where(l_c > 0.0, l_c, 1.0)
    psum = p[:, 0:qb]
    for h in range(1, hh):
        psum = psum + p[:, h * qb:(h + 1) * qb]

    n_sel = ks_ref.shape[0] // NSA_SEL_LEN
    imp = _dot(ov_ref[...], psum, precision=HIGHEST)[:n_sel]
    j = krow(n_sel)
    cur = qpos(n_sel) >> NSA_SEL_SHIFT
    forced = (j == 0) | (j == cur) | (j == cur - 1)
    score = jnp.where(j <= cur, imp + jnp.where(forced, NSA_FORCE_BONUS, 0.0), NEG_INF)
    rank = jnp.zeros((n_sel, qb), F32)
    for jp in range(n_sel):
        cj = score[jp:jp + 1, :]
        tie = jnp.where(j > jp, 1.0, 0.0)
        rank = rank + jnp.where(cj > score, 1.0, jnp.where(cj == score, tie, 0.0))
    selm = jnp.where(rank < float(min(NSA_TOPK, n_sel)), 1.0, 0.0)
    selm = jnp.concatenate([selm, jnp.zeros((LANE - n_sel, qb), F32)], axis=0).astype(BF16)

    tk = NSA_SEL_TK
    blk = lax.broadcasted_iota(jnp.int32, (tk, LANE), 0) >> NSA_SEL_SHIFT
    selcol = lax.broadcasted_iota(jnp.int32, (tk, LANE), 1)

    def sel_tile(kt):
        sl = pl.ds(pl.multiple_of(kt * tk, tk), tk)
        expand = jnp.where(selcol == kt * (tk // NSA_SEL_LEN) + blk, 1.0, 0.0).astype(BF16)
        picked = _dot(expand, selm)
        maskf = jnp.where(kt * tk + krow(tk) <= qpos(tk), picked, 0.0)
        l, acc, _ = branch(ks_ref[sl, :], vs_ref[sl, :].astype(BF16), maskf)
        return l, acc

    def sel_body(i, carry):
        l0, a0 = sel_tile(2 * i)
        l1, a1 = sel_tile(2 * i + 1)
        return carry[0] + (l0 + l1), carry[1] + (a0 + a1)

    zero = (jnp.zeros((1, hh * qb), F32), jnp.zeros((dh, hh * qb), F32))
    trips = (n * qb + qb + 2 * tk - 1) // (2 * tk)
    o_sel = _attn_finish(*lax.fori_loop(0, trips, sel_body, zero))

    wk = NSA_WINDOW + qb
    start = jnp.maximum(n * qb - NSA_WINDOW, 0)
    sl = pl.ds(pl.multiple_of(start, qb), wk)
    dist = qpos(wk) - (start + krow(wk))
    inwin = jnp.where(dist >= 0, jnp.where(dist < NSA_WINDOW, 1.0, 0.0), 0.0)
    l_w, acc_w, _ = branch(kw_ref[sl, :], vw_ref[sl, :].astype(BF16), inwin)
    o_win = _attn_finish(l_w, acc_w)

    gt = gt_ref[...]
    for h in range(hh):
        hq = slice(h * qb, (h + 1) * qb)
        o = (gt[3 * h:3 * h + 1, :] * o_cmp[:, hq] + gt[3 * h + 1:3 * h + 2, :] * o_sel[:, hq]
             + gt[3 * h + 2:3 * h + 3, :] * o_win[:, hq])
        o_ref[:, h * dh:(h + 1) * dh] = o.T.astype(BF16)


def _nsa_overlap_table(seq):
    n_cmp = (seq - NSA_CMP_LEN) // NSA_CMP_STRIDE + 1
    n_sel = seq // NSA_SEL_LEN
    cmp_start = np.arange(n_cmp) * NSA_CMP_STRIDE
    sel_start = np.arange(n_sel) * NSA_SEL_LEN
    overlap = ((cmp_start[None, :] < sel_start[:, None] + NSA_SEL_LEN)
               & (cmp_start[None, :] + NSA_CMP_LEN > sel_start[:, None])).astype(np.float32)
    tab = np.zeros((LANE, n_cmp + 1), np.float32)
    tab[:n_sel, :n_cmp] = overlap
    return jnp.asarray(tab)


def _nsa_attn(proj, qt, gt, kcmp, vcmp, ksr, kwr, bsz, seq):
    nb = seq // Q_BLOCK
    cmp_spec = pl.BlockSpec((1, kcmp.shape[1], NSA_DH), lambda b, n: (b, 0, 0))
    kseq = pl.BlockSpec((seq, LANE), lambda b, n: (b, 0))
    vcol = lambda cb: pl.BlockSpec((seq, LANE), lambda b, n, cb=cb: (b, cb))
    ov = _nsa_overlap_table(seq)
    return pl.pallas_call(
        _nsa_kernel,
        grid=(bsz, nb),
        in_specs=[pl.BlockSpec((D_GROUP, Q_BLOCK), lambda b, n: (0, b * nb + n)),
                  pl.BlockSpec((LANE, Q_BLOCK), lambda b, n: (0, b * nb + n)),
                  cmp_spec, cmp_spec, kseq, vcol(CB_NVS), kseq, vcol(CB_NVW),
                  pl.BlockSpec(ov.shape, lambda b, n: (0, 0))],
        out_specs=pl.BlockSpec((Q_BLOCK, D_GROUP), lambda b, n: (b * nb + n, 0)),
        out_shape=jax.ShapeDtypeStruct((bsz * seq, D_GROUP), BF16),
        compiler_params=_cparams(("arbitrary", "arbitrary")),
        name="nsa_attn",
    )(qt, gt, kcmp, vcmp, ksr, proj, kwr, proj, ov)


def _diff_prep_kernel(q_ref, k_ref, cos_ref, sin_ref, qt_ref, ko_ref):
    cos = cos_ref[...]
    sin = sin_ref[...]
    lane = lax.broadcasted_iota(jnp.int32, cos.shape, 1)
    lo = lane < DIFF_DH
    first = (lane & (DIFF_DH - 1)) < DIFF_DH // 2

    def prep(x):
        x2 = x * x
        ss_lo = jnp.sum(jnp.where(lo, x2, 0.0), axis=-1, keepdims=True)
        ss_hi = jnp.sum(jnp.where(lo, 0.0, x2), axis=-1, keepdims=True)
        ms = jnp.where(lo, ss_lo, ss_hi) * (1.0 / DIFF_DH)
        xn = x * lax.rsqrt(ms + EPS)
        partner = jnp.where(first, pltpu.roll(xn, LANE - DIFF_DH // 2, axis=1),
                            pltpu.roll(xn, DIFF_DH // 2, axis=1))
        return xn * cos + partner * sin

    for h in range(DIFF_HEADS):
        sl = slice(h * LANE, (h + 1) * LANE)
        qt_ref[sl, :] = (prep(q_ref[:, sl]) * (DIFF_DH ** -0.5)).T.astype(BF16)
        ko_ref[:, sl] = prep(k_ref[:, sl]).astype(BF16)


def _diff_prep(proj, cos64, sin64, tm):
    t = proj.shape[0]
    blk = lambda cb: pl.BlockSpec((tm, D_GROUP), lambda i, cb=cb: (i, cb // 4))
    tab = pl.BlockSpec((tm, LANE), lambda i: (i, 0))
    return pl.pallas_call(
        _diff_prep_kernel,
        grid=(t // tm,),
        in_specs=[blk(CB_DQ), blk(CB_DK), tab, tab],
        out_specs=[pl.BlockSpec((D_GROUP, tm), lambda i: (0, i)),
                   pl.BlockSpec((tm, D_GROUP), lambda i: (i, 0))],
        out_shape=[jax.ShapeDtypeStruct((D_GROUP, t), BF16), jax.ShapeDtypeStruct((t, D_GROUP), BF16)],
        compiler_params=_cparams(("arbitrary",)),
        name="diff_prep",
    )(proj, proj, cos64, sin64)


def _diff_kernel(lam_ref, qt_ref, k_ref, v_ref, o_ref, acc_ref, l_ref, *, out_scale):
    nq = DIFF_TQ
    qi = pl.program_id(2)
    qt = qt_ref[...]
    row = lax.broadcasted_iota(jnp.int32, qt.shape, 0)
    zero = jnp.zeros_like(qt)
    qq = jnp.concatenate([jnp.where(row < DIFF_DH, qt, zero), jnp.where(row < DIFF_DH, zero, qt)], axis=1)
    acc_ref[...] = jnp.zeros_like(acc_ref)
    l_ref[...] = jnp.zeros_like(l_ref)

    tk = DIFF_TK
    per_q = nq // tk

    def tile(kt, masked):
        sl = pl.ds(pl.multiple_of(kt * tk, tk), tk)
        p = jnp.exp(_dot(k_ref[sl, :], qq))
        if masked:
            kpos = kt * tk + lax.broadcasted_iota(jnp.int32, p.shape, 0)
            qpos = qi * nq + (lax.broadcasted_iota(jnp.int32, p.shape, 1) & (nq - 1))
            p = jnp.where(kpos <= qpos, p, 0.0)
        return jnp.sum(p, axis=0, keepdims=True), _dot_tn(v_ref[sl, :].astype(BF16), p.astype(BF16))

    def body(i, c):
        l0, a0 = tile(2 * i, False)
        l1, a1 = tile(2 * i + 1, False)
        l_ref[...] += l0 + l1
        acc_ref[...] += a0 + a1
        return c

    lax.fori_loop(0, qi * (per_q // 2), body, 0)
    for d in range(per_q):
        l0, a0 = tile(qi * per_q + d, True)
        l_ref[...] += l0
        acc_ref[...] += a0
    l = l_ref[...]
    acc = acc_ref[...]
    o = acc[:, :nq] / l[:, :nq] - lam_ref[0, 0] * (acc[:, nq:] / l[:, nq:])
    on = o * lax.rsqrt(jnp.mean(o * o, axis=0, keepdims=True) + EPS) * out_scale
    o_ref[...] = on.T.astype(BF16)


def _diff_attn(lam, qt, kd, proj, bsz, seq, lam_init):
    nq = DIFF_TQ
    nb = seq // nq
    h = DIFF_HEADS
    return pl.pallas_call(
        functools.partial(_diff_kernel, out_scale=1.0 - lam_init),
        grid=(bsz, h, nb),
        in_specs=[pl.BlockSpec(memory_space=pltpu.SMEM),
                  pl.BlockSpec((LANE, nq), lambda b, hh, n: (hh, b * nb + n)),
                  pl.BlockSpec((seq, LANE), lambda b, hh, n: (b, hh)),
                  pl.BlockSpec((seq, LANE), lambda b, hh, n: (b, CB_DV + hh))],
        out_specs=pl.BlockSpec((nq, LANE), lambda b, hh, n: (b * nb + n, hh)),
        out_shape=jax.ShapeDtypeStruct((bsz * seq, D_GROUP), BF16),
        scratch_shapes=[pltpu.VMEM((LANE, 2 * nq), F32), pltpu.VMEM((1, 2 * nq), F32)],
        compiler_params=_cparams(("arbitrary", "arbitrary", "arbitrary")),
        name="diff_attn",
    )(lam, qt, kd, proj)


def _outproj_kernel(ya_ref, yb_ref, yc_ref, yd_ref, w_ref, x_ref, g_ref, xo_ref, h_ref):
    mixed = jnp.concatenate([ya_ref[...], yb_ref[...], yc_ref[...], yd_ref[...]], axis=1)
    x = x_ref[...] + _dot(mixed, w_ref[0])
    xo_ref[...] = x
    h_ref[...] = (_rms_rows(x) * g_ref[...]).astype(BF16)


def _outproj(ys, w3_bf, layer, x2, g, tm):
    t, d = x2.shape
    yspec = pl.BlockSpec((tm, D_GROUP), lambda i: (i, 0))
    row = pl.BlockSpec((tm, d), lambda i: (i, 0))
    return pl.pallas_call(
        _outproj_kernel,
        grid=(t // tm,),
        in_specs=[yspec, yspec, yspec, yspec,
                  pl.BlockSpec((1, d, d), lambda i: (layer, 0, 0)), row,
                  pl.BlockSpec((1, d), lambda i: (0, 0))],
        out_specs=[row, row],
        out_shape=[jax.ShapeDtypeStruct((t, d), F32), jax.ShapeDtypeStruct((t, d), BF16)],
        compiler_params=_cparams(("arbitrary",)),
        name="outproj",
    )(*ys, w3_bf, x2, g)


def _mlp_kernel(h_ref, x_ref, w1_ref, w2_ref, o_ref):
    @pl.when(pl.program_id(1) == 0)
    def _():
        o_ref[...] = x_ref[...]

    a = _dot(h_ref[...], w1_ref[0].astype(BF16))
    a = jnp.square(jnp.maximum(a, 0.0)).astype(BF16)
    o_ref[...] += _dot(a, w2_ref[0].astype(BF16))


def _mlp(h2, x2, w1, w2, layer, tm, tf):
    t, d = x2.shape
    f = w1.shape[2]
    return pl.pallas_call(
        _mlp_kernel,
        grid=(t // tm, f // tf),
        in_specs=[pl.BlockSpec((tm, d), lambda i, j: (i, 0)),
                  pl.BlockSpec((tm, d), lambda i, j: (i, 0), pipeline_mode=pl.Buffered(1)),
                  pl.BlockSpec((1, d, tf), lambda i, j: (layer, 0, j)),
                  pl.BlockSpec((1, tf, d), lambda i, j: (layer, j, 0))],
        out_specs=pl.BlockSpec((tm, d), lambda i, j: (i, 0)),
        out_shape=jax.ShapeDtypeStruct((t, d), F32),
        compiler_params=_cparams(("arbitrary", "arbitrary")),
        name="mlp",
    )(h2, x2, w1, w2)


def _rope_tables(positions):
    bsz, seq = positions.shape
    pos = positions.astype(F32)[..., None]

    def tab(d):
        inv = ROPE_THETA ** (-jnp.arange(0, d, 2, dtype=F32) / d)
        ang = pos * inv
        return jnp.cos(ang), jnp.sin(ang)

    c, s = tab(NSA_DH)
    cos128 = jnp.concatenate([c, c], axis=-1).reshape(bsz * seq, LANE)
    sin128 = jnp.concatenate([-s, s], axis=-1).reshape(bsz * seq, LANE)
    c, s = tab(DIFF_DH)
    cos64 = jnp.concatenate([c, c, c, c], axis=-1).reshape(bsz * seq, LANE)
    sin64 = jnp.concatenate([-s, s, -s, s], axis=-1).reshape(bsz * seq, LANE)
    return cos128, sin128, cos64, sin64


def kernel(x, positions, norm1_g, w_in, s5_lambda_re, s5_lambda_im, s5_log_dt, s5_b_re, s5_b_im, s5_c_re, s5_c_im, s5_d, s5_w_glu, nsa_pe_k, nsa_pe_v, nsa_w_cmp_k, nsa_w_cmp_v, diff_lq1, diff_lk1, diff_lq2, diff_lk2, w_out, norm2_g, mlp_w1, mlp_w2):
    bsz, seq, d = x.shape
    t = bsz * seq
    depth = w_in.shape[0]
    tm = min(1024, t)
    cos128, sin128, cos64, sin64 = _rope_tables(positions)
    ret_tabs = _ret_tables()
    x2 = x.reshape(t, d).astype(F32)
    w_in = w_in.astype(BF16)
    w_out_bf = w_out.astype(BF16)
    w_glu_bf = s5_w_glu.astype(BF16)
    wk_bf = nsa_w_cmp_k.astype(BF16)
    wv_bf = nsa_w_cmp_v.astype(BF16)
    s5_tabs = _s5_tables(s5_lambda_re, s5_lambda_im, s5_log_dt, s5_b_re, s5_b_im, s5_c_re, s5_c_im, s5_d)
    for layer in range(depth):
        g1 = norm1_g[layer].astype(F32)[None, :]
        proj = _inproj(x2, g1, w_in, w_in[layer:layer + 1, :, GATE_RAW_END:], layer, min(IN_TM, t), IN_TN)
        proj_d = proj

        y_a = _s5_mixer(proj, bsz, seq, tuple(a[layer] for a in s5_tabs), w_glu_bf[layer])

        y_b = _retention(proj, cos128, sin128, ret_tabs, bsz, seq)

        kcmp, vcmp, ksr, kwr, nqt, ngt = _nsa_prep(proj, cos128, sin128, nsa_pe_k[layer], nsa_pe_v[layer],
                                                   wk_bf[layer], wv_bf[layer], bsz, seq)
        y_c = _nsa_attn(proj, nqt, ngt, kcmp, vcmp, ksr, kwr, bsz, seq)

        lam_init = 0.8 - 0.6 * math.exp(-0.3 * layer)
        lam = (jnp.exp(jnp.sum(diff_lq1[layer].astype(F32) * diff_lk1[layer].astype(F32)))
               - jnp.exp(jnp.sum(diff_lq2[layer].astype(F32) * diff_lk2[layer].astype(F32))) + lam_init)
        qt, kd = _diff_prep(proj_d, cos64, sin64, tm)
        y_d = _diff_attn(lam.reshape(1, 1).astype(F32), qt, kd, proj_d, bsz, seq, lam_init)

        x2, h2 = _outproj((y_a, y_b, y_c, y_d), w_out_bf, layer, x2,
                          norm2_g[layer].astype(F32)[None, :], min(512, t))
        x2 = _mlp(h2, x2, mlp_w1.astype(F32), mlp_w2.astype(F32), layer, tm, 512)
    return x2.reshape(bsz, seq, d).astype(x.dtype)
```

```python
import functools
import math

import jax
import jax.numpy as jnp
import numpy as np
from jax import lax
from jax.experimental import pallas as pl
from jax.experimental.pallas import tpu as pltpu

F32 = jnp.float32
BF16 = jnp.bfloat16
HIGHEST = lax.Precision.HIGHEST

D_MODEL = 2048
D_GROUP = 512
D_FF = 4 * D_MODEL
ROPE_THETA = 10000.0
EPS = 1e-6
Q_BLOCK = 128
NEG_INF = -1e30

S5_GROUP = 16
S5_G = 32
S5_P = 64
S5_CHUNK = 16
S5_CORE_GROUPS = 2
S5_TAB_GROUPS = 4

RET_HEADS = 4
RET_DK = 128
RET_CHUNK = 128

NSA_HEADS = 4
NSA_DH = 128
NSA_CMP_LEN = 32
NSA_CMP_STRIDE = 16
NSA_SEL_LEN = 64
NSA_TOPK = 8
NSA_WINDOW = 256
NSA_FORCE_BONUS = 1e4
NSA_SEL_SHIFT = 6
NSA_SEL_TK = 256

DIFF_HEADS = 4
DIFF_DH = 64
DIFF_TQ = 512
DIFF_TK = 256

LANE = 128
GATE_RAW_END = 3852
IN_TN = 512
MAIN_TILES = 8
CB_U, CB_RQ, CB_RK, CB_RV, CB_RG, CB_NQ = 0, 4, 8, 12, 16, 20
CB_NKC, CB_NVC, CB_NKS, CB_NVS, CB_NKW, CB_NVW, CB_GATE = 24, 25, 26, 27, 28, 29, 30
CB_DQ, CB_DK, CB_DV = 32, 36, 40

VMEM_LIMIT = 56 * 1024 * 1024
IN_TM = 2048
IN_VMEM_LIMIT = 60 * 1024 * 1024


def _cparams(sem):
    return pltpu.CompilerParams(dimension_semantics=sem, vmem_limit_bytes=VMEM_LIMIT)


def _dot(a, b, **kw):
    return jnp.dot(a, b, preferred_element_type=F32, **kw)


def _dot_nt(a, b):
    return lax.dot_general(a, b, (((1,), (1,)), ((), ())), preferred_element_type=F32)


def _dot_tn(a, b):
    return lax.dot_general(a, b, (((0,), (0,)), ((), ())), preferred_element_type=F32)


def _rms_rows(x):
    return x * lax.rsqrt(jnp.mean(x * x, axis=-1, keepdims=True) + EPS)


def _rope128(x, cos, sin_signed):
    return x * cos + pltpu.roll(x, 64, axis=1) * sin_signed


def _inproj_kernel(x_ref, g_ref, wm_ref, wd_ref, o_ref, xn_ref):
    j = pl.program_id(1)

    @pl.when(j == 0)
    def _():
        x = x_ref[...]
        xn_ref[...] = (_rms_rows(x) * g_ref[...]).astype(BF16)

    @pl.when(j < MAIN_TILES)
    def _():
        o_ref[...] = _dot(xn_ref[...], wm_ref[0])

    @pl.when(j >= MAIN_TILES)
    def _():
        o_ref[...] = _dot(xn_ref[...], wd_ref[0])


def _inproj(x2, g, w3, w_diff, layer, tm, tn):
    t, d = x2.shape
    n_tiles = MAIN_TILES + w_diff.shape[2] // tn
    return pl.pallas_call(
        _inproj_kernel,
        grid=(t // tm, n_tiles),
        in_specs=[pl.BlockSpec((tm, d), lambda i, j: (i, 0)),
                  pl.BlockSpec((1, d), lambda i, j: (0, 0)),
                  pl.BlockSpec((1, d, tn), lambda i, j: (layer, 0, jnp.minimum(j, MAIN_TILES - 1))),
                  pl.BlockSpec((1, d, tn), lambda i, j: (0, 0, jnp.maximum(j - MAIN_TILES, 0)))],
        out_specs=pl.BlockSpec((tm, tn), lambda i, j: (i, j)),
        out_shape=jax.ShapeDtypeStruct((t, n_tiles * tn), F32),
        scratch_shapes=[pltpu.VMEM((tm, d), BF16)],
        compiler_params=pltpu.CompilerParams(dimension_semantics=("arbitrary", "arbitrary"),
                                             vmem_limit_bytes=IN_VMEM_LIMIT),
        name="inproj",
    )(x2, g, w3, w_diff)


def _s5_tab_kernel(*refs):
    for a in range(S5_TAB_GROUPS):
        _s5_tab_group(*[r.at[0, a] for r in refs])


def _s5_tab_group(lre_ref, lim_ref, ldt_ref, b_re_ref, b_im_ref, c_re_ref, c_im_ref,
                  wi_ref, ws_ref, wo_ref, ap_ref):
    tc, p, gs = S5_CHUNK, S5_P, S5_GROUP
    w = tc * gs
    lam_re = lre_ref[...]
    lam_im = lim_ref[...]
    dt = jnp.exp(ldt_ref[...])
    lre = lam_re * dt
    lim = lam_im * dt

    def lbpow(e):
        mag = jnp.exp(lre * e)
        return mag * jnp.cos(lim * e), mag * jnp.sin(lim * e)

    pr, pi = lbpow(lax.broadcasted_iota(jnp.int32, (24, 2 * p), 0).astype(F32))
    lb_re, lb_im = pr[1:2], pi[1:2]
    den = lam_re * lam_re + lam_im * lam_im
    f_re = ((lb_re - 1.0) * lam_re + lb_im * lam_im) / den
    f_im = (lb_im * lam_re - (lb_re - 1.0) * lam_im) / den

    r8 = lax.broadcasted_iota(jnp.int32, (8, 2 * p), 0)
    f8 = jnp.where(r8 == 0, f_re, jnp.where(r8 == 1, f_im, 0.0))
    cols = jnp.concatenate([pr, pi, f8, jnp.zeros((72, 2 * p), F32)], axis=0).T
    pr_c, pi_c = cols[:p, 0:tc], cols[:p, 24:24 + tc]
    f_re_c, f_im_c = cols[:p, 48:49], cols[:p, 49:50]

    b_re = b_re_ref[...]
    b_im = b_im_ref[...]
    bb_re = f_re_c * b_re - f_im_c * b_im
    bb_im = f_re_c * b_im + f_im_c * b_re
    hrow = lax.broadcasted_iota(jnp.int32, (gs, w), 0)
    lane_w = lax.broadcasted_iota(jnp.int32, (gs, w), 1)
    tile_m = jnp.where((lane_w & (gs - 1)) == hrow, 1.0, 0.0)
    rev_m = jnp.where(hrow == (tc - 1) - (lane_w >> 4), 1.0, 0.0)
    bbt_re = _dot(bb_re, tile_m, precision=HIGHEST)
    bbt_im = _dot(bb_im, tile_m, precision=HIGHEST)
    p15_re = _dot(pr_c, rev_m, precision=HIGHEST)
    p15_im = _dot(pi_c, rev_m, precision=HIGHEST)
    ws_ref[...] = jnp.concatenate([p15_re * bbt_re - p15_im * bbt_im,
                                    p15_re * bbt_im + p15_im * bbt_re], axis=0).astype(BF16)

    c_re = c_re_ref[...]
    c_im = c_im_ref[...]
    lo = lax.broadcasted_iota(jnp.int32, (1, 2 * p), 1) < p
    row_a = jnp.where(lo, pr, -pi)
    row_b = jnp.where(lo, -pi, -pr)
    wo_ref[...] = jnp.concatenate([c_re * row_a[t + 1:t + 2] + c_im * row_b[t + 1:t + 2]
                                    for t in range(tc)], axis=0).astype(BF16)

    x_re = jnp.concatenate([c_re * pr[k:k + 1] - c_im * pi[k:k + 1] for k in range(tc)], axis=0)
    x_im = jnp.concatenate([c_re * pi[k:k + 1] + c_im * pr[k:k + 1] for k in range(tc)], axis=0)
    zpad = jnp.zeros((p, w), F32)
    kt = (_dot(x_re, jnp.concatenate([bbt_re, zpad], axis=0), precision=HIGHEST)
          - _dot(x_im, jnp.concatenate([bbt_im, zpad], axis=0), precision=HIGHEST))
    sblk = lane_w >> 4
    rows = []
    for t in range(tc):
        acc = jnp.zeros((gs, w), F32)
        for k in range(t + 1):
            acc = jnp.where(sblk == t - k, kt[k * gs:(k + 1) * gs], acc)
        rows.append(acc)
    wi_ref[...] = jnp.concatenate(rows, axis=0).astype(BF16)

    r16 = lax.broadcasted_iota(jnp.int32, (16, 2 * p), 0)
    sr, si = lbpow(lax.shift_left(jnp.full((16, 2 * p), tc, jnp.int32), r16 >> 1).astype(F32))
    ap_ref[...] = jnp.where((r16 & 1) == 1, jnp.where(lo, -si, si), sr)


def _s5_tables(lam_re, lam_im, log_dt, b_re, b_im, c_re, c_im, d_skip):
    depth, g, p = lam_re.shape
    w = S5_CHUNK * S5_GROUP
    dup = lambda a: jnp.concatenate([a.astype(F32), a.astype(F32)], axis=-1)
    ldt = jnp.broadcast_to(log_dt.astype(F32)[:, :, None, None], (depth, g, 1, 2 * p))
    ng = S5_TAB_GROUPS
    row = pl.BlockSpec((1, ng, 1, 2 * p), lambda l, i: (l, i, 0, 0))
    bspec = pl.BlockSpec((1, ng, p, S5_GROUP), lambda l, i: (l, i, 0, 0))
    cspec = pl.BlockSpec((1, ng, S5_GROUP, 2 * p), lambda l, i: (l, i, 0, 0))
    out = lambda r, c: pl.BlockSpec((1, ng, r, c), lambda l, i: (l, i, 0, 0))
    tabs = pl.pallas_call(
        _s5_tab_kernel,
        grid=(depth, g // ng),
        in_specs=[row, row, row, bspec, bspec, cspec, cspec],
        out_specs=[out(w, w), out(2 * p, w), out(w, 2 * p), out(16, 2 * p)],
        out_shape=[jax.ShapeDtypeStruct((depth, g, w, w), BF16),
                   jax.ShapeDtypeStruct((depth, g, 2 * p, w), BF16),
                   jax.ShapeDtypeStruct((depth, g, w, 2 * p), BF16),
                   jax.ShapeDtypeStruct((depth, g, 16, 2 * p), F32)],
        compiler_params=_cparams(("arbitrary", "arbitrary")),
        name="s5_tables",
    )(dup(lam_re)[:, :, None, :], dup(lam_im)[:, :, None, :], ldt, b_re.astype(F32), b_im.astype(F32),
      dup(c_re), dup(c_im))
    d_row = d_skip.astype(F32).reshape(depth, 1, g * S5_GROUP)
    return (*tabs, d_row)


def _s5_pack_kernel(*refs):
    at_ref = refs[-1]
    nc = at_ref.shape[2]
    for t in range(S5_CHUNK):
        for j, u_ref in enumerate(refs[:-1]):
            at_ref[t, j * LANE:(j + 1) * LANE, :] = u_ref[pl.ds(t, nc, stride=S5_CHUNK), :].T.astype(BF16)


def _u_specs(seq):
    return [pl.BlockSpec((seq, LANE), lambda b, j=j: (b, CB_U + j)) for j in range(D_GROUP // LANE)]


def _s5_pack(proj, bsz, seq):
    nc = seq // S5_CHUNK
    return pl.pallas_call(
        _s5_pack_kernel,
        grid=(bsz,),
        in_specs=_u_specs(seq),
        out_specs=pl.BlockSpec((S5_CHUNK, D_GROUP, nc), lambda b: (0, 0, b)),
        out_shape=jax.ShapeDtypeStruct((S5_CHUNK, D_GROUP, bsz * nc), BF16),
        compiler_params=_cparams(("arbitrary",)),
        name="s5_pack",
    )(*([proj] * (D_GROUP // LANE)))


def _s5_kernel(ut_ref, wi_ref, ws_ref, wo_ref, ap_ref, yt_ref, *, n_chunk):
    tc, r = ut_ref.shape[0], ut_ref.shape[2]
    gs = S5_GROUP
    n_step = int(math.log2(n_chunk))
    for a in range(S5_CORE_GROUPS):
        ut = ut_ref[:, a * gs:(a + 1) * gs, :].reshape(tc * gs, r)
        s = _dot(ws_ref[a], ut)
        lane = lax.broadcasted_iota(jnp.int32, s.shape, 1) & (n_chunk - 1)
        apt = ap_ref[a].T
        x = s
        for k in range(n_step):
            sh = 1 << k
            xs = jnp.where(lane >= sh, pltpu.roll(x, sh, axis=1), 0.0)
            xsw = jnp.concatenate([xs[S5_P:], xs[:S5_P]], axis=0)
            x = x + xs * apt[:, 2 * k:2 * k + 1] + xsw * apt[:, 2 * k + 1:2 * k + 2]
        xp = jnp.where(lane >= 1, pltpu.roll(x, 1, axis=1), 0.0)
        y = _dot(wi_ref[a], ut) + _dot(wo_ref[a], xp.astype(BF16))
        yt_ref[:, a * gs:(a + 1) * gs, :] = y.reshape(tc, gs, r)


def _s5_core(at, tabs, n_chunk):
    tc, ch, r = at.shape
    w_intra_t, w_state_t, w_out_t, apow, _ = tabs
    g = w_intra_t.shape[0]
    w = tc * S5_GROUP
    ng = S5_CORE_GROUPS
    return pl.pallas_call(
        functools.partial(_s5_kernel, n_chunk=n_chunk),
        grid=(g // ng,),
        in_specs=[pl.BlockSpec((tc, ng * S5_GROUP, r), lambda i: (0, i, 0)),
                  pl.BlockSpec((ng, w, w), lambda i: (i, 0, 0)),
                  pl.BlockSpec((ng, 2 * S5_P, w), lambda i: (i, 0, 0)),
                  pl.BlockSpec((ng, w, 2 * S5_P), lambda i: (i, 0, 0)),
                  pl.BlockSpec((ng, 16, 2 * S5_P), lambda i: (i, 0, 0))],
        out_specs=pl.BlockSpec((tc, ng * S5_GROUP, r), lambda i: (0, i, 0)),
        out_shape=jax.ShapeDtypeStruct((tc, ch, r), F32),
        compiler_params=_cparams(("arbitrary",)),
        name="s5_core",
    )(at, w_intra_t, w_state_t, w_out_t, apow)


def _s5_out_kernel(yt_ref, u0_ref, u1_ref, u2_ref, u3_ref, d_ref, wg_ref, o_ref, y_scr):
    nc = yt_ref.shape[2]
    for t in range(S5_CHUNK):
        rows = pl.ds(t, nc, stride=S5_CHUNK)
        for j, u_ref in enumerate((u0_ref, u1_ref, u2_ref, u3_ref)):
            sl = slice(j * LANE, (j + 1) * LANE)
            y_scr[j, rows, :] = jax.nn.gelu(yt_ref[t, sl, :].T + d_ref[:, sl] * u_ref[rows, :])
    y = jnp.concatenate([y_scr[j] for j in range(D_GROUP // LANE)], axis=1)
    o_ref[...] = (y * jax.nn.sigmoid(_dot(y.astype(BF16), wg_ref[...]))).astype(BF16)


def _s5_out(yt, proj, d_row, w_glu_bf, bsz, seq):
    nc = seq // S5_CHUNK
    return pl.pallas_call(
        _s5_out_kernel,
        grid=(bsz,),
        in_specs=[pl.BlockSpec((S5_CHUNK, D_GROUP, nc), lambda b: (0, 0, b))] + _u_specs(seq)
        + [pl.BlockSpec((1, D_GROUP), lambda b: (0, 0)),
           pl.BlockSpec((D_GROUP, D_GROUP), lambda b: (0, 0))],
        out_specs=pl.BlockSpec((seq, D_GROUP), lambda b: (b, 0)),
        out_shape=jax.ShapeDtypeStruct((bsz * seq, D_GROUP), BF16),
        scratch_shapes=[pltpu.VMEM((D_GROUP // LANE, seq, LANE), F32)],
        compiler_params=_cparams(("arbitrary",)),
        name="s5_out",
    )(yt, proj, proj, proj, proj, d_row, w_glu_bf)


def _s5_mixer(proj, bsz, seq, tabs, w_glu_bf):
    yt = _s5_core(_s5_pack(proj, bsz, seq), tabs, seq // S5_CHUNK)
    return _s5_out(yt, proj, tabs[4], w_glu_bf, bsz, seq)


def _ret_tables():
    h, c = RET_HEADS, RET_CHUNK
    log_g = jnp.log(1.0 - 2.0 ** (-5.0 - jnp.arange(h, dtype=F32)))
    idx = jnp.arange(c, dtype=F32)
    diff = idx[:, None] - idx[None, :]
    dmat = jnp.where(diff >= 0, jnp.exp(jnp.maximum(diff, 0.0)[None] * log_g[:, None, None]), 0.0)
    zeta = jnp.exp((c - 1.0 - idx)[None, :] * log_g[:, None])
    xi = jnp.exp((idx + 1.0)[None, :] * log_g[:, None])
    g_chunk = jnp.exp(c * log_g)
    zeta_b = jnp.broadcast_to(zeta[:, :, None], (h, c, RET_DK))
    xi_b = jnp.broadcast_to(xi[:, :, None], (h, c, RET_DK))
    gch_b = jnp.broadcast_to(g_chunk[:, None, None], (h, 8, RET_DK))
    return dmat, zeta_b, xi_b, gch_b


def _ret_kernel(q_ref, k_ref, v_ref, g_ref, cos_ref, sin_ref, dm_ref, ze_ref, xi_ref, gc_ref,
                o_ref, *, n_chunk):
    c = RET_CHUNK
    dm = dm_ref[0]
    ze = ze_ref[0]
    xi = xi_ref[0]
    gch = gc_ref[0, 0:1, :]

    def body(n, r):
        sl = pl.ds(pl.multiple_of(n * c, c), c)
        cos = cos_ref[sl, :]
        sin = sin_ref[sl, :]
        qc = _rope128(q_ref[sl, :], cos, sin)
        kc = _rope128(k_ref[sl, :], cos, sin) * (RET_DK ** -0.5)
        vc = v_ref[sl, :].astype(BF16)
        inner = _dot_nt(qc.astype(BF16), kc.astype(BF16)) * dm
        o = _dot(inner.astype(BF16), vc) + _dot((qc * xi).astype(BF16), r.astype(BF16))
        kv = _dot_tn((kc * ze).astype(BF16), vc)
        r = r * gch + kv
        mu = jnp.mean(o, axis=-1, keepdims=True)
        oc = o - mu
        var = jnp.mean(oc * oc, axis=-1, keepdims=True)
        on = oc * lax.rsqrt(var + EPS)
        gg = g_ref[sl, :]
        o_ref[sl, :] = (gg * jax.nn.sigmoid(gg) * on).astype(BF16)
        return r

    lax.fori_loop(0, n_chunk, body, jnp.zeros((RET_DK, RET_DK), F32), unroll=8)


def _retention(proj, cos128, sin128, tabs, bsz, seq):
    dmat, zeta_b, xi_b, gch_b = tabs
    h = RET_HEADS
    col = lambda cb: pl.BlockSpec((seq, LANE), lambda b, hh, cb=cb: (b, cb + hh))
    tab = pl.BlockSpec((seq, LANE), lambda b, hh: (b, 0))
    head = lambda r: pl.BlockSpec((1, r, LANE), lambda b, hh: (hh, 0, 0))
    return pl.pallas_call(
        functools.partial(_ret_kernel, n_chunk=seq // RET_CHUNK),
        grid=(bsz, h),
        in_specs=[col(CB_RQ), col(CB_RK), col(CB_RV), col(CB_RG), tab, tab,
                  head(RET_CHUNK), head(RET_CHUNK), head(RET_CHUNK), head(8)],
        out_specs=pl.BlockSpec((seq, LANE), lambda b, hh: (b, hh)),
        out_shape=jax.ShapeDtypeStruct((bsz * seq, D_GROUP), BF16),
        compiler_params=_cparams(("arbitrary", "arbitrary")),
        name="retention",
    )(proj, proj, proj, proj, cos128, sin128, dmat, zeta_b, xi_b, gch_b)


def _nsa_prep_kernel(kc_ref, vc_ref, pek_ref, pev_ref, wk_ref, wv_ref, ks_ref, kw_ref, cos_ref, sin_ref,
                     q_ref, g_ref, kcmp_ref, vcmp_ref, ksr_ref, kwr_ref, qt_ref, gt_ref):
    half = NSA_CMP_STRIDE * NSA_DH
    for h in range(NSA_HEADS):
        sl = slice(h * NSA_DH, (h + 1) * NSA_DH)
        qt_ref[sl, :] = _rope128(_rms_rows(q_ref[:, sl]), cos_ref[...], sin_ref[...]).T.astype(BF16)
    gt_ref[...] = jax.nn.sigmoid(g_ref[...]).T
    n16 = kcmp_ref.shape[1]
    kparts, vparts = [], []
    for i in range(NSA_CMP_STRIDE):
        rows = pl.ds(i, n16, stride=NSA_CMP_STRIDE)
        kparts.append(_rope128(kc_ref[rows, :], cos_ref[rows, :], sin_ref[rows, :]))
        vparts.append(vc_ref[rows, :])
    xr = jnp.concatenate(kparts, axis=1)
    xv = jnp.concatenate(vparts, axis=1)
    a = _dot((xr + pek_ref[0:1, :]).astype(BF16), wk_ref[0:half, :])
    b = _dot((xr + pek_ref[1:2, :]).astype(BF16), wk_ref[half:2 * half, :])
    kcm = _rms_rows(a + pltpu.roll(b, n16 - 1, axis=0))
    rowi = lax.broadcasted_iota(jnp.int32, kcm.shape, 0)
    kcmp_ref[0] = jnp.where(rowi < n16 - 1, kcm, 0.0).astype(BF16)
    av = _dot((xv + pev_ref[0:1, :]).astype(BF16), wv_ref[0:half, :])
    bv = _dot((xv + pev_ref[1:2, :]).astype(BF16), wv_ref[half:2 * half, :])
    vcm = av + pltpu.roll(bv, n16 - 1, axis=0)
    vcmp_ref[0] = jnp.where(rowi < n16 - 1, vcm, 0.0).astype(BF16)
    cos = cos_ref[...]
    sin = sin_ref[...]
    ksr_ref[...] = _rope128(_rms_rows(ks_ref[...]), cos, sin).astype(BF16)
    kwr_ref[...] = _rope128(_rms_rows(kw_ref[...]), cos, sin).astype(BF16)


def _nsa_prep(proj, cos128, sin128, pe_k, pe_v, wk_bf, wv_bf, bsz, seq):
    n16 = seq // NSA_CMP_STRIDE
    half = NSA_CMP_STRIDE * NSA_DH
    pek = pe_k.astype(F32).reshape(2, half)
    pev = pe_v.astype(F32).reshape(2, half)
    full = lambda a: pl.BlockSpec(a.shape, lambda b: (0,) * a.ndim)
    col = lambda cb: pl.BlockSpec((seq, LANE), lambda b, cb=cb: (b, cb))
    tab = pl.BlockSpec((seq, LANE), lambda b: (b, 0))
    cmp_spec = pl.BlockSpec((1, n16, NSA_DH), lambda b: (b, 0, 0))
    seq_spec = pl.BlockSpec((seq, LANE), lambda b: (b, 0))
    return pl.pallas_call(
        _nsa_prep_kernel,
        grid=(bsz,),
        in_specs=[col(CB_NKC), col(CB_NVC), full(pek), full(pev), full(wk_bf), full(wv_bf),
                  col(CB_NKS), col(CB_NKW), tab, tab,
                  pl.BlockSpec((seq, D_GROUP), lambda b: (b, CB_NQ // 4)), col(CB_GATE)],
        out_specs=[cmp_spec, cmp_spec, seq_spec, seq_spec,
                   pl.BlockSpec((D_GROUP, seq), lambda b: (0, b)), pl.BlockSpec((LANE, seq), lambda b: (0, b))],
        out_shape=[jax.ShapeDtypeStruct((bsz, n16, NSA_DH), BF16),
                   jax.ShapeDtypeStruct((bsz, n16, NSA_DH), BF16),
                   jax.ShapeDtypeStruct((bsz * seq, NSA_DH), BF16),
                   jax.ShapeDtypeStruct((bsz * seq, NSA_DH), BF16),
                   jax.ShapeDtypeStruct((D_GROUP, bsz * seq), BF16),
                   jax.ShapeDtypeStruct((LANE, bsz * seq), F32)],
        compiler_params=_cparams(("arbitrary",)),
        name="nsa_prep",
    )(proj, proj, pek, pev, wk_bf, wv_bf, proj, proj, cos128, sin128, proj, proj)


def _attn_finish(l, acc):
    return jnp.where(l > 0.0, acc / jnp.where(l > 0.0, l, 1.0), 0.0)


def _nsa_kernel(qt_ref, gt_ref, kc_ref, vc_ref, ks_ref, vs_ref, kw_ref, vw_ref, ov_ref, o_ref):
    hh, qb, dh = NSA_HEADS, Q_BLOCK, NSA_DH
    scale = dh ** -0.5
    n = pl.program_id(1)
    qq = jnp.concatenate([qt_ref[h * dh:(h + 1) * dh, :] for h in range(hh)], axis=1)

    def branch(keys, vals, maskf):
        s = _dot(keys, qq) * scale
        e = jnp.where(jnp.concatenate([maskf] * hh, axis=1) > 0.5, jnp.exp(s), 0.0)
        return jnp.sum(e, axis=0, keepdims=True), _dot_tn(vals, e.astype(BF16)), e

    def qpos(rows):
        return n * qb + lax.broadcasted_iota(jnp.int32, (rows, qb), 1)

    def krow(rows):
        return lax.broadcasted_iota(jnp.int32, (rows, qb), 0)

    ncp = kc_ref.shape[1]
    vis = jnp.where(krow(ncp) * NSA_CMP_STRIDE + (NSA_CMP_LEN - 1) <= qpos(ncp), 1.0, 0.0)
    l_c, acc_c, e_c = branch(kc_ref[0], vc_ref[0], vis)
    o_cmp = _attn_finish(l_c, acc_c)
    p = e_c / jnp.where(l_c > 0.0, l_c, 1.0)
    psum = p[:, 0:qb]
    for h in range(1, hh):
        psum = psum + p[:, h * qb:(h + 1) * qb]

    n_sel = ks_ref.shape[0] // NSA_SEL_LEN
    imp = _dot(ov_ref[...], psum, precision=HIGHEST)[:n_sel]
    j = krow(n_sel)
    cur = qpos(n_sel) >> NSA_SEL_SHIFT
    forced = (j == 0) | (j == cur) | (j == cur - 1)
    score = jnp.where(j <= cur, imp + jnp.where(forced, NSA_FORCE_BONUS, 0.0), NEG_INF)
    rank = jnp.zeros((n_sel, qb), F32)
    for jp in range(n_sel):
        cj = score[jp:jp + 1, :]
        tie = jnp.where(j > jp, 1.0, 0.0)
        rank = rank + jnp.where(cj > score, 1.0, jnp.where(cj == score, tie, 0.0))
    selm = jnp.where(rank < float(min(NSA_TOPK, n_sel)), 1.0, 0.0)
    selm = jnp.concatenate([selm, jnp.zeros((LANE - n_sel, qb), F32)], axis=0).astype(BF16)

    tk = NSA_SEL_TK
    blk = lax.broadcasted_iota(jnp.int32, (tk, LANE), 0) >> NSA_SEL_SHIFT
    selcol = lax.broadcasted_iota(jnp.int32, (tk, LANE), 1)

    def sel_tile(kt):
        sl = pl.ds(pl.multiple_of(kt * tk, tk), tk)
        expand = jnp.where(selcol == kt * (tk // NSA_SEL_LEN) + blk, 1.0, 0.0).astype(BF16)
        picked = _dot(expand, selm)
        maskf = jnp.where(kt * tk + krow(tk) <= qpos(tk), picked, 0.0)
        l, acc, _ = branch(ks_ref[sl, :], vs_ref[sl, :].astype(BF16), maskf)
        return l, acc

    def sel_body(i, carry):
        l0, a0 = sel_tile(2 * i)
        l1, a1 = sel_tile(2 * i + 1)
        return carry[0] + (l0 + l1), carry[1] + (a0 + a1)

    zero = (jnp.zeros((1, hh * qb), F32), jnp.zeros((dh, hh * qb), F32))
    trips = (n * qb + qb + 2 * tk - 1) // (2 * tk)
    o_sel = _attn_finish(*lax.fori_loop(0, trips, sel_body, zero))

    wk = NSA_WINDOW + qb
    start = jnp.maximum(n * qb - NSA_WINDOW, 0)
    sl = pl.ds(pl.multiple_of(start, qb), wk)
    dist = qpos(wk) - (start + krow(wk))
    inwin = jnp.where(dist >= 0, jnp.where(dist < NSA_WINDOW, 1.0, 0.0), 0.0)
    l_w, acc_w, _ = branch(kw_ref[sl, :], vw_ref[sl, :].astype(BF16), inwin)
    o_win = _attn_finish(l_w, acc_w)

    gt = gt_ref[...]
    for h in range(hh):
        hq = slice(h * qb, (h + 1) * qb)
        o = (gt[3 * h:3 * h + 1, :] * o_cmp[:, hq] + gt[3 * h + 1:3 * h + 2, :] * o_sel[:, hq]
             + gt[3 * h + 2:3 * h + 3, :] * o_win[:, hq])
        o_ref[:, h * dh:(h + 1) * dh] = o.T.astype(BF16)


def _nsa_overlap_table(seq):
    n_cmp = (seq - NSA_CMP_LEN) // NSA_CMP_STRIDE + 1
    n_sel = seq // NSA_SEL_LEN
    cmp_start = np.arange(n_cmp) * NSA_CMP_STRIDE
    sel_start = np.arange(n_sel) * NSA_SEL_LEN
    overlap = ((cmp_start[None, :] < sel_start[:, None] + NSA_SEL_LEN)
               & (cmp_start[None, :] + NSA_CMP_LEN > sel_start[:, None])).astype(np.float32)
    tab = np.zeros((LANE, n_cmp + 1), np.float32)
    tab[:n_sel, :n_cmp] = overlap
    return jnp.asarray(tab)


def _nsa_attn(proj, qt, gt, kcmp, vcmp, ksr, kwr, bsz, seq):
    nb = seq // Q_BLOCK
    cmp_spec = pl.BlockSpec((1, kcmp.shape[1], NSA_DH), lambda b, n: (b, 0, 0))
    kseq = pl.BlockSpec((seq, LANE), lambda b, n: (b, 0))
    vcol = lambda cb: pl.BlockSpec((seq, LANE), lambda b, n, cb=cb: (b, cb))
    ov = _nsa_overlap_table(seq)
    return pl.pallas_call(
        _nsa_kernel,
        grid=(bsz, nb),
        in_specs=[pl.BlockSpec((D_GROUP, Q_BLOCK), lambda b, n: (0, b * nb + n)),
                  pl.BlockSpec((LANE, Q_BLOCK), lambda b, n: (0, b * nb + n)),
                  cmp_spec, cmp_spec, kseq, vcol(CB_NVS), kseq, vcol(CB_NVW),
                  pl.BlockSpec(ov.shape, lambda b, n: (0, 0))],
        out_specs=pl.BlockSpec((Q_BLOCK, D_GROUP), lambda b, n: (b * nb + n, 0)),
        out_shape=jax.ShapeDtypeStruct((bsz * seq, D_GROUP), BF16),
        compiler_params=_cparams(("arbitrary", "arbitrary")),
        name="nsa_attn",
    )(qt, gt, kcmp, vcmp, ksr, proj, kwr, proj, ov)


def _diff_prep_kernel(q_ref, k_ref, cos_ref, sin_ref, qt_ref, ko_ref):
    cos = cos_ref[...]
    sin = sin_ref[...]
    lane = lax.broadcasted_iota(jnp.int32, cos.shape, 1)
    lo = lane < DIFF_DH
    first = (lane & (DIFF_DH - 1)) < DIFF_DH // 2

    def prep(x):
        x2 = x * x
        ss_lo = jnp.sum(jnp.where(lo, x2, 0.0), axis=-1, keepdims=True)
        ss_hi = jnp.sum(jnp.where(lo, 0.0, x2), axis=-1, keepdims=True)
        ms = jnp.where(lo, ss_lo, ss_hi) * (1.0 / DIFF_DH)
        xn = x * lax.rsqrt(ms + EPS)
        partner = jnp.where(first, pltpu.roll(xn, LANE - DIFF_DH // 2, axis=1),
                            pltpu.roll(xn, DIFF_DH // 2, axis=1))
        return xn * cos + partner * sin

    for h in range(DIFF_HEADS):
        sl = slice(h * LANE, (h + 1) * LANE)
        qt_ref[sl, :] = (prep(q_ref[:, sl]) * (DIFF_DH ** -0.5)).T.astype(BF16)
        ko_ref[:, sl] = prep(k_ref[:, sl]).astype(BF16)


def _diff_prep(proj, cos64, sin64, tm):
    t = proj.shape[0]
    blk = lambda cb: pl.BlockSpec((tm, D_GROUP), lambda i, cb=cb: (i, cb // 4))
    tab = pl.BlockSpec((tm, LANE), lambda i: (i, 0))
    return pl.pallas_call(
        _diff_prep_kernel,
        grid=(t // tm,),
        in_specs=[blk(CB_DQ), blk(CB_DK), tab, tab],
        out_specs=[pl.BlockSpec((D_GROUP, tm), lambda i: (0, i)),
                   pl.BlockSpec((tm, D_GROUP), lambda i: (i, 0))],
        out_shape=[jax.ShapeDtypeStruct((D_GROUP, t), BF16), jax.ShapeDtypeStruct((t, D_GROUP), BF16)],
        compiler_params=_cparams(("arbitrary",)),
        name="diff_prep",
    )(proj, proj, cos64, sin64)


def _diff_kernel(lam_ref, qt_ref, k_ref, v_ref, o_ref, acc_ref, l_ref, *, out_scale):
    nq = DIFF_TQ
    qi = pl.program_id(2)
    qt = qt_ref[...]
    row = lax.broadcasted_iota(jnp.int32, qt.shape, 0)
    zero = jnp.zeros_like(qt)
    qq = jnp.concatenate([jnp.where(row < DIFF_DH, qt, zero), jnp.where(row < DIFF_DH, zero, qt)], axis=1)
    acc_ref[...] = jnp.zeros_like(acc_ref)
    l_ref[...] = jnp.zeros_like(l_ref)

    tk = DIFF_TK
    per_q = nq // tk

    def tile(kt, masked):
        sl = pl.ds(pl.multiple_of(kt * tk, tk), tk)
        p = jnp.exp(_dot(k_ref[sl, :], qq))
        if masked:
            kpos = kt * tk + lax.broadcasted_iota(jnp.int32, p.shape, 0)
            qpos = qi * nq + (lax.broadcasted_iota(jnp.int32, p.shape, 1) & (nq - 1))
            p = jnp.where(kpos <= qpos, p, 0.0)
        return jnp.sum(p, axis=0, keepdims=True), _dot_tn(v_ref[sl, :].astype(BF16), p.astype(BF16))

    def body(i, c):
        l0, a0 = tile(2 * i, False)
        l1, a1 = tile(2 * i + 1, False)
        l_ref[...] += l0 + l1
        acc_ref[...] += a0 + a1
        return c

    lax.fori_loop(0, qi * (per_q // 2), body, 0)
    for d in range(per_q):
        l0, a0 = tile(qi * per_q + d, True)
        l_ref[...] += l0
        acc_ref[...] += a0
    l = l_ref[...]
    acc = acc_ref[...]
    o = acc[:, :nq] / l[:, :nq] - lam_ref[0, 0] * (acc[:, nq:] / l[:, nq:])
    on = o * lax.rsqrt(jnp.mean(o * o, axis=0, keepdims=True) + EPS) * out_scale
    o_ref[...] = on.T.astype(BF16)


def _diff_attn(lam, qt, kd, proj, bsz, seq, lam_init):
    nq = DIFF_TQ
    nb = seq // nq
    h = DIFF_HEADS
    return pl.pallas_call(
        functools.partial(_diff_kernel, out_scale=1.0 - lam_init),
        grid=(bsz, h, nb),
        in_specs=[pl.BlockSpec(memory_space=pltpu.SMEM),
                  pl.BlockSpec((LANE, nq), lambda b, hh, n: (hh, b * nb + n)),
                  pl.BlockSpec((seq, LANE), lambda b, hh, n: (b, hh)),
                  pl.BlockSpec((seq, LANE), lambda b, hh, n: (b, CB_DV + hh))],
        out_specs=pl.BlockSpec((nq, LANE), lambda b, hh, n: (b * nb + n, hh)),
        out_shape=jax.ShapeDtypeStruct((bsz * seq, D_GROUP), BF16),
        scratch_shapes=[pltpu.VMEM((LANE, 2 * nq), F32), pltpu.VMEM((1, 2 * nq), F32)],
        compiler_params=_cparams(("arbitrary", "arbitrary", "arbitrary")),
        name="diff_attn",
    )(lam, qt, kd, proj)


def _outproj_kernel(ya_ref, yb_ref, yc_ref, yd_ref, w_ref, x_ref, g_ref, xo_ref, h_ref):
    mixed = jnp.concatenate([ya_ref[...], yb_ref[...], yc_ref[...], yd_ref[...]], axis=1)
    x = x_ref[...] + _dot(mixed, w_ref[0])
    xo_ref[...] = x
    h_ref[...] = (_rms_rows(x) * g_ref[...]).astype(BF16)


def _outproj(ys, w3_bf, layer, x2, g, tm):
    t, d = x2.shape
    yspec = pl.BlockSpec((tm, D_GROUP), lambda i: (i, 0))
    row = pl.BlockSpec((tm, d), lambda i: (i, 0))
    return pl.pallas_call(
        _outproj_kernel,
        grid=(t // tm,),
        in_specs=[yspec, yspec, yspec, yspec,
                  pl.BlockSpec((1, d, d), lambda i: (layer, 0, 0)), row,
                  pl.BlockSpec((1, d), lambda i: (0, 0))],
        out_specs=[row, row],
        out_shape=[jax.ShapeDtypeStruct((t, d), F32), jax.ShapeDtypeStruct((t, d), BF16)],
        compiler_params=_cparams(("arbitrary",)),
        name="outproj",
    )(*ys, w3_bf, x2, g)


def _mlp_kernel(h_ref, x_ref, w1_ref, w2_ref, o_ref):
    @pl.when(pl.program_id(1) == 0)
    def _():
        o_ref[...] = x_ref[...]

    a = _dot(h_ref[...], w1_ref[0].astype(BF16))
    a = jnp.square(jnp.maximum(a, 0.0)).astype(BF16)
    o_ref[...] += _dot(a, w2_ref[0].astype(BF16))


def _mlp(h2, x2, w1, w2, layer, tm, tf):
    t, d = x2.shape
    f = w1.shape[2]
    return pl.pallas_call(
        _mlp_kernel,
        grid=(t // tm, f // tf),
        in_specs=[pl.BlockSpec((tm, d), lambda i, j: (i, 0)),
                  pl.BlockSpec((tm, d), lambda i, j: (i, 0), pipeline_mode=pl.Buffered(1)),
                  pl.BlockSpec((1, d, tf), lambda i, j: (layer, 0, j)),
                  pl.BlockSpec((1, tf, d), lambda i, j: (layer, j, 0))],
        out_specs=pl.BlockSpec((tm, d), lambda i, j: (i, 0)),
        out_shape=jax.ShapeDtypeStruct((t, d), F32),
        compiler_params=_cparams(("arbitrary", "arbitrary")),
        name="mlp",
    )(h2, x2, w1, w2)


def _rope_tables(positions):
    bsz, seq = positions.shape
    pos = positions.astype(F32)[..., None]

    def tab(d):
        inv = ROPE_THETA ** (-jnp.arange(0, d, 2, dtype=F32) / d)
        ang = pos * inv
        return jnp.cos(ang), jnp.sin(ang)

    c, s = tab(NSA_DH)
    cos128 = jnp.concatenate([c, c], axis=-1).reshape(bsz * seq, LANE)
    sin128 = jnp.concatenate([-s, s], axis=-1).reshape(bsz * seq, LANE)
    c, s = tab(DIFF_DH)
    cos64 = jnp.concatenate([c, c, c, c], axis=-1).reshape(bsz * seq, LANE)
    sin64 = jnp.concatenate([-s, s, -s, s], axis=-1).reshape(bsz * seq, LANE)
    return cos128, sin128, cos64, sin64


def kernel(x, positions, norm1_g, w_in, s5_lambda_re, s5_lambda_im, s5_log_dt, s5_b_re, s5_b_im, s5_c_re, s5_c_im, s5_d, s5_w_glu, nsa_pe_k, nsa_pe_v, nsa_w_cmp_k, nsa_w_cmp_v, diff_lq1, diff_lk1, diff_lq2, diff_lk2, w_out, norm2_g, mlp_w1, mlp_w2):
    bsz, seq, d = x.shape
    t = bsz * seq
    depth = w_in.shape[0]
    tm = min(1024, t)
    cos128, sin128, cos64, sin64 = _rope_tables(positions)
    ret_tabs = _ret_tables()
    x2 = x.reshape(t, d).astype(F32)
    w_in = w_in.astype(BF16)
    w_out_bf = w_out.astype(BF16)
    w_glu_bf = s5_w_glu.astype(BF16)
    wk_bf = nsa_w_cmp_k.astype(BF16)
    wv_bf = nsa_w_cmp_v.astype(BF16)
    s5_tabs = _s5_tables(s5_lambda_re, s5_lambda_im, s5_log_dt, s5_b_re, s5_b_im, s5_c_re, s5_c_im, s5_d)
    for layer in range(depth):
        g1 = norm1_g[layer].astype(F32)[None, :]
        proj = _inproj(x2, g1, w_in, w_in[layer:layer + 1, :, GATE_RAW_END:], layer, min(IN_TM, t), IN_TN)
        proj_d = proj

        y_a = _s5_mixer(proj, bsz, seq, tuple(a[layer] for a in s5_tabs), w_glu_bf[layer])

        y_b = _retention(proj, cos128, sin128, ret_tabs, bsz, seq)

        kcmp, vcmp, ksr, kwr, nqt, ngt = _nsa_prep(proj, cos128, sin128, nsa_pe_k[layer], nsa_pe_v[layer],
                                                   wk_bf[layer], wv_bf[layer], bsz, seq)
        y_c = _nsa_attn(proj, nqt, ngt, kcmp, vcmp, ksr, kwr, bsz, seq)

        lam_init = 0.8 - 0.6 * math.exp(-0.3 * layer)
        lam = (jnp.exp(jnp.sum(diff_lq1[layer].astype(F32) * diff_lk1[layer].astype(F32)))
               - jnp.exp(jnp.sum(diff_lq2[layer].astype(F32) * diff_lk2[layer].astype(F32))) + lam_init)
        qt, kd = _diff_prep(proj_d, cos64, sin64, tm)
        y_d = _diff_attn(lam.reshape(1, 1).astype(F32), qt, kd, proj_d, bsz, seq, lam_init)

        x2, h2 = _outproj((y_a, y_b, y_c, y_d), w_out_bf, layer, x2,
                          norm2_g[layer].astype(F32)[None, :], min(512, t))
        x2 = _mlp(h2, x2, mlp_w1.astype(F32), mlp_w2.astype(F32), layer, tm, 512)
    return x2.reshape(bsz, seq, d).astype(x.dtype)
```

```python
import functools
import math

import jax
import jax.numpy as jnp
import numpy as np
from jax import lax
from jax.experimental import pallas as pl
from jax.experimental.pallas import tpu as pltpu

F32 = jnp.float32
BF16 = jnp.bfloat16
HIGHEST = lax.Precision.HIGHEST

D_MODEL = 2048
D_GROUP = 512
ROPE_THETA = 10000.0
EPS = 1e-6
Q_BLOCK = 128
NEG_INF = -1e30

S5_GROUP = 16
S5_P = 64
S5_CHUNK = 16
S5_CORE_GROUPS = 4
S5_TAB_GROUPS = 8

RET_HEADS = 4
RET_DK = 128
RET_CHUNK = 128
RET_UNROLL = 16

NSA_HEADS = 4
NSA_DH = 128
NSA_CMP_LEN = 32
NSA_CMP_STRIDE = 16
NSA_SEL_LEN = 64
NSA_TOPK = 8
NSA_WINDOW = 256
NSA_FORCE_BONUS = 1e4
NSA_SEL_SHIFT = 6
NSA_SEL_TK = 256

DIFF_HEADS = 4
DIFF_DH = 64
DIFF_TQ = 512
DIFF_TK = 256

LANE = 128
GATE_RAW_END = 3852
IN_TN = 512
MAIN_TILES = 8
CB_U, CB_RQ, CB_RK, CB_RV, CB_RG, CB_NQ = 0, 4, 8, 12, 16, 20
CB_NKC, CB_NVC, CB_NKS, CB_NVS, CB_NKW, CB_NVW, CB_GATE = 24, 25, 26, 27, 28, 29, 30
CB_DQ, CB_DK, CB_DV = 32, 36, 40

VMEM_LIMIT = 56 * 1024 * 1024
IN_TM = 2048
IN_VMEM_LIMIT = 60 * 1024 * 1024
ROW_TILE = 1024
MLP_TF = 512
OUT_TM = 512


def _cparams(sem):
    return pltpu.CompilerParams(dimension_semantics=sem, vmem_limit_bytes=VMEM_LIMIT)


def _dot(a, b, **kw):
    return jnp.dot(a, b, preferred_element_type=F32, **kw)


def _dot_nt(a, b):
    return lax.dot_general(a, b, (((1,), (1,)), ((), ())), preferred_element_type=F32)


def _dot_tn(a, b):
    return lax.dot_general(a, b, (((0,), (0,)), ((), ())), preferred_element_type=F32)


def _rms_rows(x):
    return x * lax.rsqrt(jnp.mean(x * x, axis=-1, keepdims=True) + EPS)


def _rope128(x, cos, sin_signed):
    return x * cos + pltpu.roll(x, 64, axis=1) * sin_signed


def _inproj_kernel(x_ref, g_ref, wm_ref, wd_ref, o_ref, xn_ref):
    j = pl.program_id(1)

    @pl.when(j == 0)
    def _():
        x = x_ref[...]
        xn_ref[...] = (_rms_rows(x) * g_ref[...]).astype(BF16)

    @pl.when(j < MAIN_TILES)
    def _():
        o_ref[...] = _dot(xn_ref[...], wm_ref[0])

    @pl.when(j >= MAIN_TILES)
    def _():
        o_ref[...] = _dot(xn_ref[...], wd_ref[0])


def _inproj(x2, g, w3, w_diff, layer, tm, tn):
    t, d = x2.shape
    n_tiles = MAIN_TILES + w_diff.shape[2] // tn
    return pl.pallas_call(
        _inproj_kernel,
        grid=(t // tm, n_tiles),
        in_specs=[pl.BlockSpec((tm, d), lambda i, j: (i, 0)),
                  pl.BlockSpec((1, d), lambda i, j: (0, 0)),
                  pl.BlockSpec((1, d, tn), lambda i, j: (layer, 0, jnp.minimum(j, MAIN_TILES - 1))),
                  pl.BlockSpec((1, d, tn), lambda i, j: (0, 0, jnp.maximum(j - MAIN_TILES, 0)))],
        out_specs=pl.BlockSpec((tm, tn), lambda i, j: (i, j)),
        out_shape=jax.ShapeDtypeStruct((t, n_tiles * tn), F32),
        scratch_shapes=[pltpu.VMEM((tm, d), BF16)],
        compiler_params=pltpu.CompilerParams(dimension_semantics=("arbitrary", "arbitrary"),
                                             vmem_limit_bytes=IN_VMEM_LIMIT),
        name="inproj",
    )(x2, g, w3, w_diff)


def _s5_tab_kernel(*refs):
    for a in range(S5_TAB_GROUPS):
        _s5_tab_group(*[r.at[0, a] for r in refs])


def _s5_tab_group(lre_ref, lim_ref, ldt_ref, b_re_ref, b_im_ref, c_re_ref, c_im_ref,
                  wi_ref, ws_ref, wo_ref, ap_ref):
    tc, p, gs = S5_CHUNK, S5_P, S5_GROUP
    w = tc * gs
    lam_re = lre_ref[...]
    lam_im = lim_ref[...]
    dt = jnp.exp(ldt_ref[...])
    lre = lam_re * dt
    lim = lam_im * dt

    def lbpow(e):
        mag = jnp.exp(lre * e)
        return mag * jnp.cos(lim * e), mag * jnp.sin(lim * e)

    pr, pi = lbpow(lax.broadcasted_iota(jnp.int32, (24, 2 * p), 0).astype(F32))
    lb_re, lb_im = pr[1:2], pi[1:2]
    den = lam_re * lam_re + lam_im * lam_im
    f_re = ((lb_re - 1.0) * lam_re + lb_im * lam_im) / den
    f_im = (lb_im * lam_re - (lb_re - 1.0) * lam_im) / den

    r8 = lax.broadcasted_iota(jnp.int32, (8, 2 * p), 0)
    f8 = jnp.where(r8 == 0, f_re, jnp.where(r8 == 1, f_im, 0.0))
    cols = jnp.concatenate([pr, pi, f8, jnp.zeros((72, 2 * p), F32)], axis=0).T
    pr_c, pi_c = cols[:p, 0:tc], cols[:p, 24:24 + tc]
    f_re_c, f_im_c = cols[:p, 48:49], cols[:p, 49:50]

    b_re = b_re_ref[...]
    b_im = b_im_ref[...]
    bb_re = f_re_c * b_re - f_im_c * b_im
    bb_im = f_re_c * b_im + f_im_c * b_re
    hrow = lax.broadcasted_iota(jnp.int32, (gs, w), 0)
    lane_w = lax.broadcasted_iota(jnp.int32, (gs, w), 1)
    tile_m = jnp.where((lane_w & (gs - 1)) == hrow, 1.0, 0.0)
    rev_m = jnp.where(hrow == (tc - 1) - (lane_w >> 4), 1.0, 0.0)
    bbt_re = _dot(bb_re, tile_m, precision=HIGHEST)
    bbt_im = _dot(bb_im, tile_m, precision=HIGHEST)
    p15_re = _dot(pr_c, rev_m, precision=HIGHEST)
    p15_im = _dot(pi_c, rev_m, precision=HIGHEST)
    ws_ref[...] = jnp.concatenate([p15_re * bbt_re - p15_im * bbt_im,
                                    p15_re * bbt_im + p15_im * bbt_re], axis=0).astype(BF16)

    c_re = c_re_ref[...]
    c_im = c_im_ref[...]
    lo = lax.broadcasted_iota(jnp.int32, (1, 2 * p), 1) < p
    row_a = jnp.where(lo, pr, -pi)
    row_b = jnp.where(lo, -pi, -pr)
    wo_ref[...] = jnp.concatenate([c_re * row_a[t + 1:t + 2] + c_im * row_b[t + 1:t + 2]
                                    for t in range(tc)], axis=0).astype(BF16)

    x_re = jnp.concatenate([c_re * pr[k:k + 1] - c_im * pi[k:k + 1] for k in range(tc)], axis=0)
    x_im = jnp.concatenate([c_re * pi[k:k + 1] + c_im * pr[k:k + 1] for k in range(tc)], axis=0)
    zpad = jnp.zeros((p, w), F32)
    kt = (_dot(x_re, jnp.concatenate([bbt_re, zpad], axis=0), precision=HIGHEST)
          - _dot(x_im, jnp.concatenate([bbt_im, zpad], axis=0), precision=HIGHEST))
    sblk = lane_w >> 4
    rows = []
    for t in range(tc):
        acc = jnp.zeros((gs, w), F32)
        for k in range(t + 1):
            acc = jnp.where(sblk == t - k, kt[k * gs:(k + 1) * gs], acc)
        rows.append(acc)
    wi_ref[...] = jnp.concatenate(rows, axis=0).astype(BF16)

    r16 = lax.broadcasted_iota(jnp.int32, (16, 2 * p), 0)
    sr, si = lbpow(lax.shift_left(jnp.full((16, 2 * p), tc, jnp.int32), r16 >> 1).astype(F32))
    ap_ref[...] = jnp.where((r16 & 1) == 1, jnp.where(lo, -si, si), sr)


def _s5_tables(lam_re, lam_im, log_dt, b_re, b_im, c_re, c_im, d_skip):
    depth, g, p = lam_re.shape
    w = S5_CHUNK * S5_GROUP
    dup = lambda a: jnp.concatenate([a.astype(F32), a.astype(F32)], axis=-1)
    ldt = jnp.broadcast_to(log_dt.astype(F32)[:, :, None, None], (depth, g, 1, 2 * p))
    ng = S5_TAB_GROUPS
    row = pl.BlockSpec((1, ng, 1, 2 * p), lambda l, i: (l, i, 0, 0))
    bspec = pl.BlockSpec((1, ng, p, S5_GROUP), lambda l, i: (l, i, 0, 0))
    cspec = pl.BlockSpec((1, ng, S5_GROUP, 2 * p), lambda l, i: (l, i, 0, 0))
    out = lambda r, c: pl.BlockSpec((1, ng, r, c), lambda l, i: (l, i, 0, 0))
    tabs = pl.pallas_call(
        _s5_tab_kernel,
        grid=(depth, g // ng),
        in_specs=[row, row, row, bspec, bspec, cspec, cspec],
        out_specs=[out(w, w), out(2 * p, w), out(w, 2 * p), out(16, 2 * p)],
        out_shape=[jax.ShapeDtypeStruct((depth, g, w, w), BF16),
                   jax.ShapeDtypeStruct((depth, g, 2 * p, w), BF16),
                   jax.ShapeDtypeStruct((depth, g, w, 2 * p), BF16),
                   jax.ShapeDtypeStruct((depth, g, 16, 2 * p), F32)],
        compiler_params=_cparams(("arbitrary", "arbitrary")),
        name="s5_tables",
    )(dup(lam_re)[:, :, None, :], dup(lam_im)[:, :, None, :], ldt, b_re.astype(F32), b_im.astype(F32),
      dup(c_re), dup(c_im))
    d_row = d_skip.astype(F32).reshape(depth, 1, g * S5_GROUP)
    return (*tabs, d_row)


def _s5_pack_kernel(*refs):
    at_ref = refs[-1]
    nc = at_ref.shape[2]
    for t in range(S5_CHUNK):
        for j, u_ref in enumerate(refs[:-1]):
            at_ref[t, j * LANE:(j + 1) * LANE, :] = u_ref[pl.ds(t, nc, stride=S5_CHUNK), :].T.astype(BF16)


def _u_specs(seq):
    return [pl.BlockSpec((seq, LANE), lambda b, j=j: (b, CB_U + j)) for j in range(D_GROUP // LANE)]


def _s5_pack(proj, bsz, seq):
    nc = seq // S5_CHUNK
    return pl.pallas_call(
        _s5_pack_kernel,
        grid=(bsz,),
        in_specs=_u_specs(seq),
        out_specs=pl.BlockSpec((S5_CHUNK, D_GROUP, nc), lambda b: (0, 0, b)),
        out_shape=jax.ShapeDtypeStruct((S5_CHUNK, D_GROUP, bsz * nc), BF16),
        compiler_params=_cparams(("arbitrary",)),
        name="s5_pack",
    )(*([proj] * (D_GROUP // LANE)))


def _s5_kernel(ut_ref, wi_ref, ws_ref, wo_ref, ap_ref, yt_ref, *, n_chunk):
    tc, r = ut_ref.shape[0], ut_ref.shape[2]
    gs = S5_GROUP
    n_step = int(math.log2(n_chunk))
    for a in range(S5_CORE_GROUPS):
        ut = ut_ref[:, a * gs:(a + 1) * gs, :].reshape(tc * gs, r)
        s = _dot(ws_ref[a], ut)
        lane = lax.broadcasted_iota(jnp.int32, s.shape, 1) & (n_chunk - 1)
        apt = ap_ref[a].T
        x = s
        for k in range(n_step):
            sh = 1 << k
            xs = jnp.where(lane >= sh, pltpu.roll(x, sh, axis=1), 0.0)
            xsw = jnp.concatenate([xs[S5_P:], xs[:S5_P]], axis=0)
            x = x + xs * apt[:, 2 * k:2 * k + 1] + xsw * apt[:, 2 * k + 1:2 * k + 2]
        xp = jnp.where(lane >= 1, pltpu.roll(x, 1, axis=1), 0.0)
        y = _dot(wi_ref[a], ut) + _dot(wo_ref[a], xp.astype(BF16))
        yt_ref[:, a * gs:(a + 1) * gs, :] = y.reshape(tc, gs, r)


def _s5_core(at, tabs, n_chunk):
    tc, ch, r = at.shape
    w_intra_t, w_state_t, w_out_t, apow, _ = tabs
    g = w_intra_t.shape[0]
    w = tc * S5_GROUP
    ng = S5_CORE_GROUPS
    return pl.pallas_call(
        functools.partial(_s5_kernel, n_chunk=n_chunk),
        grid=(g // ng,),
        in_specs=[pl.BlockSpec((tc, ng * S5_GROUP, r), lambda i: (0, i, 0)),
                  pl.BlockSpec((ng, w, w), lambda i: (i, 0, 0)),
                  pl.BlockSpec((ng, 2 * S5_P, w), lambda i: (i, 0, 0)),
                  pl.BlockSpec((ng, w, 2 * S5_P), lambda i: (i, 0, 0)),
                  pl.BlockSpec((ng, 16, 2 * S5_P), lambda i: (i, 0, 0))],
        out_specs=pl.BlockSpec((tc, ng * S5_GROUP, r), lambda i: (0, i, 0)),
        out_shape=jax.ShapeDtypeStruct((tc, ch, r), F32),
        compiler_params=_cparams(("arbitrary",)),
        name="s5_core",
    )(at, w_intra_t, w_state_t, w_out_t, apow)


def _s5_out_kernel(yt_ref, u0_ref, u1_ref, u2_ref, u3_ref, d_ref, wg_ref, o_ref, y_scr):
    nc = yt_ref.shape[2]
    for t in range(S5_CHUNK):
        rows = pl.ds(t, nc, stride=S5_CHUNK)
        for j, u_ref in enumerate((u0_ref, u1_ref, u2_ref, u3_ref)):
            sl = slice(j * LANE, (j + 1) * LANE)
            y_scr[j, rows, :] = jax.nn.gelu(yt_ref[t, sl, :].T + d_ref[:, sl] * u_ref[rows, :])
    y = jnp.concatenate([y_scr[j] for j in range(D_GROUP // LANE)], axis=1)
    o_ref[...] = (y * jax.nn.sigmoid(_dot(y.astype(BF16), wg_ref[...]))).astype(BF16)


def _s5_out(yt, proj, d_row, w_glu_bf, bsz, seq):
    nc = seq // S5_CHUNK
    return pl.pallas_call(
        _s5_out_kernel,
        grid=(bsz,),
        in_specs=[pl.BlockSpec((S5_CHUNK, D_GROUP, nc), lambda b: (0, 0, b))] + _u_specs(seq)
        + [pl.BlockSpec((1, D_GROUP), lambda b: (0, 0)),
           pl.BlockSpec((D_GROUP, D_GROUP), lambda b: (0, 0))],
        out_specs=pl.BlockSpec((seq, D_GROUP), lambda b: (b, 0)),
        out_shape=jax.ShapeDtypeStruct((bsz * seq, D_GROUP), BF16),
        scratch_shapes=[pltpu.VMEM((D_GROUP // LANE, seq, LANE), F32)],
        compiler_params=_cparams(("arbitrary",)),
        name="s5_out",
    )(yt, proj, proj, proj, proj, d_row, w_glu_bf)


def _s5_mixer(proj, bsz, seq, tabs, w_glu_bf):
    yt = _s5_core(_s5_pack(proj, bsz, seq), tabs, seq // S5_CHUNK)
    return _s5_out(yt, proj, tabs[4], w_glu_bf, bsz, seq)


def _ret_tables():
    h, c = RET_HEADS, RET_CHUNK
    log_g = jnp.log(1.0 - 2.0 ** (-5.0 - jnp.arange(h, dtype=F32)))
    idx = jnp.arange(c, dtype=F32)
    diff = idx[:, None] - idx[None, :]
    dmat = jnp.where(diff >= 0, jnp.exp(jnp.maximum(diff, 0.0)[None] * log_g[:, None, None]), 0.0)
    zeta = jnp.exp((c - 1.0 - idx)[None, :] * log_g[:, None])
    xi = jnp.exp((idx + 1.0)[None, :] * log_g[:, None])
    g_chunk = jnp.exp(c * log_g)
    zeta_b = jnp.broadcast_to(zeta[:, :, None], (h, c, RET_DK))
    xi_b = jnp.broadcast_to(xi[:, :, None], (h, c, RET_DK))
    gch_b = jnp.broadcast_to(g_chunk[:, None, None], (h, 8, RET_DK))
    return dmat, zeta_b, xi_b, gch_b


def _ret_kernel(q_ref, k_ref, v_ref, g_ref, cos_ref, sin_ref, dm_ref, ze_ref, xi_ref, gc_ref,
                o_ref, *, n_chunk):
    c = RET_CHUNK
    dm = dm_ref[0]
    ze = ze_ref[0]
    xi = xi_ref[0]
    gch = gc_ref[0, 0:1, :]

    def body(n, r):
        sl = pl.ds(pl.multiple_of(n * c, c), c)
        cos = cos_ref[sl, :]
        sin = sin_ref[sl, :]
        qc = _rope128(q_ref[sl, :], cos, sin)
        kc = _rope128(k_ref[sl, :], cos, sin) * (RET_DK ** -0.5)
        vc = v_ref[sl, :].astype(BF16)
        inner = _dot_nt(qc.astype(BF16), kc.astype(BF16)) * dm
        o = _dot(inner.astype(BF16), vc) + _dot((qc * xi).astype(BF16), r.astype(BF16))
        kv = _dot_tn((kc * ze).astype(BF16), vc)
        r = r * gch + kv
        mu = jnp.mean(o, axis=-1, keepdims=True)
        oc = o - mu
        var = jnp.mean(oc * oc, axis=-1, keepdims=True)
        on = oc * lax.rsqrt(var + EPS)
        gg = g_ref[sl, :]
        o_ref[sl, :] = (gg * jax.nn.sigmoid(gg) * on).astype(BF16)
        return r

    lax.fori_loop(0, n_chunk, body, jnp.zeros((RET_DK, RET_DK), F32), unroll=RET_UNROLL)


def _retention(proj, cos128, sin128, tabs, bsz, seq):
    dmat, zeta_b, xi_b, gch_b = tabs
    h = RET_HEADS
    col = lambda cb: pl.BlockSpec((seq, LANE), lambda b, hh, cb=cb: (b, cb + hh))
    tab = pl.BlockSpec((seq, LANE), lambda b, hh: (b, 0))
    head = lambda r: pl.BlockSpec((1, r, LANE), lambda b, hh: (hh, 0, 0))
    return pl.pallas_call(
        functools.partial(_ret_kernel, n_chunk=seq // RET_CHUNK),
        grid=(bsz, h),
        in_specs=[col(CB_RQ), col(CB_RK), col(CB_RV), col(CB_RG), tab, tab,
                  head(RET_CHUNK), head(RET_CHUNK), head(RET_CHUNK), head(8)],
        out_specs=pl.BlockSpec((seq, LANE), lambda b, hh: (b, hh)),
        out_shape=jax.ShapeDtypeStruct((bsz * seq, D_GROUP), BF16),
        compiler_params=_cparams(("arbitrary", "arbitrary")),
        name="retention",
    )(proj, proj, proj, proj, cos128, sin128, dmat, zeta_b, xi_b, gch_b)


def _nsa_prep_kernel(kc_ref, vc_ref, pek_ref, pev_ref, wk_ref, wv_ref, ks_ref, kw_ref, cos_ref, sin_ref,
                     q_ref, g_ref, kcmp_ref, vcmp_ref, ksr_ref, kwr_ref, qt_ref, gt_ref):
    half = NSA_CMP_STRIDE * NSA_DH
    for h in range(NSA_HEADS):
        sl = slice(h * NSA_DH, (h + 1) * NSA_DH)
        qt_ref[sl, :] = _rope128(_rms_rows(q_ref[:, sl]), cos_ref[...], sin_ref[...]).T.astype(BF16)
    gt_ref[...] = jax.nn.sigmoid(g_ref[...]).T
    n16 = kcmp_ref.shape[1]
    kparts, vparts = [], []
    for i in range(NSA_CMP_STRIDE):
        rows = pl.ds(i, n16, stride=NSA_CMP_STRIDE)
        kparts.append(_rope128(kc_ref[rows, :], cos_ref[rows, :], sin_ref[rows, :]))
        vparts.append(vc_ref[rows, :])
    xr = jnp.concatenate(kparts, axis=1)
    xv = jnp.concatenate(vparts, axis=1)
    a = _dot((xr + pek_ref[0:1, :]).astype(BF16), wk_ref[0:half, :])
    b = _dot((xr + pek_ref[1:2, :]).astype(BF16), wk_ref[half:2 * half, :])
    kcm = _rms_rows(a + pltpu.roll(b, n16 - 1, axis=0))
    rowi = lax.broadcasted_iota(jnp.int32, kcm.shape, 0)
    kcmp_ref[0] = jnp.where(rowi < n16 - 1, kcm, 0.0).astype(BF16)
    av = _dot((xv + pev_ref[0:1, :]).astype(BF16), wv_ref[0:half, :])
    bv = _dot((xv + pev_ref[1:2, :]).astype(BF16), wv_ref[half:2 * half, :])
    vcm = av + pltpu.roll(bv, n16 - 1, axis=0)
    vcmp_ref[0] = jnp.where(rowi < n16 - 1, vcm, 0.0).astype(BF16)
    cos = cos_ref[...]
    sin = sin_ref[...]
    ksr_ref[...] = _rope128(_rms_rows(ks_ref[...]), cos, sin).astype(BF16)
    kwr_ref[...] = _rope128(_rms_rows(kw_ref[...]), cos, sin).astype(BF16)


def _nsa_prep(proj, cos128, sin128, pe_k, pe_v, wk_bf, wv_bf, bsz, seq):
    n16 = seq // NSA_CMP_STRIDE
    half = NSA_CMP_STRIDE * NSA_DH
    pek = pe_k.astype(F32).reshape(2, half)
    pev = pe_v.astype(F32).reshape(2, half)
    full = lambda a: pl.BlockSpec(a.shape, lambda b: (0,) * a.ndim)
    col = lambda cb: pl.BlockSpec((seq, LANE), lambda b, cb=cb: (b, cb))
    tab = pl.BlockSpec((seq, LANE), lambda b: (b, 0))
    cmp_spec = pl.BlockSpec((1, n16, NSA_DH), lambda b: (b, 0, 0))
    seq_spec = pl.BlockSpec((seq, LANE), lambda b: (b, 0))
    return pl.pallas_call(
        _nsa_prep_kernel,
        grid=(bsz,),
        in_specs=[col(CB_NKC), col(CB_NVC), full(pek), full(pev), full(wk_bf), full(wv_bf),
                  col(CB_NKS), col(CB_NKW), tab, tab,
                  pl.BlockSpec((seq, D_GROUP), lambda b: (b, CB_NQ // 4)), col(CB_GATE)],
        out_specs=[cmp_spec, cmp_spec, seq_spec, seq_spec,
                   pl.BlockSpec((D_GROUP, seq), lambda b: (0, b)), pl.BlockSpec((LANE, seq), lambda b: (0, b))],
        out_shape=[jax.ShapeDtypeStruct((bsz, n16, NSA_DH), BF16),
                   jax.ShapeDtypeStruct((bsz, n16, NSA_DH), BF16),
                   jax.ShapeDtypeStruct((bsz * seq, NSA_DH), BF16),
                   jax.ShapeDtypeStruct((bsz * seq, NSA_DH), BF16),
                   jax.ShapeDtypeStruct((D_GROUP, bsz * seq), BF16),
                   jax.ShapeDtypeStruct((LANE, bsz * seq), F32)],
        compiler_params=_cparams(("arbitrary",)),
        name="nsa_prep",
    )(proj, proj, pek, pev, wk_bf, wv_bf, proj, proj, cos128, sin128, proj, proj)


def _attn_finish(l, acc):
    return jnp.where(l > 0.0, acc / jnp.where(l > 0.0, l, 1.0), 0.0)


def _nsa_kernel(qt_ref, gt_ref, kc_ref, vc_ref, ks_ref, vs_ref, kw_ref, vw_ref, ov_ref, o_ref):
    hh, qb, dh = NSA_HEADS, Q_BLOCK, NSA_DH
    scale = dh ** -0.5
    n = pl.program_id(1)
    qq = jnp.concatenate([qt_ref[h * dh:(h + 1) * dh, :] for h in range(hh)], axis=1)

    def branch(keys, vals, maskf):
        s = _dot(keys, qq) * scale
        e = jnp.where(jnp.concatenate([maskf] * hh, axis=1) > 0.5, jnp.exp(s), 0.0)
        return jnp.sum(e, axis=0, keepdims=True), _dot_tn(vals, e.astype(BF16)), e

    def qpos(rows):
        return n * qb + lax.broadcasted_iota(jnp.int32, (rows, qb), 1)

    def krow(rows):
        return lax.broadcasted_iota(jnp.int32, (rows, qb), 0)

    ncp = kc_ref.shape[1]
    vis = jnp.where(krow(ncp) * NSA_CMP_STRIDE + (NSA_CMP_LEN - 1) <= qpos(ncp), 1.0, 0.0)
    l_c, acc_c, e_c = branch(kc_ref[0], vc_ref[0], vis)
    o_cmp = _attn_finish(l_c, acc_c)
    p = e_c / jnp.where(l_c > 0.0, l_c, 1.0)
    psum = p[:, 0:qb]
    for h in range(1, hh):
        psum = psum + p[:, h * qb:(h + 1) * qb]

    n_sel = ks_ref.shape[0] // NSA_SEL_LEN
    imp = _dot(ov_ref[...], psum, precision=HIGHEST)[:n_sel]
    j = krow(n_sel)
    cur = qpos(n_sel) >> NSA_SEL_SHIFT
    forced = (j == 0) | (j == cur) | (j == cur - 1)
    score = jnp.where(j <= cur, imp + jnp.where(forced, NSA_FORCE_BONUS, 0.0), NEG_INF)
    rank = jnp.zeros((n_sel, qb), F32)
    for jp in range(n_sel):
        cj = score[jp:jp + 1, :]
        tie = jnp.where(j > jp, 1.0, 0.0)
        rank = rank + jnp.where(cj > score, 1.0, jnp.where(cj == score, tie, 0.0))
    selm = jnp.where(rank < float(min(NSA_TOPK, n_sel)), 1.0, 0.0)
    selm = jnp.concatenate([selm, jnp.zeros((LANE - n_sel, qb), F32)], axis=0).astype(BF16)

    tk = NSA_SEL_TK
    blk = lax.broadcasted_iota(jnp.int32, (tk, LANE), 0) >> NSA_SEL_SHIFT
    selcol = lax.broadcasted_iota(jnp.int32, (tk, LANE), 1)

    def sel_tile(kt):
        sl = pl.ds(pl.multiple_of(kt * tk, tk), tk)
        expand = jnp.where(selcol == kt * (tk // NSA_SEL_LEN) + blk, 1.0, 0.0).astype(BF16)
        picked = _dot(expand, selm)
        maskf = jnp.where(kt * tk + krow(tk) <= qpos(tk), picked, 0.0)
        l, acc, _ = branch(ks_ref[sl, :], vs_ref[sl, :].astype(BF16), maskf)
        return l, acc

    def sel_body(i, carry):
        l0, a0 = sel_tile(2 * i)
        l1, a1 = sel_tile(2 * i + 1)
        return carry[0] + (l0 + l1), carry[1] + (a0 + a1)

    zero = (jnp.zeros((1, hh * qb), F32), jnp.zeros((dh, hh * qb), F32))
    trips = (n * qb + qb + 2 * tk - 1) // (2 * tk)
    o_sel = _attn_finish(*lax.fori_loop(0, trips, sel_body, zero))

    wk = NSA_WINDOW + qb
    start = jnp.maximum(n * qb - NSA_WINDOW, 0)
    sl = pl.ds(pl.multiple_of(start, qb), wk)
    dist = qpos(wk) - (start + krow(wk))
    inwin = jnp.where(dist >= 0, jnp.where(dist < NSA_WINDOW, 1.0, 0.0), 0.0)
    l_w, acc_w, _ = branch(kw_ref[sl, :], vw_ref[sl, :].astype(BF16), inwin)
    o_win = _attn_finish(l_w, acc_w)

    gt = gt_ref[...]
    for h in range(hh):
        hq = slice(h * qb, (h + 1) * qb)
        o = (gt[3 * h:3 * h + 1, :] * o_cmp[:, hq] + gt[3 * h + 1:3 * h + 2, :] * o_sel[:, hq]
             + gt[3 * h + 2:3 * h + 3, :] * o_win[:, hq])
        o_ref[:, h * dh:(h + 1) * dh] = o.T.astype(BF16)


def _nsa_overlap_table(seq):
    n_cmp = (seq - NSA_CMP_LEN) // NSA_CMP_STRIDE + 1
    n_sel = seq // NSA_SEL_LEN
    cmp_start = np.arange(n_cmp) * NSA_CMP_STRIDE
    sel_start = np.arange(n_sel) * NSA_SEL_LEN
    overlap = ((cmp_start[None, :] < sel_start[:, None] + NSA_SEL_LEN)
               & (cmp_start[None, :] + NSA_CMP_LEN > sel_start[:, None])).astype(np.float32)
    tab = np.zeros((LANE, n_cmp + 1), np.float32)
    tab[:n_sel, :n_cmp] = overlap
    return jnp.asarray(tab)


def _nsa_attn(proj, qt, gt, kcmp, vcmp, ksr, kwr, bsz, seq):
    nb = seq // Q_BLOCK
    cmp_spec = pl.BlockSpec((1, kcmp.shape[1], NSA_DH), lambda b, n: (b, 0, 0))
    kseq = pl.BlockSpec((seq, LANE), lambda b, n: (b, 0))
    vcol = lambda cb: pl.BlockSpec((seq, LANE), lambda b, n, cb=cb: (b, cb))
    ov = _nsa_overlap_table(seq)
    return pl.pallas_call(
        _nsa_kernel,
        grid=(bsz, nb),
        in_specs=[pl.BlockSpec((D_GROUP, Q_BLOCK), lambda b, n: (0, b * nb + n)),
                  pl.BlockSpec((LANE, Q_BLOCK), lambda b, n: (0, b * nb + n)),
                  cmp_spec, cmp_spec, kseq, vcol(CB_NVS), kseq, vcol(CB_NVW),
                  pl.BlockSpec(ov.shape, lambda b, n: (0, 0))],
        out_specs=pl.BlockSpec((Q_BLOCK, D_GROUP), lambda b, n: (b * nb + n, 0)),
        out_shape=jax.ShapeDtypeStruct((bsz * seq, D_GROUP), BF16),
        compiler_params=_cparams(("arbitrary", "arbitrary")),
        name="nsa_attn",
    )(qt, gt, kcmp, vcmp, ksr, proj, kwr, proj, ov)


def _diff_prep_kernel(q_ref, k_ref, cos_ref, sin_ref, qt_ref, ko_ref):
    cos = cos_ref[...]
    sin = sin_ref[...]
    lane = lax.broadcasted_iota(jnp.int32, cos.shape, 1)
    lo = lane < DIFF_DH
    first = (lane & (DIFF_DH - 1)) < DIFF_DH // 2

    def prep(x):
        x2 = x * x
        ss_lo = jnp.sum(jnp.where(lo, x2, 0.0), axis=-1, keepdims=True)
        ss_hi = jnp.sum(jnp.where(lo, 0.0, x2), axis=-1, keepdims=True)
        ms = jnp.where(lo, ss_lo, ss_hi) * (1.0 / DIFF_DH)
        xn = x * lax.rsqrt(ms + EPS)
        partner = jnp.where(first, pltpu.roll(xn, LANE - DIFF_DH // 2, axis=1),
                            pltpu.roll(xn, DIFF_DH // 2, axis=1))
        return xn * cos + partner * sin

    for h in range(DIFF_HEADS):
        sl = slice(h * LANE, (h + 1) * LANE)
        qt_ref[sl, :] = (prep(q_ref[:, sl]) * (DIFF_DH ** -0.5)).T.astype(BF16)
        ko_ref[:, sl] = prep(k_ref[:, sl]).astype(BF16)


def _diff_prep(proj, cos64, sin64, tm):
    t = proj.shape[0]
    blk = lambda cb: pl.BlockSpec((tm, D_GROUP), lambda i, cb=cb: (i, cb // 4))
    tab = pl.BlockSpec((tm, LANE), lambda i: (i, 0))
    return pl.pallas_call(
        _diff_prep_kernel,
        grid=(t // tm,),
        in_specs=[blk(CB_DQ), blk(CB_DK), tab, tab],
        out_specs=[pl.BlockSpec((D_GROUP, tm), lambda i: (0, i)),
                   pl.BlockSpec((tm, D_GROUP), lambda i: (i, 0))],
        out_shape=[jax.ShapeDtypeStruct((D_GROUP, t), BF16), jax.ShapeDtypeStruct((t, D_GROUP), BF16)],
        compiler_params=_cparams(("arbitrary",)),
        name="diff_prep",
    )(proj, proj, cos64, sin64)


def _diff_kernel(lam_ref, qt_ref, k_ref, v_ref, o_ref, acc_ref, l_ref, *, out_scale):
    nq = DIFF_TQ
    qi = pl.program_id(2)
    qt = qt_ref[...]
    row = lax.broadcasted_iota(jnp.int32, qt.shape, 0)
    zero = jnp.zeros_like(qt)
    qq = jnp.concatenate([jnp.where(row < DIFF_DH, qt, zero), jnp.where(row < DIFF_DH, zero, qt)], axis=1)
    acc_ref[...] = jnp.zeros_like(acc_ref)
    l_ref[...] = jnp.zeros_like(l_ref)

    tk = DIFF_TK
    per_q = nq // tk

    def tile(kt, masked):
        sl = pl.ds(pl.multiple_of(kt * tk, tk), tk)
        p = jnp.exp(_dot(k_ref[sl, :], qq))
        if masked:
            kpos = kt * tk + lax.broadcasted_iota(jnp.int32, p.shape, 0)
            qpos = qi * nq + (lax.broadcasted_iota(jnp.int32, p.shape, 1) & (nq - 1))
            p = jnp.where(kpos <= qpos, p, 0.0)
        return jnp.sum(p, axis=0, keepdims=True), _dot_tn(v_ref[sl, :].astype(BF16), p.astype(BF16))

    def body(i, c):
        l0, a0 = tile(2 * i, False)
        l1, a1 = tile(2 * i + 1, False)
        l_ref[...] += l0 + l1
        acc_ref[...] += a0 + a1
        return c

    lax.fori_loop(0, qi * (per_q // 2), body, 0)
    for d in range(per_q):
        l0, a0 = tile(qi * per_q + d, True)
        l_ref[...] += l0
        acc_ref[...] += a0
    l = l_ref[...]
    acc = acc_ref[...]
    o = acc[:, :nq] / l[:, :nq] - lam_ref[0, 0] * (acc[:, nq:] / l[:, nq:])
    on = o * lax.rsqrt(jnp.mean(o * o, axis=0, keepdims=True) + EPS) * out_scale
    o_ref[...] = on.T.astype(BF16)


def _diff_attn(lam, qt, kd, proj, bsz, seq, lam_init):
    nq = DIFF_TQ
    nb = seq // nq
    h = DIFF_HEADS
    return pl.pallas_call(
        functools.partial(_diff_kernel, out_scale=1.0 - lam_init),
        grid=(bsz, h, nb),
        in_specs=[pl.BlockSpec(memory_space=pltpu.SMEM),
                  pl.BlockSpec((LANE, nq), lambda b, hh, n: (hh, b * nb + n)),
                  pl.BlockSpec((seq, LANE), lambda b, hh, n: (b, hh)),
                  pl.BlockSpec((seq, LANE), lambda b, hh, n: (b, CB_DV + hh))],
        out_specs=pl.BlockSpec((nq, LANE), lambda b, hh, n: (b * nb + n, hh)),
        out_shape=jax.ShapeDtypeStruct((bsz * seq, D_GROUP), BF16),
        scratch_shapes=[pltpu.VMEM((LANE, 2 * nq), F32), pltpu.VMEM((1, 2 * nq), F32)],
        compiler_params=_cparams(("arbitrary", "arbitrary", "arbitrary")),
        name="diff_attn",
    )(lam, qt, kd, proj)


def _outproj_kernel(ya_ref, yb_ref, yc_ref, yd_ref, w_ref, x_ref, g_ref, xo_ref, h_ref):
    mixed = jnp.concatenate([ya_ref[...], yb_ref[...], yc_ref[...], yd_ref[...]], axis=1)
    x = x_ref[...] + _dot(mixed, w_ref[0])
    xo_ref[...] = x
    h_ref[...] = (_rms_rows(x) * g_ref[...]).astype(BF16)


def _outproj(ys, w3_bf, layer, x2, g, tm):
    t, d = x2.shape
    yspec = pl.BlockSpec((tm, D_GROUP), lambda i: (i, 0))
    row = pl.BlockSpec((tm, d), lambda i: (i, 0))
    return pl.pallas_call(
        _outproj_kernel,
        grid=(t // tm,),
        in_specs=[yspec, yspec, yspec, yspec,
                  pl.BlockSpec((1, d, d), lambda i: (layer, 0, 0)), row,
                  pl.BlockSpec((1, d), lambda i: (0, 0))],
        out_specs=[row, row],
        out_shape=[jax.ShapeDtypeStruct((t, d), F32), jax.ShapeDtypeStruct((t, d), BF16)],
        compiler_params=_cparams(("arbitrary",)),
        name="outproj",
    )(*ys, w3_bf, x2, g)


def _mlp_kernel(h_ref, x_ref, w1_ref, w2_ref, o_ref):
    @pl.when(pl.program_id(1) == 0)
    def _():
        o_ref[...] = x_ref[...]

    a = _dot(h_ref[...], w1_ref[0].astype(BF16))
    a = jnp.square(jnp.maximum(a, 0.0)).astype(BF16)
    o_ref[...] += _dot(a, w2_ref[0].astype(BF16))


def _mlp(h2, x2, w1, w2, layer, tm, tf):
    t, d = x2.shape
    f = w1.shape[2]
    return pl.pallas_call(
        _mlp_kernel,
        grid=(t // tm, f // tf),
        in_specs=[pl.BlockSpec((tm, d), lambda i, j: (i, 0)),
                  pl.BlockSpec((tm, d), lambda i, j: (i, 0), pipeline_mode=pl.Buffered(1)),
                  pl.BlockSpec((1, d, tf), lambda i, j: (layer, 0, j)),
                  pl.BlockSpec((1, tf, d), lambda i, j: (layer, j, 0))],
        out_specs=pl.BlockSpec((tm, d), lambda i, j: (i, 0)),
        out_shape=jax.ShapeDtypeStruct((t, d), F32),
        compiler_params=_cparams(("arbitrary", "arbitrary")),
        name="mlp",
    )(h2, x2, w1, w2)


def _rope_tables(positions):
    bsz, seq = positions.shape
    pos = positions.astype(F32)[..., None]

    def tab(d):
        inv = ROPE_THETA ** (-jnp.arange(0, d, 2, dtype=F32) / d)
        ang = pos * inv
        return jnp.cos(ang), jnp.sin(ang)

    c, s = tab(NSA_DH)
    cos128 = jnp.concatenate([c, c], axis=-1).reshape(bsz * seq, LANE)
    sin128 = jnp.concatenate([-s, s], axis=-1).reshape(bsz * seq, LANE)
    c, s = tab(DIFF_DH)
    cos64 = jnp.concatenate([c, c, c, c], axis=-1).reshape(bsz * seq, LANE)
    sin64 = jnp.concatenate([-s, s, -s, s], axis=-1).reshape(bsz * seq, LANE)
    return cos128, sin128, cos64, sin64


def kernel(x, positions, norm1_g, w_in, s5_lambda_re, s5_lambda_im, s5_log_dt, s5_b_re, s5_b_im, s5_c_re, s5_c_im, s5_d, s5_w_glu, nsa_pe_k, nsa_pe_v, nsa_w_cmp_k, nsa_w_cmp_v, diff_lq1, diff_lk1, diff_lq2, diff_lk2, w_out, norm2_g, mlp_w1, mlp_w2):
    bsz, seq, d = x.shape
    t = bsz * seq
    depth = w_in.shape[0]
    tm = min(ROW_TILE, t)
    cos128, sin128, cos64, sin64 = _rope_tables(positions)
    ret_tabs = _ret_tables()
    x2 = x.reshape(t, d).astype(F32)
    w_in = w_in.astype(BF16)
    w_out_bf = w_out.astype(BF16)
    w_glu_bf = s5_w_glu.astype(BF16)
    wk_bf = nsa_w_cmp_k.astype(BF16)
    wv_bf = nsa_w_cmp_v.astype(BF16)
    s5_tabs = _s5_tables(s5_lambda_re, s5_lambda_im, s5_log_dt, s5_b_re, s5_b_im, s5_c_re, s5_c_im, s5_d)
    for layer in range(depth):
        g1 = norm1_g[layer].astype(F32)[None, :]
        proj = _inproj(x2, g1, w_in, w_in[layer:layer + 1, :, GATE_RAW_END:], layer, min(IN_TM, t), IN_TN)

        y_a = _s5_mixer(proj, bsz, seq, tuple(a[layer] for a in s5_tabs), w_glu_bf[layer])

        y_b = _retention(proj, cos128, sin128, ret_tabs, bsz, seq)

        kcmp, vcmp, ksr, kwr, nqt, ngt = _nsa_prep(proj, cos128, sin128, nsa_pe_k[layer], nsa_pe_v[layer],
                                                   wk_bf[layer], wv_bf[layer], bsz, seq)
        y_c = _nsa_attn(proj, nqt, ngt, kcmp, vcmp, ksr, kwr, bsz, seq)

        lam_init = 0.8 - 0.6 * math.exp(-0.3 * layer)
        lam = (jnp.exp(jnp.sum(diff_lq1[layer].astype(F32) * diff_lk1[layer].astype(F32)))
               - jnp.exp(jnp.sum(diff_lq2[layer].astype(F32) * diff_lk2[layer].astype(F32))) + lam_init)
        qt, kd = _diff_prep(proj, cos64, sin64, tm)
        y_d = _diff_attn(lam.reshape(1, 1).astype(F32), qt, kd, proj, bsz, seq, lam_init)

        x2, h2 = _outproj((y_a, y_b, y_c, y_d), w_out_bf, layer, x2,
                          norm2_g[layer].astype(F32)[None, :], min(OUT_TM, t))
        x2 = _mlp(h2, x2, mlp_w1.astype(F32), mlp_w2.astype(F32), layer, tm, MLP_TF)
    return x2.reshape(bsz, seq, d).astype(x.dtype)
```

```python
import functools
import math

import jax
import jax.numpy as jnp
import numpy as np
from jax import lax
from jax.experimental import pallas as pl
from jax.experimental.pallas import tpu as pltpu

F32 = jnp.float32
BF16 = jnp.bfloat16
HIGHEST = lax.Precision.HIGHEST

D_MODEL = 2048
D_GROUP = 512
ROPE_THETA = 10000.0
EPS = 1e-6
Q_BLOCK = 128
NEG_INF = -1e30

S5_GROUP = 16
S5_P = 64
S5_CHUNK = 16
S5_CORE_GROUPS = 4
S5_TAB_GROUPS = 8

RET_HEADS = 4
RET_DK = 128
RET_CHUNK = 128
RET_UNROLL = 16

NSA_HEADS = 4
NSA_DH = 128
NSA_CMP_LEN = 32
NSA_CMP_STRIDE = 16
NSA_SEL_LEN = 64
NSA_TOPK = 8
NSA_WINDOW = 256
NSA_FORCE_BONUS = 1e4
NSA_SEL_SHIFT = 6
NSA_SEL_TK = 256

DIFF_HEADS = 4
DIFF_DH = 64
DIFF_TQ = 512
DIFF_TK = 256

LANE = 128
GATE_RAW_END = 3852
IN_TN = 512
MAIN_TILES = 8
CB_U, CB_RQ, CB_RK, CB_RV, CB_RG, CB_NQ = 0, 4, 8, 12, 16, 20
CB_NKC, CB_NVC, CB_NKS, CB_NVS, CB_NKW, CB_NVW, CB_GATE = 24, 25, 26, 27, 28, 29, 30
CB_DQ, CB_DK, CB_DV = 32, 36, 40

VMEM_LIMIT = 56 * 1024 * 1024
IN_TM = 2048
IN_VMEM_LIMIT = 60 * 1024 * 1024
ROW_TILE = 1024
MLP_TF = 512
OUT_TM = 512


def _cparams(sem):
    return pltpu.CompilerParams(dimension_semantics=sem, vmem_limit_bytes=VMEM_LIMIT)


def _dot(a, b, **kw):
    return jnp.dot(a, b, preferred_element_type=F32, **kw)


def _dot_nt(a, b):
    return lax.dot_general(a, b, (((1,), (1,)), ((), ())), preferred_element_type=F32)


def _dot_tn(a, b):
    return lax.dot_general(a, b, (((0,), (0,)), ((), ())), preferred_element_type=F32)


def _rms_rows(x):
    return x * lax.rsqrt(jnp.mean(x * x, axis=-1, keepdims=True) + EPS)


def _rope128(x, cos, sin_signed):
    return x * cos + pltpu.roll(x, 64, axis=1) * sin_signed


def _inproj_kernel(x_ref, g_ref, wm_ref, wd_ref, o_ref, xn_ref):
    j = pl.program_id(1)

    @pl.when(j == 0)
    def _():
        x = x_ref[...]
        xn_ref[...] = (_rms_rows(x) * g_ref[...]).astype(BF16)

    @pl.when(j < MAIN_TILES)
    def _():
        o_ref[...] = _dot(xn_ref[...], wm_ref[0])

    @pl.when(j >= MAIN_TILES)
    def _():
        o_ref[...] = _dot(xn_ref[...], wd_ref[0])


def _inproj(x2, g, w3, w_diff, layer, tm, tn):
    t, d = x2.shape
    n_tiles = MAIN_TILES + w_diff.shape[2] // tn
    return pl.pallas_call(
        _inproj_kernel,
        grid=(t // tm, n_tiles),
        in_specs=[pl.BlockSpec((tm, d), lambda i, j: (i, 0)),
                  pl.BlockSpec((1, d), lambda i, j: (0, 0)),
                  pl.BlockSpec((1, d, tn), lambda i, j: (layer, 0, jnp.minimum(j, MAIN_TILES - 1))),
                  pl.BlockSpec((1, d, tn), lambda i, j: (0, 0, jnp.maximum(j - MAIN_TILES, 0)))],
        out_specs=pl.BlockSpec((tm, tn), lambda i, j: (i, j)),
        out_shape=jax.ShapeDtypeStruct((t, n_tiles * tn), F32),
        scratch_shapes=[pltpu.VMEM((tm, d), BF16)],
        compiler_params=pltpu.CompilerParams(dimension_semantics=("arbitrary", "arbitrary"),
                                             vmem_limit_bytes=IN_VMEM_LIMIT),
        name="inproj",
    )(x2, g, w3, w_diff)


def _s5_tab_kernel(*refs):
    for a in range(S5_TAB_GROUPS):
        _s5_tab_group(*[r.at[0, a] for r in refs])


def _s5_tab_group(lre_ref, lim_ref, ldt_ref, b_re_ref, b_im_ref, c_re_ref, c_im_ref,
                  wi_ref, ws_ref, wo_ref, ap_ref):
    tc, p, gs = S5_CHUNK, S5_P, S5_GROUP
    w = tc * gs
    lam_re = lre_ref[...]
    lam_im = lim_ref[...]
    dt = jnp.exp(ldt_ref[...])
    lre = lam_re * dt
    lim = lam_im * dt

    def lbpow(e):
        mag = jnp.exp(lre * e)
        return mag * jnp.cos(lim * e), mag * jnp.sin(lim * e)

    pr, pi = lbpow(lax.broadcasted_iota(jnp.int32, (24, 2 * p), 0).astype(F32))
    lb_re, lb_im = pr[1:2], pi[1:2]
    den = lam_re * lam_re + lam_im * lam_im
    f_re = ((lb_re - 1.0) * lam_re + lb_im * lam_im) / den
    f_im = (lb_im * lam_re - (lb_re - 1.0) * lam_im) / den

    r8 = lax.broadcasted_iota(jnp.int32, (8, 2 * p), 0)
    f8 = jnp.where(r8 == 0, f_re, jnp.where(r8 == 1, f_im, 0.0))
    cols = jnp.concatenate([pr, pi, f8, jnp.zeros((72, 2 * p), F32)], axis=0).T
    pr_c, pi_c = cols[:p, 0:tc], cols[:p, 24:24 + tc]
    f_re_c, f_im_c = cols[:p, 48:49], cols[:p, 49:50]

    b_re = b_re_ref[...]
    b_im = b_im_ref[...]
    bb_re = f_re_c * b_re - f_im_c * b_im
    bb_im = f_re_c * b_im + f_im_c * b_re
    hrow = lax.broadcasted_iota(jnp.int32, (gs, w), 0)
    lane_w = lax.broadcasted_iota(jnp.int32, (gs, w), 1)
    tile_m = jnp.where((lane_w & (gs - 1)) == hrow, 1.0, 0.0)
    rev_m = jnp.where(hrow == (tc - 1) - (lane_w >> 4), 1.0, 0.0)
    bbt_re = _dot(bb_re, tile_m, precision=HIGHEST)
    bbt_im = _dot(bb_im, tile_m, precision=HIGHEST)
    p15_re = _dot(pr_c, rev_m, precision=HIGHEST)
    p15_im = _dot(pi_c, rev_m, precision=HIGHEST)
    ws_ref[...] = jnp.concatenate([p15_re * bbt_re - p15_im * bbt_im,
                                    p15_re * bbt_im + p15_im * bbt_re], axis=0).astype(BF16)

    c_re = c_re_ref[...]
    c_im = c_im_ref[...]
    lo = lax.broadcasted_iota(jnp.int32, (1, 2 * p), 1) < p
    row_a = jnp.where(lo, pr, -pi)
    row_b = jnp.where(lo, -pi, -pr)
    wo_ref[...] = jnp.concatenate([c_re * row_a[t + 1:t + 2] + c_im * row_b[t + 1:t + 2]
                                    for t in range(tc)], axis=0).astype(BF16)

    x_re = jnp.concatenate([c_re * pr[k:k + 1] - c_im * pi[k:k + 1] for k in range(tc)], axis=0)
    x_im = jnp.concatenate([c_re * pi[k:k + 1] + c_im * pr[k:k + 1] for k in range(tc)], axis=0)
    zpad = jnp.zeros((p, w), F32)
    kt = (_dot(x_re, jnp.concatenate([bbt_re, zpad], axis=0), precision=HIGHEST)
          - _dot(x_im, jnp.concatenate([bbt_im, zpad], axis=0), precision=HIGHEST))
    sblk = lane_w >> 4
    rows = []
    for t in range(tc):
        acc = jnp.zeros((gs, w), F32)
        for k in range(t + 1):
            acc = jnp.where(sblk == t - k, kt[k * gs:(k + 1) * gs], acc)
        rows.append(acc)
    wi_ref[...] = jnp.concatenate(rows, axis=0).astype(BF16)

    r16 = lax.broadcasted_iota(jnp.int32, (16, 2 * p), 0)
    sr, si = lbpow(lax.shift_left(jnp.full((16, 2 * p), tc, jnp.int32), r16 >> 1).astype(F32))
    ap_ref[...] = jnp.where((r16 & 1) == 1, jnp.where(lo, -si, si), sr)


def _s5_tables(lam_re, lam_im, log_dt, b_re, b_im, c_re, c_im, d_skip):
    depth, g, p = lam_re.shape
    w = S5_CHUNK * S5_GROUP
    dup = lambda a: jnp.concatenate([a.astype(F32), a.astype(F32)], axis=-1)
    ldt = jnp.broadcast_to(log_dt.astype(F32)[:, :, None, None], (depth, g, 1, 2 * p))
    ng = S5_TAB_GROUPS
    row = pl.BlockSpec((1, ng, 1, 2 * p), lambda l, i: (l, i, 0, 0))
    bspec = pl.BlockSpec((1, ng, p, S5_GROUP), lambda l, i: (l, i, 0, 0))
    cspec = pl.BlockSpec((1, ng, S5_GROUP, 2 * p), lambda l, i: (l, i, 0, 0))
    out = lambda r, c: pl.BlockSpec((1, ng, r, c), lambda l, i: (l, i, 0, 0))
    tabs = pl.pallas_call(
        _s5_tab_kernel,
        grid=(depth, g // ng),
        in_specs=[row, row, row, bspec, bspec, cspec, cspec],
        out_specs=[out(w, w), out(2 * p, w), out(w, 2 * p), out(16, 2 * p)],
        out_shape=[jax.ShapeDtypeStruct((depth, g, w, w), BF16),
                   jax.ShapeDtypeStruct((depth, g, 2 * p, w), BF16),
                   jax.ShapeDtypeStruct((depth, g, w, 2 * p), BF16),
                   jax.ShapeDtypeStruct((depth, g, 16, 2 * p), F32)],
        compiler_params=_cparams(("arbitrary", "arbitrary")),
        name="s5_tables",
    )(dup(lam_re)[:, :, None, :], dup(lam_im)[:, :, None, :], ldt, b_re.astype(F32), b_im.astype(F32),
      dup(c_re), dup(c_im))
    d_row = d_skip.astype(F32).reshape(depth, 1, g * S5_GROUP)
    return (*tabs, d_row)


def _s5_pack_kernel(*refs):
    at_ref = refs[-1]
    nc = at_ref.shape[2]
    for t in range(S5_CHUNK):
        for j, u_ref in enumerate(refs[:-1]):
            at_ref[t, j * LANE:(j + 1) * LANE, :] = u_ref[pl.ds(t, nc, stride=S5_CHUNK), :].T.astype(BF16)


def _u_specs(seq):
    return [pl.BlockSpec((seq, LANE), lambda b, j=j: (b, CB_U + j)) for j in range(D_GROUP // LANE)]


def _s5_pack(proj, bsz, seq):
    nc = seq // S5_CHUNK
    return pl.pallas_call(
        _s5_pack_kernel,
        grid=(bsz,),
        in_specs=_u_specs(seq),
        out_specs=pl.BlockSpec((S5_CHUNK, D_GROUP, nc), lambda b: (0, 0, b)),
        out_shape=jax.ShapeDtypeStruct((S5_CHUNK, D_GROUP, bsz * nc), BF16),
        compiler_params=_cparams(("arbitrary",)),
        name="s5_pack",
    )(*([proj] * (D_GROUP // LANE)))


def _s5_kernel(ut_ref, wi_ref, ws_ref, wo_ref, ap_ref, yt_ref, *, n_chunk):
    tc, r = ut_ref.shape[0], ut_ref.shape[2]
    gs = S5_GROUP
    n_step = int(math.log2(n_chunk))
    for a in range(S5_CORE_GROUPS):
        ut = ut_ref[:, a * gs:(a + 1) * gs, :].reshape(tc * gs, r)
        s = _dot(ws_ref[a], ut)
        lane = lax.broadcasted_iota(jnp.int32, s.shape, 1) & (n_chunk - 1)
        apt = ap_ref[a].T
        x = s
        for k in range(n_step):
            sh = 1 << k
            xs = jnp.where(lane >= sh, pltpu.roll(x, sh, axis=1), 0.0)
            xsw = jnp.concatenate([xs[S5_P:], xs[:S5_P]], axis=0)
            x = x + xs * apt[:, 2 * k:2 * k + 1] + xsw * apt[:, 2 * k + 1:2 * k + 2]
        xp = jnp.where(lane >= 1, pltpu.roll(x, 1, axis=1), 0.0)
        y = _dot(wi_ref[a], ut) + _dot(wo_ref[a], xp.astype(BF16))
        yt_ref[:, a * gs:(a + 1) * gs, :] = y.reshape(tc, gs, r)


def _s5_core(at, tabs, n_chunk):
    tc, ch, r = at.shape
    w_intra_t, w_state_t, w_out_t, apow, _ = tabs
    g = w_intra_t.shape[0]
    w = tc * S5_GROUP
    ng = S5_CORE_GROUPS
    return pl.pallas_call(
        functools.partial(_s5_kernel, n_chunk=n_chunk),
        grid=(g // ng,),
        in_specs=[pl.BlockSpec((tc, ng * S5_GROUP, r), lambda i: (0, i, 0)),
                  pl.BlockSpec((ng, w, w), lambda i: (i, 0, 0)),
                  pl.BlockSpec((ng, 2 * S5_P, w), lambda i: (i, 0, 0)),
                  pl.BlockSpec((ng, w, 2 * S5_P), lambda i: (i, 0, 0)),
                  pl.BlockSpec((ng, 16, 2 * S5_P), lambda i: (i, 0, 0))],
        out_specs=pl.BlockSpec((tc, ng * S5_GROUP, r), lambda i: (0, i, 0)),
        out_shape=jax.ShapeDtypeStruct((tc, ch, r), F32),
        compiler_params=_cparams(("arbitrary",)),
        name="s5_core",
    )(at, w_intra_t, w_state_t, w_out_t, apow)


def _s5_out_kernel(yt_ref, u0_ref, u1_ref, u2_ref, u3_ref, d_ref, wg_ref, o_ref, y_scr):
    nc = yt_ref.shape[2]
    for t in range(S5_CHUNK):
        rows = pl.ds(t, nc, stride=S5_CHUNK)
        for j, u_ref in enumerate((u0_ref, u1_ref, u2_ref, u3_ref)):
            sl = slice(j * LANE, (j + 1) * LANE)
            y_scr[j, rows, :] = jax.nn.gelu(yt_ref[t, sl, :].T + d_ref[:, sl] * u_ref[rows, :])
    y = jnp.concatenate([y_scr[j] for j in range(D_GROUP // LANE)], axis=1)
    o_ref[...] = (y * jax.nn.sigmoid(_dot(y.astype(BF16), wg_ref[...]))).astype(BF16)


def _s5_out(yt, proj, d_row, w_glu_bf, bsz, seq):
    nc = seq // S5_CHUNK
    return pl.pallas_call(
        _s5_out_kernel,
        grid=(bsz,),
        in_specs=[pl.BlockSpec((S5_CHUNK, D_GROUP, nc), lambda b: (0, 0, b))] + _u_specs(seq)
        + [pl.BlockSpec((1, D_GROUP), lambda b: (0, 0)),
           pl.BlockSpec((D_GROUP, D_GROUP), lambda b: (0, 0))],
        out_specs=pl.BlockSpec((seq, D_GROUP), lambda b: (b, 0)),
        out_shape=jax.ShapeDtypeStruct((bsz * seq, D_GROUP), BF16),
        scratch_shapes=[pltpu.VMEM((D_GROUP // LANE, seq, LANE), F32)],
        compiler_params=_cparams(("arbitrary",)),
        name="s5_out",
    )(yt, proj, proj, proj, proj, d_row, w_glu_bf)


def _s5_mixer(proj, bsz, seq, tabs, w_glu_bf):
    yt = _s5_core(_s5_pack(proj, bsz, seq), tabs, seq // S5_CHUNK)
    return _s5_out(yt, proj, tabs[4], w_glu_bf, bsz, seq)


def _ret_tables():
    h, c = RET_HEADS, RET_CHUNK
    log_g = jnp.log(1.0 - 2.0 ** (-5.0 - jnp.arange(h, dtype=F32)))
    idx = jnp.arange(c, dtype=F32)
    diff = idx[:, None] - idx[None, :]
    dmat = jnp.where(diff >= 0, jnp.exp(jnp.maximum(diff, 0.0)[None] * log_g[:, None, None]), 0.0)
    zeta = jnp.exp((c - 1.0 - idx)[None, :] * log_g[:, None])
    xi = jnp.exp((idx + 1.0)[None, :] * log_g[:, None])
    g_chunk = jnp.exp(c * log_g)
    zeta_b = jnp.broadcast_to(zeta[:, :, None], (h, c, RET_DK))
    xi_b = jnp.broadcast_to(xi[:, :, None], (h, c, RET_DK))
    gch_b = jnp.broadcast_to(g_chunk[:, None, None], (h, 8, RET_DK))
    return dmat, zeta_b, xi_b, gch_b


def _ret_kernel(q_ref, k_ref, v_ref, g_ref, cos_ref, sin_ref, dm_ref, ze_ref, xi_ref, gc_ref,
                o_ref, *, n_chunk):
    c = RET_CHUNK
    dm = dm_ref[0]
    ze = ze_ref[0]
    xi = xi_ref[0]
    gch = gc_ref[0, 0:1, :]

    def body(n, r):
        sl = pl.ds(pl.multiple_of(n * c, c), c)
        cos = cos_ref[sl, :]
        sin = sin_ref[sl, :]
        qc = _rope128(q_ref[sl, :], cos, sin)
        kc = _rope128(k_ref[sl, :], cos, sin) * (RET_DK ** -0.5)
        vc = v_ref[sl, :].astype(BF16)
        inner = _dot_nt(qc.astype(BF16), kc.astype(BF16)) * dm
        o = _dot(inner.astype(BF16), vc) + _dot((qc * xi).astype(BF16), r.astype(BF16))
        kv = _dot_tn((kc * ze).astype(BF16), vc)
        r = r * gch + kv
        mu = jnp.mean(o, axis=-1, keepdims=True)
        oc = o - mu
        var = jnp.mean(oc * oc, axis=-1, keepdims=True)
        on = oc * lax.rsqrt(var + EPS)
        gg = g_ref[sl, :]
        o_ref[sl, :] = (gg * jax.nn.sigmoid(gg) * on).astype(BF16)
        return r

    lax.fori_loop(0, n_chunk, body, jnp.zeros((RET_DK, RET_DK), F32), unroll=RET_UNROLL)


def _retention(proj, cos128, sin128, tabs, bsz, seq):
    dmat, zeta_b, xi_b, gch_b = tabs
    h = RET_HEADS
    col = lambda cb: pl.BlockSpec((seq, LANE), lambda b, hh, cb=cb: (b, cb + hh))
    tab = pl.BlockSpec((seq, LANE), lambda b, hh: (b, 0))
    head = lambda r: pl.BlockSpec((1, r, LANE), lambda b, hh: (hh, 0, 0))
    return pl.pallas_call(
        functools.partial(_ret_kernel, n_chunk=seq // RET_CHUNK),
        grid=(bsz, h),
        in_specs=[col(CB_RQ), col(CB_RK), col(CB_RV), col(CB_RG), tab, tab,
                  head(RET_CHUNK), head(RET_CHUNK), head(RET_CHUNK), head(8)],
        out_specs=pl.BlockSpec((seq, LANE), lambda b, hh: (b, hh)),
        out_shape=jax.ShapeDtypeStruct((bsz * seq, D_GROUP), BF16),
        compiler_params=_cparams(("arbitrary", "arbitrary")),
        name="retention",
    )(proj, proj, proj, proj, cos128, sin128, dmat, zeta_b, xi_b, gch_b)


def _nsa_prep_kernel(kc_ref, vc_ref, pek_ref, pev_ref, wk_ref, wv_ref, ks_ref, kw_ref, cos_ref, sin_ref,
                     q_ref, g_ref, kcmp_ref, vcmp_ref, ksr_ref, kwr_ref, qt_ref, gt_ref):
    half = NSA_CMP_STRIDE * NSA_DH
    for h in range(NSA_HEADS):
        sl = slice(h * NSA_DH, (h + 1) * NSA_DH)
        qt_ref[sl, :] = _rope128(_rms_rows(q_ref[:, sl]), cos_ref[...], sin_ref[...]).T.astype(BF16)
    gt_ref[...] = jax.nn.sigmoid(g_ref[...]).T
    n16 = kcmp_ref.shape[1]
    kparts, vparts = [], []
    for i in range(NSA_CMP_STRIDE):
        rows = pl.ds(i, n16, stride=NSA_CMP_STRIDE)
        kparts.append(_rope128(kc_ref[rows, :], cos_ref[rows, :], sin_ref[rows, :]))
        vparts.append(vc_ref[rows, :])
    xr = jnp.concatenate(kparts, axis=1)
    xv = jnp.concatenate(vparts, axis=1)
    a = _dot((xr + pek_ref[0:1, :]).astype(BF16), wk_ref[0:half, :])
    b = _dot((xr + pek_ref[1:2, :]).astype(BF16), wk_ref[half:2 * half, :])
    kcm = _rms_rows(a + pltpu.roll(b, n16 - 1, axis=0))
    rowi = lax.broadcasted_iota(jnp.int32, kcm.shape, 0)
    kcmp_ref[0] = jnp.where(rowi < n16 - 1, kcm, 0.0).astype(BF16)
    av = _dot((xv + pev_ref[0:1, :]).astype(BF16), wv_ref[0:half, :])
    bv = _dot((xv + pev_ref[1:2, :]).astype(BF16), wv_ref[half:2 * half, :])
    vcm = av + pltpu.roll(bv, n16 - 1, axis=0)
    vcmp_ref[0] = jnp.where(rowi < n16 - 1, vcm, 0.0).astype(BF16)
    cos = cos_ref[...]
    sin = sin_ref[...]
    ksr_ref[...] = _rope128(_rms_rows(ks_ref[...]), cos, sin).astype(BF16)
    kwr_ref[...] = _rope128(_rms_rows(kw_ref[...]), cos, sin).astype(BF16)


def _nsa_prep(proj, cos128, sin128, pe_k, pe_v, wk_bf, wv_bf, bsz, seq):
    n16 = seq // NSA_CMP_STRIDE
    half = NSA_CMP_STRIDE * NSA_DH
    pek = pe_k.astype(F32).reshape(2, half)
    pev = pe_v.astype(F32).reshape(2, half)
    full = lambda a: pl.BlockSpec(a.shape, lambda b: (0,) * a.ndim)
    col = lambda cb: pl.BlockSpec((seq, LANE), lambda b, cb=cb: (b, cb))
    tab = pl.BlockSpec((seq, LANE), lambda b: (b, 0))
    cmp_spec = pl.BlockSpec((1, n16, NSA_DH), lambda b: (b, 0, 0))
    seq_spec = pl.BlockSpec((seq, LANE), lambda b: (b, 0))
    return pl.pallas_call(
        _nsa_prep_kernel,
        grid=(bsz,),
        in_specs=[col(CB_NKC), col(CB_NVC), full(pek), full(pev), full(wk_bf), full(wv_bf),
                  col(CB_NKS), col(CB_NKW), tab, tab,
                  pl.BlockSpec((seq, D_GROUP), lambda b: (b, CB_NQ // 4)), col(CB_GATE)],
        out_specs=[cmp_spec, cmp_spec, seq_spec, seq_spec,
                   pl.BlockSpec((D_GROUP, seq), lambda b: (0, b)), pl.BlockSpec((LANE, seq), lambda b: (0, b))],
        out_shape=[jax.ShapeDtypeStruct((bsz, n16, NSA_DH), BF16),
                   jax.ShapeDtypeStruct((bsz, n16, NSA_DH), BF16),
                   jax.ShapeDtypeStruct((bsz * seq, NSA_DH), BF16),
                   jax.ShapeDtypeStruct((bsz * seq, NSA_DH), BF16),
                   jax.ShapeDtypeStruct((D_GROUP, bsz * seq), BF16),
                   jax.ShapeDtypeStruct((LANE, bsz * seq), F32)],
        compiler_params=_cparams(("arbitrary",)),
        name="nsa_prep",
    )(proj, proj, pek, pev, wk_bf, wv_bf, proj, proj, cos128, sin128, proj, proj)


def _attn_finish(l, acc):
    return jnp.where(l > 0.0, acc / jnp.where(l > 0.0, l, 1.0), 0.0)


def _nsa_kernel(qt_ref, gt_ref, kc_ref, vc_ref, ks_ref, vs_ref, kw_ref, vw_ref, ov_ref, o_ref):
    hh, qb, dh = NSA_HEADS, Q_BLOCK, NSA_DH
    scale = dh ** -0.5
    n = pl.program_id(1)
    qq = jnp.concatenate([qt_ref[h * dh:(h + 1) * dh, :] for h in range(hh)], axis=1)

    def branch(keys, vals, maskf):
        s = _dot(keys, qq) * scale
        visible = maskf > 0.5
        e = jnp.concatenate([jnp.where(visible, jnp.exp(s[:, h * qb:(h + 1) * qb]), 0.0) for h in range(hh)],
                            axis=1)
        return jnp.sum(e, axis=0, keepdims=True), _dot_tn(vals, e.astype(BF16)), e

    def qpos(rows):
        return n * qb + lax.broadcasted_iota(jnp.int32, (rows, qb), 1)

    def krow(rows):
        return lax.broadcasted_iota(jnp.int32, (rows, qb), 0)

    ncp = kc_ref.shape[1]
    vis = jnp.where(krow(ncp) * NSA_CMP_STRIDE + (NSA_CMP_LEN - 1) <= qpos(ncp), 1.0, 0.0)
    l_c, acc_c, e_c = branch(kc_ref[0], vc_ref[0], vis)
    o_cmp = _attn_finish(l_c, acc_c)
    p = e_c / jnp.where(l_c > 0.0, l_c, 1.0)
    psum = p[:, 0:qb]
    for h in range(1, hh):
        psum = psum + p[:, h * qb:(h + 1) * qb]

    n_sel = ks_ref.shape[0] // NSA_SEL_LEN
    imp = _dot(ov_ref[...], psum, precision=HIGHEST)[:n_sel]
    j = krow(n_sel)
    cur = qpos(n_sel) >> NSA_SEL_SHIFT
    forced = (j == 0) | (j == cur) | (j == cur - 1)
    score = jnp.where(j <= cur, imp + jnp.where(forced, NSA_FORCE_BONUS, 0.0), NEG_INF)
    rank = jnp.zeros((n_sel, qb), F32)
    for jp in range(n_sel):
        cj = score[jp:jp + 1, :]
        tie = jnp.where(j > jp, 1.0, 0.0)
        rank = rank + jnp.where(cj > score, 1.0, jnp.where(cj == score, tie, 0.0))
    selm = jnp.where(rank < float(min(NSA_TOPK, n_sel)), 1.0, 0.0)
    selm = jnp.concatenate([selm, jnp.zeros((LANE - n_sel, qb), F32)], axis=0).astype(BF16)

    tk = NSA_SEL_TK
    blk = lax.broadcasted_iota(jnp.int32, (tk, LANE), 0) >> NSA_SEL_SHIFT
    selcol = lax.broadcasted_iota(jnp.int32, (tk, LANE), 1)

    def sel_tile(kt):
        sl = pl.ds(pl.multiple_of(kt * tk, tk), tk)
        expand = jnp.where(selcol == kt * (tk // NSA_SEL_LEN) + blk, 1.0, 0.0).astype(BF16)
        picked = _dot(expand, selm)
        maskf = jnp.where(kt * tk + krow(tk) <= qpos(tk), picked, 0.0)
        l, acc, _ = branch(ks_ref[sl, :], vs_ref[sl, :].astype(BF16), maskf)
        return l, acc

    def sel_body(i, carry):
        l0, a0 = sel_tile(2 * i)
        l1, a1 = sel_tile(2 * i + 1)
        return carry[0] + (l0 + l1), carry[1] + (a0 + a1)

    zero = (jnp.zeros((1, hh * qb), F32), jnp.zeros((dh, hh * qb), F32))
    trips = (n * qb + qb + 2 * tk - 1) // (2 * tk)
    o_sel = _attn_finish(*lax.fori_loop(0, trips, sel_body, zero))

    wk = NSA_WINDOW + qb
    start = jnp.maximum(n * qb - NSA_WINDOW, 0)
    sl = pl.ds(pl.multiple_of(start, qb), wk)
    dist = qpos(wk) - (start + krow(wk))
    inwin = jnp.where(dist >= 0, jnp.where(dist < NSA_WINDOW, 1.0, 0.0), 0.0)
    l_w, acc_w, _ = branch(kw_ref[sl, :], vw_ref[sl, :].astype(BF16), inwin)
    o_win = _attn_finish(l_w, acc_w)

    gt = gt_ref[...]
    for h in range(hh):
        hq = slice(h * qb, (h + 1) * qb)
        o = (gt[3 * h:3 * h + 1, :] * o_cmp[:, hq] + gt[3 * h + 1:3 * h + 2, :] * o_sel[:, hq]
             + gt[3 * h + 2:3 * h + 3, :] * o_win[:, hq])
        o_ref[:, h * dh:(h + 1) * dh] = o.T.astype(BF16)


def _nsa_overlap_table(seq):
    n_cmp = (seq - NSA_CMP_LEN) // NSA_CMP_STRIDE + 1
    n_sel = seq // NSA_SEL_LEN
    cmp_start = np.arange(n_cmp) * NSA_CMP_STRIDE
    sel_start = np.arange(n_sel) * NSA_SEL_LEN
    overlap = ((cmp_start[None, :] < sel_start[:, None] + NSA_SEL_LEN)
               & (cmp_start[None, :] + NSA_CMP_LEN > sel_start[:, None])).astype(np.float32)
    tab = np.zeros((LANE, n_cmp + 1), np.float32)
    tab[:n_sel, :n_cmp] = overlap
    return jnp.asarray(tab)


def _nsa_attn(proj, qt, gt, kcmp, vcmp, ksr, kwr, bsz, seq):
    nb = seq // Q_BLOCK
    cmp_spec = pl.BlockSpec((1, kcmp.shape[1], NSA_DH), lambda b, n: (b, 0, 0))
    kseq = pl.BlockSpec((seq, LANE), lambda b, n: (b, 0))
    vcol = lambda cb: pl.BlockSpec((seq, LANE), lambda b, n, cb=cb: (b, cb))
    ov = _nsa_overlap_table(seq)
    return pl.pallas_call(
        _nsa_kernel,
        grid=(bsz, nb),
        in_specs=[pl.BlockSpec((D_GROUP, Q_BLOCK), lambda b, n: (0, b * nb + n)),
                  pl.BlockSpec((LANE, Q_BLOCK), lambda b, n: (0, b * nb + n)),
                  cmp_spec, cmp_spec, kseq, vcol(CB_NVS), kseq, vcol(CB_NVW),
                  pl.BlockSpec(ov.shape, lambda b, n: (0, 0))],
        out_specs=pl.BlockSpec((Q_BLOCK, D_GROUP), lambda b, n: (b * nb + n, 0)),
        out_shape=jax.ShapeDtypeStruct((bsz * seq, D_GROUP), BF16),
        compiler_params=_cparams(("arbitrary", "arbitrary")),
        name="nsa_attn",
    )(qt, gt, kcmp, vcmp, ksr, proj, kwr, proj, ov)


def _diff_prep_kernel(q_ref, k_ref, cos_ref, sin_ref, qt_ref, ko_ref):
    cos = cos_ref[...]
    sin = sin_ref[...]
    lane = lax.broadcasted_iota(jnp.int32, cos.shape, 1)
    lo = lane < DIFF_DH
    first = (lane & (DIFF_DH - 1)) < DIFF_DH // 2

    def prep(x):
        x2 = x * x
        ss_lo = jnp.sum(jnp.where(lo, x2, 0.0), axis=-1, keepdims=True)
        ss_hi = jnp.sum(jnp.where(lo, 0.0, x2), axis=-1, keepdims=True)
        ms = jnp.where(lo, ss_lo, ss_hi) * (1.0 / DIFF_DH)
        xn = x * lax.rsqrt(ms + EPS)
        partner = jnp.where(first, pltpu.roll(xn, LANE - DIFF_DH // 2, axis=1),
                            pltpu.roll(xn, DIFF_DH // 2, axis=1))
        return xn * cos + partner * sin

    for h in range(DIFF_HEADS):
        sl = slice(h * LANE, (h + 1) * LANE)
        qt_ref[sl, :] = (prep(q_ref[:, sl]) * (DIFF_DH ** -0.5)).T.astype(BF16)
        ko_ref[:, sl] = prep(k_ref[:, sl]).astype(BF16)


def _diff_prep(proj, cos64, sin64, tm):
    t = proj.shape[0]
    blk = lambda cb: pl.BlockSpec((tm, D_GROUP), lambda i, cb=cb: (i, cb // 4))
    tab = pl.BlockSpec((tm, LANE), lambda i: (i, 0))
    return pl.pallas_call(
        _diff_prep_kernel,
        grid=(t // tm,),
        in_specs=[blk(CB_DQ), blk(CB_DK), tab, tab],
        out_specs=[pl.BlockSpec((D_GROUP, tm), lambda i: (0, i)),
                   pl.BlockSpec((tm, D_GROUP), lambda i: (i, 0))],
        out_shape=[jax.ShapeDtypeStruct((D_GROUP, t), BF16), jax.ShapeDtypeStruct((t, D_GROUP), BF16)],
        compiler_params=_cparams(("arbitrary",)),
        name="diff_prep",
    )(proj, proj, cos64, sin64)


def _diff_kernel(lam_ref, qt_ref, k_ref, v_ref, o_ref, acc_ref, l_ref, *, out_scale):
    nq = DIFF_TQ
    qi = pl.program_id(2)
    qt = qt_ref[...]
    row = lax.broadcasted_iota(jnp.int32, qt.shape, 0)
    zero = jnp.zeros_like(qt)
    qq = jnp.concatenate([jnp.where(row < DIFF_DH, qt, zero), jnp.where(row < DIFF_DH, zero, qt)], axis=1)
    acc_ref[...] = jnp.zeros_like(acc_ref)
    l_ref[...] = jnp.zeros_like(l_ref)

    tk = DIFF_TK
    per_q = nq // tk

    def tile(kt, masked):
        sl = pl.ds(pl.multiple_of(kt * tk, tk), tk)
        p = jnp.exp(_dot(k_ref[sl, :], qq))
        if masked:
            kpos = kt * tk + lax.broadcasted_iota(jnp.int32, p.shape, 0)
            qpos = qi * nq + (lax.broadcasted_iota(jnp.int32, p.shape, 1) & (nq - 1))
            p = jnp.where(kpos <= qpos, p, 0.0)
        return jnp.sum(p, axis=0, keepdims=True), _dot_tn(v_ref[sl, :].astype(BF16), p.astype(BF16))

    def body(i, c):
        l0, a0 = tile(2 * i, False)
        l1, a1 = tile(2 * i + 1, False)
        l_ref[...] += l0 + l1
        acc_ref[...] += a0 + a1
        return c

    lax.fori_loop(0, qi * (per_q // 2), body, 0)
    for d in range(per_q):
        l0, a0 = tile(qi * per_q + d, True)
        l_ref[...] += l0
        acc_ref[...] += a0
    l = l_ref[...]
    acc = acc_ref[...]
    o = acc[:, :nq] / l[:, :nq] - lam_ref[0, 0] * (acc[:, nq:] / l[:, nq:])
    on = o * lax.rsqrt(jnp.mean(o * o, axis=0, keepdims=True) + EPS) * out_scale
    o_ref[...] = on.T.astype(BF16)


def _diff_attn(lam, qt, kd, proj, bsz, seq, lam_init):
    nq = DIFF_TQ
    nb = seq // nq
    h = DIFF_HEADS
    return pl.pallas_call(
        functools.partial(_diff_kernel, out_scale=1.0 - lam_init),
        grid=(bsz, h, nb),
        in_specs=[pl.BlockSpec(memory_space=pltpu.SMEM),
                  pl.BlockSpec((LANE, nq), lambda b, hh, n: (hh, b * nb + n)),
                  pl.BlockSpec((seq, LANE), lambda b, hh, n: (b, hh)),
                  pl.BlockSpec((seq, LANE), lambda b, hh, n: (b, CB_DV + hh))],
        out_specs=pl.BlockSpec((nq, LANE), lambda b, hh, n: (b * nb + n, hh)),
        out_shape=jax.ShapeDtypeStruct((bsz * seq, D_GROUP), BF16),
        scratch_shapes=[pltpu.VMEM((LANE, 2 * nq), F32), pltpu.VMEM((1, 2 * nq), F32)],
        compiler_params=_cparams(("arbitrary", "arbitrary", "arbitrary")),
        name="diff_attn",
    )(lam, qt, kd, proj)


def _outproj_kernel(ya_ref, yb_ref, yc_ref, yd_ref, w_ref, x_ref, g_ref, xo_ref, h_ref):
    mixed = jnp.concatenate([ya_ref[...], yb_ref[...], yc_ref[...], yd_ref[...]], axis=1)
    x = x_ref[...] + _dot(mixed, w_ref[0])
    xo_ref[...] = x
    h_ref[...] = (_rms_rows(x) * g_ref[...]).astype(BF16)


def _outproj(ys, w3_bf, layer, x2, g, tm):
    t, d = x2.shape
    yspec = pl.BlockSpec((tm, D_GROUP), lambda i: (i, 0))
    row = pl.BlockSpec((tm, d), lambda i: (i, 0))
    return pl.pallas_call(
        _outproj_kernel,
        grid=(t // tm,),
        in_specs=[yspec, yspec, yspec, yspec,
                  pl.BlockSpec((1, d, d), lambda i: (layer, 0, 0)), row,
                  pl.BlockSpec((1, d), lambda i: (0, 0))],
        out_specs=[row, row],
        out_shape=[jax.ShapeDtypeStruct((t, d), F32), jax.ShapeDtypeStruct((t, d), BF16)],
        compiler_params=_cparams(("arbitrary",)),
        name="outproj",
    )(*ys, w3_bf, x2, g)


def _mlp_kernel(h_ref, x_ref, w1_ref, w2_ref, o_ref):
    @pl.when(pl.program_id(1) == 0)
    def _():
        o_ref[...] = x_ref[...]

    a = _dot(h_ref[...], w1_ref[0].astype(BF16))
    a = jnp.square(jnp.maximum(a, 0.0)).astype(BF16)
    o_ref[...] += _dot(a, w2_ref[0].astype(BF16))


def _mlp(h2, x2, w1, w2, layer, tm, tf):
    t, d = x2.shape
    f = w1.shape[2]
    return pl.pallas_call(
        _mlp_kernel,
        grid=(t // tm, f // tf),
        in_specs=[pl.BlockSpec((tm, d), lambda i, j: (i, 0)),
                  pl.BlockSpec((tm, d), lambda i, j: (i, 0), pipeline_mode=pl.Buffered(1)),
                  pl.BlockSpec((1, d, tf), lambda i, j: (layer, 0, j)),
                  pl.BlockSpec((1, tf, d), lambda i, j: (layer, j, 0))],
        out_specs=pl.BlockSpec((tm, d), lambda i, j: (i, 0)),
        out_shape=jax.ShapeDtypeStruct((t, d), F32),
        compiler_params=_cparams(("arbitrary", "arbitrary")),
        name="mlp",
    )(h2, x2, w1, w2)


def _rope_tables(positions):
    bsz, seq = positions.shape
    pos = positions.astype(F32)[..., None]

    def tab(d):
        inv = ROPE_THETA ** (-jnp.arange(0, d, 2, dtype=F32) / d)
        ang = pos * inv
        return jnp.cos(ang), jnp.sin(ang)

    c, s = tab(NSA_DH)
    cos128 = jnp.concatenate([c, c], axis=-1).reshape(bsz * seq, LANE)
    sin128 = jnp.concatenate([-s, s], axis=-1).reshape(bsz * seq, LANE)
    c, s = tab(DIFF_DH)
    cos64 = jnp.concatenate([c, c, c, c], axis=-1).reshape(bsz * seq, LANE)
    sin64 = jnp.concatenate([-s, s, -s, s], axis=-1).reshape(bsz * seq, LANE)
    return cos128, sin128, cos64, sin64


def kernel(x, positions, norm1_g, w_in, s5_lambda_re, s5_lambda_im, s5_log_dt, s5_b_re, s5_b_im, s5_c_re, s5_c_im, s5_d, s5_w_glu, nsa_pe_k, nsa_pe_v, nsa_w_cmp_k, nsa_w_cmp_v, diff_lq1, diff_lk1, diff_lq2, diff_lk2, w_out, norm2_g, mlp_w1, mlp_w2):
    bsz, seq, d = x.shape
    t = bsz * seq
    depth = w_in.shape[0]
    tm = min(ROW_TILE, t)
    cos128, sin128, cos64, sin64 = _rope_tables(positions)
    ret_tabs = _ret_tables()
    x2 = x.reshape(t, d).astype(F32)
    w_in = w_in.astype(BF16)
    w_out_bf = w_out.astype(BF16)
    w_glu_bf = s5_w_glu.astype(BF16)
    wk_bf = nsa_w_cmp_k.astype(BF16)
    wv_bf = nsa_w_cmp_v.astype(BF16)
    s5_tabs = _s5_tables(s5_lambda_re, s5_lambda_im, s5_log_dt, s5_b_re, s5_b_im, s5_c_re, s5_c_im, s5_d)
    for layer in range(depth):
        g1 = norm1_g[layer].astype(F32)[None, :]
        proj = _inproj(x2, g1, w_in, w_in[layer:layer + 1, :, GATE_RAW_END:], layer, min(IN_TM, t), IN_TN)

        y_a = _s5_mixer(proj, bsz, seq, tuple(a[layer] for a in s5_tabs), w_glu_bf[layer])

        y_b = _retention(proj, cos128, sin128, ret_tabs, bsz, seq)

        kcmp, vcmp, ksr, kwr, nqt, ngt = _nsa_prep(proj, cos128, sin128, nsa_pe_k[layer], nsa_pe_v[layer],
                                                   wk_bf[layer], wv_bf[layer], bsz, seq)
        y_c = _nsa_attn(proj, nqt, ngt, kcmp, vcmp, ksr, kwr, bsz, seq)

        lam_init = 0.8 - 0.6 * math.exp(-0.3 * layer)
        lam = (jnp.exp(jnp.sum(diff_lq1[layer].astype(F32) * diff_lk1[layer].astype(F32)))
               - jnp.exp(jnp.sum(diff_lq2[layer].astype(F32) * diff_lk2[layer].astype(F32))) + lam_init)
        qt, kd = _diff_prep(proj, cos64, sin64, tm)
        y_d = _diff_attn(lam.reshape(1, 1).astype(F32), qt, kd, proj, bsz, seq, lam_init)

        x2, h2 = _outproj((y_a, y_b, y_c, y_d), w_out_bf, layer, x2,
                          norm2_g[layer].astype(F32)[None, :], min(OUT_TM, t))
        x2 = _mlp(h2, x2, mlp_w1.astype(F32), mlp_w2.astype(F32), layer, tm, MLP_TF)
    return x2.reshape(bsz, seq, d).astype(x.dtype)
```

```python
import functools
import math

import jax
import jax.numpy as jnp
import numpy as np
from jax import lax
from jax.experimental import pallas as pl
from jax.experimental.pallas import tpu as pltpu

F32 = jnp.float32
BF16 = jnp.bfloat16
HIGHEST = lax.Precision.HIGHEST

D_MODEL = 2048
D_GROUP = 512
ROPE_THETA = 10000.0
EPS = 1e-6
Q_BLOCK = 128
NEG_INF = -1e30

S5_GROUP = 16
S5_P = 64
S5_CHUNK = 16
S5_CORE_GROUPS = 4
S5_TAB_GROUPS = 8

RET_HEADS = 4
RET_DK = 128
RET_CHUNK = 128
RET_UNROLL = 16

NSA_HEADS = 4
NSA_DH = 128
NSA_CMP_LEN = 32
NSA_CMP_STRIDE = 16
NSA_SEL_LEN = 64
NSA_TOPK = 8
NSA_WINDOW = 256
NSA_FORCE_BONUS = 1e4
NSA_SEL_SHIFT = 6
NSA_SEL_TK = 256

DIFF_HEADS = 4
DIFF_DH = 64
DIFF_TQ = 512
DIFF_TK = 256

LANE = 128
GATE_RAW_END = 3852
IN_TN = 512
MAIN_TILES = 8
CB_U, CB_RQ, CB_RK, CB_RV, CB_RG, CB_NQ = 0, 4, 8, 12, 16, 20
CB_NKC, CB_NVC, CB_NKS, CB_NVS, CB_NKW, CB_NVW, CB_GATE = 24, 25, 26, 27, 28, 29, 30
CB_DQ, CB_DK, CB_DV = 32, 36, 40

VMEM_LIMIT = 56 * 1024 * 1024
IN_TM = 2048
IN_VMEM_LIMIT = 60 * 1024 * 1024
ROW_TILE = 1024
MLP_TF = 512
OUT_TM = 512


def _cparams(sem):
    return pltpu.CompilerParams(dimension_semantics=sem, vmem_limit_bytes=VMEM_LIMIT)


def _dot(a, b, **kw):
    return jnp.dot(a, b, preferred_element_type=F32, **kw)


def _dot_nt(a, b):
    return lax.dot_general(a, b, (((1,), (1,)), ((), ())), preferred_element_type=F32)


def _dot_tn(a, b):
    return lax.dot_general(a, b, (((0,), (0,)), ((), ())), preferred_element_type=F32)


def _rms_rows(x):
    return x * lax.rsqrt(jnp.mean(x * x, axis=-1, keepdims=True) + EPS)


def _rope128(x, cos, sin_signed):
    return x * cos + pltpu.roll(x, 64, axis=1) * sin_signed


def _inproj_kernel(x_ref, g_ref, wm_ref, wd_ref, o_ref, xn_ref):
    j = pl.program_id(1)

    @pl.when(j == 0)
    def _():
        x = x_ref[...]
        xn_ref[...] = (_rms_rows(x) * g_ref[...]).astype(BF16)

    @pl.when(j < MAIN_TILES)
    def _():
        o_ref[...] = _dot(xn_ref[...], wm_ref[0])

    @pl.when(j >= MAIN_TILES)
    def _():
        o_ref[...] = _dot(xn_ref[...], wd_ref[0])


def _inproj(x2, g, w3, w_diff, layer, tm, tn):
    t, d = x2.shape
    n_tiles = MAIN_TILES + w_diff.shape[2] // tn
    return pl.pallas_call(
        _inproj_kernel,
        grid=(t // tm, n_tiles),
        in_specs=[pl.BlockSpec((tm, d), lambda i, j: (i, 0)),
                  pl.BlockSpec((1, d), lambda i, j: (0, 0)),
                  pl.BlockSpec((1, d, tn), lambda i, j: (layer, 0, jnp.minimum(j, MAIN_TILES - 1))),
                  pl.BlockSpec((1, d, tn), lambda i, j: (0, 0, jnp.maximum(j - MAIN_TILES, 0)))],
        out_specs=pl.BlockSpec((tm, tn), lambda i, j: (i, j)),
        out_shape=jax.ShapeDtypeStruct((t, n_tiles * tn), F32),
        scratch_shapes=[pltpu.VMEM((tm, d), BF16)],
        compiler_params=pltpu.CompilerParams(dimension_semantics=("arbitrary", "arbitrary"),
                                             vmem_limit_bytes=IN_VMEM_LIMIT),
        name="inproj",
    )(x2, g, w3, w_diff)


def _s5_tab_kernel(*refs):
    for a in range(S5_TAB_GROUPS):
        _s5_tab_group(*[r.at[0, a] for r in refs])


def _s5_tab_group(lre_ref, lim_ref, ldt_ref, b_re_ref, b_im_ref, c_re_ref, c_im_ref,
                  wi_ref, ws_ref, wo_ref, ap_ref):
    tc, p, gs = S5_CHUNK, S5_P, S5_GROUP
    w = tc * gs
    lam_re = lre_ref[...]
    lam_im = lim_ref[...]
    dt = jnp.exp(ldt_ref[...])
    lre = lam_re * dt
    lim = lam_im * dt

    def lbpow(e):
        mag = jnp.exp(lre * e)
        return mag * jnp.cos(lim * e), mag * jnp.sin(lim * e)

    pr, pi = lbpow(lax.broadcasted_iota(jnp.int32, (24, 2 * p), 0).astype(F32))
    lb_re, lb_im = pr[1:2], pi[1:2]
    den = lam_re * lam_re + lam_im * lam_im
    f_re = ((lb_re - 1.0) * lam_re + lb_im * lam_im) / den
    f_im = (lb_im * lam_re - (lb_re - 1.0) * lam_im) / den

    r8 = lax.broadcasted_iota(jnp.int32, (8, 2 * p), 0)
    f8 = jnp.where(r8 == 0, f_re, jnp.where(r8 == 1, f_im, 0.0))
    cols = jnp.concatenate([pr, pi, f8, jnp.zeros((72, 2 * p), F32)], axis=0).T
    pr_c, pi_c = cols[:p, 0:tc], cols[:p, 24:24 + tc]
    f_re_c, f_im_c = cols[:p, 48:49], cols[:p, 49:50]

    b_re = b_re_ref[...]
    b_im = b_im_ref[...]
    bb_re = f_re_c * b_re - f_im_c * b_im
    bb_im = f_re_c * b_im + f_im_c * b_re
    hrow = lax.broadcasted_iota(jnp.int32, (gs, w), 0)
    lane_w = lax.broadcasted_iota(jnp.int32, (gs, w), 1)
    tile_m = jnp.where((lane_w & (gs - 1)) == hrow, 1.0, 0.0)
    rev_m = jnp.where(hrow == (tc - 1) - (lane_w >> 4), 1.0, 0.0)
    bbt_re = _dot(bb_re, tile_m, precision=HIGHEST)
    bbt_im = _dot(bb_im, tile_m, precision=HIGHEST)
    p15_re = _dot(pr_c, rev_m, precision=HIGHEST)
    p15_im = _dot(pi_c, rev_m, precision=HIGHEST)
    ws_ref[...] = jnp.concatenate([p15_re * bbt_re - p15_im * bbt_im,
                                    p15_re * bbt_im + p15_im * bbt_re], axis=0).astype(BF16)

    c_re = c_re_ref[...]
    c_im = c_im_ref[...]
    lo = lax.broadcasted_iota(jnp.int32, (1, 2 * p), 1) < p
    row_a = jnp.where(lo, pr, -pi)
    row_b = jnp.where(lo, -pi, -pr)
    wo_ref[...] = jnp.concatenate([c_re * row_a[t + 1:t + 2] + c_im * row_b[t + 1:t + 2]
                                    for t in range(tc)], axis=0).astype(BF16)

    x_re = jnp.concatenate([c_re * pr[k:k + 1] - c_im * pi[k:k + 1] for k in range(tc)], axis=0)
    x_im = jnp.concatenate([c_re * pi[k:k + 1] + c_im * pr[k:k + 1] for k in range(tc)], axis=0)
    zpad = jnp.zeros((p, w), F32)
    kt = (_dot(x_re, jnp.concatenate([bbt_re, zpad], axis=0), precision=HIGHEST)
          - _dot(x_im, jnp.concatenate([bbt_im, zpad], axis=0), precision=HIGHEST))
    sblk = lane_w >> 4
    rows = []
    for t in range(tc):
        acc = jnp.zeros((gs, w), F32)
        for k in range(t + 1):
            acc = jnp.where(sblk == t - k, kt[k * gs:(k + 1) * gs], acc)
        rows.append(acc)
    wi_ref[...] = jnp.concatenate(rows, axis=0).astype(BF16)

    r16 = lax.broadcasted_iota(jnp.int32, (16, 2 * p), 0)
    sr, si = lbpow(lax.shift_left(jnp.full((16, 2 * p), tc, jnp.int32), r16 >> 1).astype(F32))
    ap_ref[...] = jnp.where((r16 & 1) == 1, jnp.where(lo, -si, si), sr)


def _s5_tables(lam_re, lam_im, log_dt, b_re, b_im, c_re, c_im, d_skip):
    depth, g, p = lam_re.shape
    w = S5_CHUNK * S5_GROUP
    dup = lambda a: jnp.concatenate([a.astype(F32), a.astype(F32)], axis=-1)
    ldt = jnp.broadcast_to(log_dt.astype(F32)[:, :, None, None], (depth, g, 1, 2 * p))
    ng = S5_TAB_GROUPS
    row = pl.BlockSpec((1, ng, 1, 2 * p), lambda l, i: (l, i, 0, 0))
    bspec = pl.BlockSpec((1, ng, p, S5_GROUP), lambda l, i: (l, i, 0, 0))
    cspec = pl.BlockSpec((1, ng, S5_GROUP, 2 * p), lambda l, i: (l, i, 0, 0))
    out = lambda r, c: pl.BlockSpec((1, ng, r, c), lambda l, i: (l, i, 0, 0))
    tabs = pl.pallas_call(
        _s5_tab_kernel,
        grid=(depth, g // ng),
        in_specs=[row, row, row, bspec, bspec, cspec, cspec],
        out_specs=[out(w, w), out(2 * p, w), out(w, 2 * p), out(16, 2 * p)],
        out_shape=[jax.ShapeDtypeStruct((depth, g, w, w), BF16),
                   jax.ShapeDtypeStruct((depth, g, 2 * p, w), BF16),
                   jax.ShapeDtypeStruct((depth, g, w, 2 * p), BF16),
                   jax.ShapeDtypeStruct((depth, g, 16, 2 * p), F32)],
        compiler_params=_cparams(("arbitrary", "arbitrary")),
        name="s5_tables",
    )(dup(lam_re)[:, :, None, :], dup(lam_im)[:, :, None, :], ldt, b_re.astype(F32), b_im.astype(F32),
      dup(c_re), dup(c_im))
    d_row = d_skip.astype(F32).reshape(depth, 1, g * S5_GROUP)
    return (*tabs, d_row)


def _s5_pack_kernel(*refs):
    at_ref = refs[-1]
    nc = at_ref.shape[2]
    for t in range(S5_CHUNK):
        for j, u_ref in enumerate(refs[:-1]):
            at_ref[t, j * LANE:(j + 1) * LANE, :] = u_ref[pl.ds(t, nc, stride=S5_CHUNK), :].T.astype(BF16)


def _u_specs(seq):
    return [pl.BlockSpec((seq, LANE), lambda b, j=j: (b, CB_U + j)) for j in range(D_GROUP // LANE)]


def _s5_pack(proj, bsz, seq):
    nc = seq // S5_CHUNK
    return pl.pallas_call(
        _s5_pack_kernel,
        grid=(bsz,),
        in_specs=_u_specs(seq),
        out_specs=pl.BlockSpec((S5_CHUNK, D_GROUP, nc), lambda b: (0, 0, b)),
        out_shape=jax.ShapeDtypeStruct((S5_CHUNK, D_GROUP, bsz * nc), BF16),
        compiler_params=_cparams(("arbitrary",)),
        name="s5_pack",
    )(*([proj] * (D_GROUP // LANE)))


def _s5_kernel(ut_ref, wi_ref, ws_ref, wo_ref, ap_ref, yt_ref, *, n_chunk):
    tc, r = ut_ref.shape[0], ut_ref.shape[2]
    gs = S5_GROUP
    n_step = int(math.log2(n_chunk))
    for a in range(S5_CORE_GROUPS):
        ut = ut_ref[:, a * gs:(a + 1) * gs, :].reshape(tc * gs, r)
        s = _dot(ws_ref[a], ut)
        lane = lax.broadcasted_iota(jnp.int32, s.shape, 1) & (n_chunk - 1)
        apt = ap_ref[a].T
        x = s
        for k in range(n_step):
            sh = 1 << k
            xs = jnp.where(lane >= sh, pltpu.roll(x, sh, axis=1), 0.0)
            xsw = jnp.concatenate([xs[S5_P:], xs[:S5_P]], axis=0)
            x = x + xs * apt[:, 2 * k:2 * k + 1] + xsw * apt[:, 2 * k + 1:2 * k + 2]
        xp = jnp.where(lane >= 1, pltpu.roll(x, 1, axis=1), 0.0)
        y = _dot(wi_ref[a], ut) + _dot(wo_ref[a], xp.astype(BF16))
        yt_ref[:, a * gs:(a + 1) * gs, :] = y.reshape(tc, gs, r)


def _s5_core(at, tabs, n_chunk):
    tc, ch, r = at.shape
    w_intra_t, w_state_t, w_out_t, apow, _ = tabs
    g = w_intra_t.shape[0]
    w = tc * S5_GROUP
    ng = S5_CORE_GROUPS
    return pl.pallas_call(
        functools.partial(_s5_kernel, n_chunk=n_chunk),
        grid=(g // ng,),
        in_specs=[pl.BlockSpec((tc, ng * S5_GROUP, r), lambda i: (0, i, 0)),
                  pl.BlockSpec((ng, w, w), lambda i: (i, 0, 0)),
                  pl.BlockSpec((ng, 2 * S5_P, w), lambda i: (i, 0, 0)),
                  pl.BlockSpec((ng, w, 2 * S5_P), lambda i: (i, 0, 0)),
                  pl.BlockSpec((ng, 16, 2 * S5_P), lambda i: (i, 0, 0))],
        out_specs=pl.BlockSpec((tc, ng * S5_GROUP, r), lambda i: (0, i, 0)),
        out_shape=jax.ShapeDtypeStruct((tc, ch, r), F32),
        compiler_params=_cparams(("arbitrary",)),
        name="s5_core",
    )(at, w_intra_t, w_state_t, w_out_t, apow)


def _s5_out_kernel(yt_ref, u0_ref, u1_ref, u2_ref, u3_ref, d_ref, wg_ref, o_ref, y_scr):
    nc = yt_ref.shape[2]
    for t in range(S5_CHUNK):
        rows = pl.ds(t, nc, stride=S5_CHUNK)
        for j, u_ref in enumerate((u0_ref, u1_ref, u2_ref, u3_ref)):
            sl = slice(j * LANE, (j + 1) * LANE)
            y_scr[j, rows, :] = jax.nn.gelu(yt_ref[t, sl, :].T + d_ref[:, sl] * u_ref[rows, :])
    y = jnp.concatenate([y_scr[j] for j in range(D_GROUP // LANE)], axis=1)
    o_ref[...] = (y * jax.nn.sigmoid(_dot(y.astype(BF16), wg_ref[...]))).astype(BF16)


def _s5_out(yt, proj, d_row, w_glu_bf, bsz, seq):
    nc = seq // S5_CHUNK
    return pl.pallas_call(
        _s5_out_kernel,
        grid=(bsz,),
        in_specs=[pl.BlockSpec((S5_CHUNK, D_GROUP, nc), lambda b: (0, 0, b))] + _u_specs(seq)
        + [pl.BlockSpec((1, D_GROUP), lambda b: (0, 0)),
           pl.BlockSpec((D_GROUP, D_GROUP), lambda b: (0, 0))],
        out_specs=pl.BlockSpec((seq, D_GROUP), lambda b: (b, 0)),
        out_shape=jax.ShapeDtypeStruct((bsz * seq, D_GROUP), BF16),
        scratch_shapes=[pltpu.VMEM((D_GROUP // LANE, seq, LANE), F32)],
        compiler_params=_cparams(("arbitrary",)),
        name="s5_out",
    )(yt, proj, proj, proj, proj, d_row, w_glu_bf)


def _s5_mixer(proj, bsz, seq, tabs, w_glu_bf):
    yt = _s5_core(_s5_pack(proj, bsz, seq), tabs, seq // S5_CHUNK)
    return _s5_out(yt, proj, tabs[4], w_glu_bf, bsz, seq)


def _ret_tables():
    h, c = RET_HEADS, RET_CHUNK
    log_g = jnp.log(1.0 - 2.0 ** (-5.0 - jnp.arange(h, dtype=F32)))
    idx = jnp.arange(c, dtype=F32)
    diff = idx[:, None] - idx[None, :]
    dmat = jnp.where(diff >= 0, jnp.exp(jnp.maximum(diff, 0.0)[None] * log_g[:, None, None]), 0.0)
    zeta = jnp.exp((c - 1.0 - idx)[None, :] * log_g[:, None])
    xi = jnp.exp((idx + 1.0)[None, :] * log_g[:, None])
    g_chunk = jnp.exp(c * log_g)
    zeta_b = jnp.broadcast_to(zeta[:, :, None], (h, c, RET_DK))
    xi_b = jnp.broadcast_to(xi[:, :, None], (h, c, RET_DK))
    gch_b = jnp.broadcast_to(g_chunk[:, None, None], (h, 8, RET_DK))
    return dmat, zeta_b, xi_b, gch_b


def _ret_kernel(q_ref, k_ref, v_ref, g_ref, cos_ref, sin_ref, dm_ref, ze_ref, xi_ref, gc_ref,
                o_ref, *, n_chunk):
    c = RET_CHUNK
    dm = dm_ref[0]
    ze = ze_ref[0]
    xi = xi_ref[0]
    gch = gc_ref[0, 0:1, :]

    def body(n, r):
        sl = pl.ds(pl.multiple_of(n * c, c), c)
        cos = cos_ref[sl, :]
        sin = sin_ref[sl, :]
        qc = _rope128(q_ref[sl, :], cos, sin)
        kc = _rope128(k_ref[sl, :], cos, sin) * (RET_DK ** -0.5)
        vc = v_ref[sl, :].astype(BF16)
        inner = _dot_nt(qc.astype(BF16), kc.astype(BF16)) * dm
        o = _dot(inner.astype(BF16), vc) + _dot((qc * xi).astype(BF16), r.astype(BF16))
        kv = _dot_tn((kc * ze).astype(BF16), vc)
        r = r * gch + kv
        mu = jnp.mean(o, axis=-1, keepdims=True)
        oc = o - mu
        var = jnp.mean(oc * oc, axis=-1, keepdims=True)
        on = oc * lax.rsqrt(var + EPS)
        gg = g_ref[sl, :]
        o_ref[sl, :] = (gg * jax.nn.sigmoid(gg) * on).astype(BF16)
        return r

    lax.fori_loop(0, n_chunk, body, jnp.zeros((RET_DK, RET_DK), F32), unroll=RET_UNROLL)


def _retention(proj, cos128, sin128, tabs, bsz, seq):
    dmat, zeta_b, xi_b, gch_b = tabs
    h = RET_HEADS
    col = lambda cb: pl.BlockSpec((seq, LANE), lambda b, hh, cb=cb: (b, cb + hh))
    tab = pl.BlockSpec((seq, LANE), lambda b, hh: (b, 0))
    head = lambda r: pl.BlockSpec((1, r, LANE), lambda b, hh: (hh, 0, 0))
    return pl.pallas_call(
        functools.partial(_ret_kernel, n_chunk=seq // RET_CHUNK),
        grid=(bsz, h),
        in_specs=[col(CB_RQ), col(CB_RK), col(CB_RV), col(CB_RG), tab, tab,
                  head(RET_CHUNK), head(RET_CHUNK), head(RET_CHUNK), head(8)],
        out_specs=pl.BlockSpec((seq, LANE), lambda b, hh: (b, hh)),
        out_shape=jax.ShapeDtypeStruct((bsz * seq, D_GROUP), BF16),
        compiler_params=_cparams(("arbitrary", "arbitrary")),
        name="retention",
    )(proj, proj, proj, proj, cos128, sin128, dmat, zeta_b, xi_b, gch_b)


def _nsa_prep_kernel(kc_ref, vc_ref, pek_ref, pev_ref, wk_ref, wv_ref, ks_ref, kw_ref, cos_ref, sin_ref,
                     q_ref, g_ref, kcmp_ref, vcmp_ref, ksr_ref, kwr_ref, qt_ref, gt_ref):
    half = NSA_CMP_STRIDE * NSA_DH
    for h in range(NSA_HEADS):
        sl = slice(h * NSA_DH, (h + 1) * NSA_DH)
        qt_ref[sl, :] = _rope128(_rms_rows(q_ref[:, sl]), cos_ref[...], sin_ref[...]).T.astype(BF16)
    gt_ref[...] = jax.nn.sigmoid(g_ref[...]).T
    n16 = kcmp_ref.shape[1]
    kparts, vparts = [], []
    for i in range(NSA_CMP_STRIDE):
        rows = pl.ds(i, n16, stride=NSA_CMP_STRIDE)
        kparts.append(_rope128(kc_ref[rows, :], cos_ref[rows, :], sin_ref[rows, :]))
        vparts.append(vc_ref[rows, :])
    xr = jnp.concatenate(kparts, axis=1)
    xv = jnp.concatenate(vparts, axis=1)
    a = _dot((xr + pek_ref[0:1, :]).astype(BF16), wk_ref[0:half, :])
    b = _dot((xr + pek_ref[1:2, :]).astype(BF16), wk_ref[half:2 * half, :])
    kcm = _rms_rows(a + pltpu.roll(b, n16 - 1, axis=0))
    rowi = lax.broadcasted_iota(jnp.int32, kcm.shape, 0)
    kcmp_ref[0] = jnp.where(rowi < n16 - 1, kcm, 0.0).astype(BF16)
    av = _dot((xv + pev_ref[0:1, :]).astype(BF16), wv_ref[0:half, :])
    bv = _dot((xv + pev_ref[1:2, :]).astype(BF16), wv_ref[half:2 * half, :])
    vcm = av + pltpu.roll(bv, n16 - 1, axis=0)
    vcmp_ref[0] = jnp.where(rowi < n16 - 1, vcm, 0.0).astype(BF16)
    cos = cos_ref[...]
    sin = sin_ref[...]
    ksr_ref[...] = _rope128(_rms_rows(ks_ref[...]), cos, sin).astype(BF16)
    kwr_ref[...] = _rope128(_rms_rows(kw_ref[...]), cos, sin).astype(BF16)


def _nsa_prep(proj, cos128, sin128, pe_k, pe_v, wk_bf, wv_bf, bsz, seq):
    n16 = seq // NSA_CMP_STRIDE
    half = NSA_CMP_STRIDE * NSA_DH
    pek = pe_k.astype(F32).reshape(2, half)
    pev = pe_v.astype(F32).reshape(2, half)
    full = lambda a: pl.BlockSpec(a.shape, lambda b: (0,) * a.ndim)
    col = lambda cb: pl.BlockSpec((seq, LANE), lambda b, cb=cb: (b, cb))
    tab = pl.BlockSpec((seq, LANE), lambda b: (b, 0))
    cmp_spec = pl.BlockSpec((1, n16, NSA_DH), lambda b: (b, 0, 0))
    seq_spec = pl.BlockSpec((seq, LANE), lambda b: (b, 0))
    return pl.pallas_call(
        _nsa_prep_kernel,
        grid=(bsz,),
        in_specs=[col(CB_NKC), col(CB_NVC), full(pek), full(pev), full(wk_bf), full(wv_bf),
                  col(CB_NKS), col(CB_NKW), tab, tab,
                  pl.BlockSpec((seq, D_GROUP), lambda b: (b, CB_NQ // 4)), col(CB_GATE)],
        out_specs=[cmp_spec, cmp_spec, seq_spec, seq_spec,
                   pl.BlockSpec((D_GROUP, seq), lambda b: (0, b)), pl.BlockSpec((LANE, seq), lambda b: (0, b))],
        out_shape=[jax.ShapeDtypeStruct((bsz, n16, NSA_DH), BF16),
                   jax.ShapeDtypeStruct((bsz, n16, NSA_DH), BF16),
                   jax.ShapeDtypeStruct((bsz * seq, NSA_DH), BF16),
                   jax.ShapeDtypeStruct((bsz * seq, NSA_DH), BF16),
                   jax.ShapeDtypeStruct((D_GROUP, bsz * seq), BF16),
                   jax.ShapeDtypeStruct((LANE, bsz * seq), F32)],
        compiler_params=_cparams(("arbitrary",)),
        name="nsa_prep",
    )(proj, proj, pek, pev, wk_bf, wv_bf, proj, proj, cos128, sin128, proj, proj)


def _attn_finish(l, acc):
    return jnp.where(l > 0.0, acc / jnp.where(l > 0.0, l, 1.0), 0.0)


def _nsa_kernel(qt_ref, gt_ref, kc_ref, vc_ref, ks_ref, vs_ref, kw_ref, vw_ref, ov_ref, o_ref):
    hh, qb, dh = NSA_HEADS, Q_BLOCK, NSA_DH
    scale = dh ** -0.5
    n = pl.program_id(1)
    qq = jnp.concatenate([qt_ref[h * dh:(h + 1) * dh, :] for h in range(hh)], axis=1)

    def branch(keys, vals, maskf):
        s = _dot(keys, qq) * scale
        visible = maskf > 0.5
        e = jnp.concatenate([jnp.where(visible, jnp.exp(s[:, h * qb:(h + 1) * qb]), 0.0) for h in range(hh)],
                            axis=1)
        return jnp.sum(e, axis=0, keepdims=True), _dot_tn(vals, e.astype(BF16)), e

    def qpos(rows):
        return n * qb + lax.broadcasted_iota(jnp.int32, (rows, qb), 1)

    def krow(rows):
        return lax.broadcasted_iota(jnp.int32, (rows, qb), 0)

    ncp = kc_ref.shape[1]
    vis = jnp.where(krow(ncp) * NSA_CMP_STRIDE + (NSA_CMP_LEN - 1) <= qpos(ncp), 1.0, 0.0)
    l_c, acc_c, e_c = branch(kc_ref[0], vc_ref[0], vis)
    o_cmp = _attn_finish(l_c, acc_c)
    p = e_c / jnp.where(l_c > 0.0, l_c, 1.0)
    psum = p[:, 0:qb]
    for h in range(1, hh):
        psum = psum + p[:, h * qb:(h + 1) * qb]

    n_sel = ks_ref.shape[0] // NSA_SEL_LEN
    imp = _dot(ov_ref[...], psum, precision=HIGHEST)[:n_sel]
    j = krow(n_sel)
    cur = qpos(n_sel) >> NSA_SEL_SHIFT
    forced = (j == 0) | (j == cur) | (j == cur - 1)
    score = jnp.where(j <= cur, imp + jnp.where(forced, NSA_FORCE_BONUS, 0.0), NEG_INF)
    rank = jnp.zeros((n_sel, qb), F32)
    for jp in range(n_sel):
        cj = score[jp:jp + 1, :]
        tie = jnp.where(j > jp, 1.0, 0.0)
        rank = rank + jnp.where(cj > score, 1.0, jnp.where(cj == score, tie, 0.0))
    selm = jnp.where(rank < float(min(NSA_TOPK, n_sel)), 1.0, 0.0)
    selm = jnp.concatenate([selm, jnp.zeros((LANE - n_sel, qb), F32)], axis=0).astype(BF16)

    tk = NSA_SEL_TK
    blk = lax.broadcasted_iota(jnp.int32, (tk, LANE), 0) >> NSA_SEL_SHIFT
    selcol = lax.broadcasted_iota(jnp.int32, (tk, LANE), 1)

    def sel_tile(kt):
        sl = pl.ds(pl.multiple_of(kt * tk, tk), tk)
        expand = jnp.where(selcol == kt * (tk // NSA_SEL_LEN) + blk, 1.0, 0.0).astype(BF16)
        picked = _dot(expand, selm)
        maskf = jnp.where(kt * tk + krow(tk) <= qpos(tk), picked, 0.0)
        l, acc, _ = branch(ks_ref[sl, :], vs_ref[sl, :].astype(BF16), maskf)
        return l, acc

    def sel_body(i, carry):
        l0, a0 = sel_tile(2 * i)
        l1, a1 = sel_tile(2 * i + 1)
        return carry[0] + (l0 + l1), carry[1] + (a0 + a1)

    def sel_last(carry):
        l0, a0 = sel_tile(n_tiles - 1)
        return carry[0] + l0, carry[1] + a0

    zero = (jnp.zeros((1, hh * qb), F32), jnp.zeros((dh, hh * qb), F32))
    n_tiles = (n * qb + qb + tk - 1) // tk
    sums = lax.fori_loop(0, n_tiles >> 1, sel_body, zero)
    o_sel = _attn_finish(*lax.cond((n_tiles & 1) == 1, sel_last, lambda c: c, sums))

    wk = NSA_WINDOW + qb
    start = jnp.maximum(n * qb - NSA_WINDOW, 0)
    sl = pl.ds(pl.multiple_of(start, qb), wk)
    dist = qpos(wk) - (start + krow(wk))
    inwin = jnp.where(dist >= 0, jnp.where(dist < NSA_WINDOW, 1.0, 0.0), 0.0)
    l_w, acc_w, _ = branch(kw_ref[sl, :], vw_ref[sl, :].astype(BF16), inwin)
    o_win = _attn_finish(l_w, acc_w)

    gt = gt_ref[...]
    for h in range(hh):
        hq = slice(h * qb, (h + 1) * qb)
        o = (gt[3 * h:3 * h + 1, :] * o_cmp[:, hq] + gt[3 * h + 1:3 * h + 2, :] * o_sel[:, hq]
             + gt[3 * h + 2:3 * h + 3, :] * o_win[:, hq])
        o_ref[:, h * dh:(h + 1) * dh] = o.T.astype(BF16)


def _nsa_overlap_table(seq):
    n_cmp = (seq - NSA_CMP_LEN) // NSA_CMP_STRIDE + 1
    n_sel = seq // NSA_SEL_LEN
    cmp_start = np.arange(n_cmp) * NSA_CMP_STRIDE
    sel_start = np.arange(n_sel) * NSA_SEL_LEN
    overlap = ((cmp_start[None, :] < sel_start[:, None] + NSA_SEL_LEN)
               & (cmp_start[None, :] + NSA_CMP_LEN > sel_start[:, None])).astype(np.float32)
    tab = np.zeros((LANE, n_cmp + 1), np.float32)
    tab[:n_sel, :n_cmp] = overlap
    return jnp.asarray(tab)


def _nsa_attn(proj, qt, gt, kcmp, vcmp, ksr, kwr, bsz, seq):
    nb = seq // Q_BLOCK
    cmp_spec = pl.BlockSpec((1, kcmp.shape[1], NSA_DH), lambda b, n: (b, 0, 0))
    kseq = pl.BlockSpec((seq, LANE), lambda b, n: (b, 0))
    vcol = lambda cb: pl.BlockSpec((seq, LANE), lambda b, n, cb=cb: (b, cb))
    ov = _nsa_overlap_table(seq)
    return pl.pallas_call(
        _nsa_kernel,
        grid=(bsz, nb),
        in_specs=[pl.BlockSpec((D_GROUP, Q_BLOCK), lambda b, n: (0, b * nb + n)),
                  pl.BlockSpec((LANE, Q_BLOCK), lambda b, n: (0, b * nb + n)),
                  cmp_spec, cmp_spec, kseq, vcol(CB_NVS), kseq, vcol(CB_NVW),
                  pl.BlockSpec(ov.shape, lambda b, n: (0, 0))],
        out_specs=pl.BlockSpec((Q_BLOCK, D_GROUP), lambda b, n: (b * nb + n, 0)),
        out_shape=jax.ShapeDtypeStruct((bsz * seq, D_GROUP), BF16),
        compiler_params=_cparams(("arbitrary", "arbitrary")),
        name="nsa_attn",
    )(qt, gt, kcmp, vcmp, ksr, proj, kwr, proj, ov)


def _diff_prep_kernel(q_ref, k_ref, cos_ref, sin_ref, qt_ref, ko_ref):
    cos = cos_ref[...]
    sin = sin_ref[...]
    lane = lax.broadcasted_iota(jnp.int32, cos.shape, 1)
    lo = lane < DIFF_DH
    first = (lane & (DIFF_DH - 1)) < DIFF_DH // 2

    def prep(x):
        x2 = x * x
        ss_lo = jnp.sum(jnp.where(lo, x2, 0.0), axis=-1, keepdims=True)
        ss_hi = jnp.sum(jnp.where(lo, 0.0, x2), axis=-1, keepdims=True)
        ms = jnp.where(lo, ss_lo, ss_hi) * (1.0 / DIFF_DH)
        xn = x * lax.rsqrt(ms + EPS)
        partner = jnp.where(first, pltpu.roll(xn, LANE - DIFF_DH // 2, axis=1),
                            pltpu.roll(xn, DIFF_DH // 2, axis=1))
        return xn * cos + partner * sin

    for h in range(DIFF_HEADS):
        sl = slice(h * LANE, (h + 1) * LANE)
        qt_ref[sl, :] = (prep(q_ref[:, sl]) * (DIFF_DH ** -0.5)).T.astype(BF16)
        ko_ref[:, sl] = prep(k_ref[:, sl]).astype(BF16)


def _diff_prep(proj, cos64, sin64, tm):
    t = proj.shape[0]
    blk = lambda cb: pl.BlockSpec((tm, D_GROUP), lambda i, cb=cb: (i, cb // 4))
    tab = pl.BlockSpec((tm, LANE), lambda i: (i, 0))
    return pl.pallas_call(
        _diff_prep_kernel,
        grid=(t // tm,),
        in_specs=[blk(CB_DQ), blk(CB_DK), tab, tab],
        out_specs=[pl.BlockSpec((D_GROUP, tm), lambda i: (0, i)),
                   pl.BlockSpec((tm, D_GROUP), lambda i: (i, 0))],
        out_shape=[jax.ShapeDtypeStruct((D_GROUP, t), BF16), jax.ShapeDtypeStruct((t, D_GROUP), BF16)],
        compiler_params=_cparams(("arbitrary",)),
        name="diff_prep",
    )(proj, proj, cos64, sin64)


def _diff_kernel(lam_ref, qt_ref, k_ref, v_ref, o_ref, acc_ref, l_ref, *, out_scale):
    nq = DIFF_TQ
    qi = pl.program_id(2)
    qt = qt_ref[...]
    row = lax.broadcasted_iota(jnp.int32, qt.shape, 0)
    zero = jnp.zeros_like(qt)
    qq = jnp.concatenate([jnp.where(row < DIFF_DH, qt, zero), jnp.where(row < DIFF_DH, zero, qt)], axis=1)
    acc_ref[...] = jnp.zeros_like(acc_ref)
    l_ref[...] = jnp.zeros_like(l_ref)

    tk = DIFF_TK
    per_q = nq // tk

    def tile(kt, masked, lo=0):
        sl = pl.ds(pl.multiple_of(kt * tk, tk), tk)
        wq = nq - lo
        qs = qq if lo == 0 else jnp.concatenate([qq[:, lo:nq], qq[:, nq + lo:]], axis=1)
        p = jnp.exp(_dot(k_ref[sl, :], qs))
        if masked:
            kpos = kt * tk + lax.broadcasted_iota(jnp.int32, p.shape, 0)
            qpos = qi * nq + lo + (lax.broadcasted_iota(jnp.int32, p.shape, 1) & (wq - 1))
            p = jnp.where(kpos <= qpos, p, 0.0)
        return jnp.sum(p, axis=0, keepdims=True), _dot_tn(v_ref[sl, :].astype(BF16), p.astype(BF16))

    def body(i, c):
        l0, a0 = tile(2 * i, False)
        l1, a1 = tile(2 * i + 1, False)
        l_ref[...] += l0 + l1
        acc_ref[...] += a0 + a1
        return c

    lax.fori_loop(0, qi * (per_q // 2), body, 0)
    for d in range(per_q):
        lo = d * tk
        l0, a0 = tile(qi * per_q + d, True, lo)
        for m in range(2):
            dst = slice(m * nq + lo, (m + 1) * nq)
            src = slice(m * (nq - lo), (m + 1) * (nq - lo))
            l_ref[:, dst] += l0[:, src]
            acc_ref[:, dst] += a0[:, src]
    l = l_ref[...]
    acc = acc_ref[...]
    o = acc[:, :nq] / l[:, :nq] - lam_ref[0, 0] * (acc[:, nq:] / l[:, nq:])
    on = o * lax.rsqrt(jnp.mean(o * o, axis=0, keepdims=True) + EPS) * out_scale
    o_ref[...] = on.T.astype(BF16)


def _diff_attn(lam, qt, kd, proj, bsz, seq, lam_init):
    nq = DIFF_TQ
    nb = seq // nq
    h = DIFF_HEADS
    return pl.pallas_call(
        functools.partial(_diff_kernel, out_scale=1.0 - lam_init),
        grid=(bsz, h, nb),
        in_specs=[pl.BlockSpec(memory_space=pltpu.SMEM),
                  pl.BlockSpec((LANE, nq), lambda b, hh, n: (hh, b * nb + n)),
                  pl.BlockSpec((seq, LANE), lambda b, hh, n: (b, hh)),
                  pl.BlockSpec((seq, LANE), lambda b, hh, n: (b, CB_DV + hh))],
        out_specs=pl.BlockSpec((nq, LANE), lambda b, hh, n: (b * nb + n, hh)),
        out_shape=jax.ShapeDtypeStruct((bsz * seq, D_GROUP), BF16),
        scratch_shapes=[pltpu.VMEM((LANE, 2 * nq), F32), pltpu.VMEM((1, 2 * nq), F32)],
        compiler_params=_cparams(("arbitrary", "arbitrary", "arbitrary")),
        name="diff_attn",
    )(lam, qt, kd, proj)


def _outproj_kernel(ya_ref, yb_ref, yc_ref, yd_ref, w_ref, x_ref, g_ref, xo_ref, h_ref):
    mixed = jnp.concatenate([ya_ref[...], yb_ref[...], yc_ref[...], yd_ref[...]], axis=1)
    x = x_ref[...] + _dot(mixed, w_ref[0])
    xo_ref[...] = x
    h_ref[...] = (_rms_rows(x) * g_ref[...]).astype(BF16)


def _outproj(ys, w3_bf, layer, x2, g, tm):
    t, d = x2.shape
    yspec = pl.BlockSpec((tm, D_GROUP), lambda i: (i, 0))
    row = pl.BlockSpec((tm, d), lambda i: (i, 0))
    return pl.pallas_call(
        _outproj_kernel,
        grid=(t // tm,),
        in_specs=[yspec, yspec, yspec, yspec,
                  pl.BlockSpec((1, d, d), lambda i: (layer, 0, 0)), row,
                  pl.BlockSpec((1, d), lambda i: (0, 0))],
        out_specs=[row, row],
        out_shape=[jax.ShapeDtypeStruct((t, d), F32), jax.ShapeDtypeStruct((t, d), BF16)],
        compiler_params=_cparams(("arbitrary",)),
        name="outproj",
    )(*ys, w3_bf, x2, g)


def _mlp_kernel(h_ref, x_ref, w1_ref, w2_ref, o_ref):
    @pl.when(pl.program_id(1) == 0)
    def _():
        o_ref[...] = x_ref[...]

    a = _dot(h_ref[...], w1_ref[0].astype(BF16))
    a = jnp.square(jnp.maximum(a, 0.0)).astype(BF16)
    o_ref[...] += _dot(a, w2_ref[0].astype(BF16))


def _mlp(h2, x2, w1, w2, layer, tm, tf):
    t, d = x2.shape
    f = w1.shape[2]
    return pl.pallas_call(
        _mlp_kernel,
        grid=(t // tm, f // tf),
        in_specs=[pl.BlockSpec((tm, d), lambda i, j: (i, 0)),
                  pl.BlockSpec((tm, d), lambda i, j: (i, 0), pipeline_mode=pl.Buffered(1)),
                  pl.BlockSpec((1, d, tf), lambda i, j: (layer, 0, j)),
                  pl.BlockSpec((1, tf, d), lambda i, j: (layer, j, 0))],
        out_specs=pl.BlockSpec((tm, d), lambda i, j: (i, 0)),
        out_shape=jax.ShapeDtypeStruct((t, d), F32),
        compiler_params=_cparams(("arbitrary", "arbitrary")),
        name="mlp",
    )(h2, x2, w1, w2)


def _rope_tables(positions):
    bsz, seq = positions.shape
    pos = positions.astype(F32)[..., None]

    def tab(d):
        inv = ROPE_THETA ** (-jnp.arange(0, d, 2, dtype=F32) / d)
        ang = pos * inv
        return jnp.cos(ang), jnp.sin(ang)

    c, s = tab(NSA_DH)
    cos128 = jnp.concatenate([c, c], axis=-1).reshape(bsz * seq, LANE)
    sin128 = jnp.concatenate([-s, s], axis=-1).reshape(bsz * seq, LANE)
    c, s = tab(DIFF_DH)
    cos64 = jnp.concatenate([c, c, c, c], axis=-1).reshape(bsz * seq, LANE)
    sin64 = jnp.concatenate([-s, s, -s, s], axis=-1).reshape(bsz * seq, LANE)
    return cos128, sin128, cos64, sin64


def kernel(x, positions, norm1_g, w_in, s5_lambda_re, s5_lambda_im, s5_log_dt, s5_b_re, s5_b_im, s5_c_re, s5_c_im, s5_d, s5_w_glu, nsa_pe_k, nsa_pe_v, nsa_w_cmp_k, nsa_w_cmp_v, diff_lq1, diff_lk1, diff_lq2, diff_lk2, w_out, norm2_g, mlp_w1, mlp_w2):
    bsz, seq, d = x.shape
    t = bsz * seq
    depth = w_in.shape[0]
    tm = min(ROW_TILE, t)
    cos128, sin128, cos64, sin64 = _rope_tables(positions)
    ret_tabs = _ret_tables()
    x2 = x.reshape(t, d).astype(F32)
    w_in = w_in.astype(BF16)
    w_out_bf = w_out.astype(BF16)
    w_glu_bf = s5_w_glu.astype(BF16)
    wk_bf = nsa_w_cmp_k.astype(BF16)
    wv_bf = nsa_w_cmp_v.astype(BF16)
    s5_tabs = _s5_tables(s5_lambda_re, s5_lambda_im, s5_log_dt, s5_b_re, s5_b_im, s5_c_re, s5_c_im, s5_d)
    for layer in range(depth):
        g1 = norm1_g[layer].astype(F32)[None, :]
        proj = _inproj(x2, g1, w_in, w_in[layer:layer + 1, :, GATE_RAW_END:], layer, min(IN_TM, t), IN_TN)

        y_a = _s5_mixer(proj, bsz, seq, tuple(a[layer] for a in s5_tabs), w_glu_bf[layer])

        y_b = _retention(proj, cos128, sin128, ret_tabs, bsz, seq)

        kcmp, vcmp, ksr, kwr, nqt, ngt = _nsa_prep(proj, cos128, sin128, nsa_pe_k[layer], nsa_pe_v[layer],
                                                   wk_bf[layer], wv_bf[layer], bsz, seq)
        y_c = _nsa_attn(proj, nqt, ngt, kcmp, vcmp, ksr, kwr, bsz, seq)

        lam_init = 0.8 - 0.6 * math.exp(-0.3 * layer)
        lam = (jnp.exp(jnp.sum(diff_lq1[layer].astype(F32) * diff_lk1[layer].astype(F32)))
               - jnp.exp(jnp.sum(diff_lq2[layer].astype(F32) * diff_lk2[layer].astype(F32))) + lam_init)
        qt, kd = _diff_prep(proj, cos64, sin64, tm)
        y_d = _diff_attn(lam.reshape(1, 1).astype(F32), qt, kd, proj, bsz, seq, lam_init)

        x2, h2 = _outproj((y_a, y_b, y_c, y_d), w_out_bf, layer, x2,
                          norm2_g[layer].astype(F32)[None, :], min(OUT_TM, t))
        x2 = _mlp(h2, x2, mlp_w1.astype(F32), mlp_w2.astype(F32), layer, tm, MLP_TF)
    return x2.reshape(bsz, seq, d).astype(x.dtype)
```
